```python
import math
import jax, jax.numpy as jnp
from jax import lax
import numpy as np

D_MODEL = 2048
BATCH = 2
SEQ = 4096
DEPTH = 1
DEC_BATCH = 128
DEC_SEQ = 8
PAST_LEN = 8192
PAGE_SIZE = 128

MIX_WIDTH = D_MODEL
GLA_WIDTH = MIX_WIDTH // 2
SWA_WIDTH = MIX_WIDTH - GLA_WIDTH
GLA_HEADS = 4
GLA_DK = (GLA_WIDTH // 2) // GLA_HEADS
GLA_DV = GLA_WIDTH // GLA_HEADS
GLA_GATE_RANK = 16
GLA_TAU = 16.0
GLA_CHUNK = 64
SWA_HEAD_DIM = 64
SWA_HEADS = SWA_WIDTH // SWA_HEAD_DIM
SWA_KV_HEADS = 2
SWA_GROUP = SWA_HEADS // SWA_KV_HEADS
WINDOW = 128
SWA_BLOCK = WINDOW
ROPE_THETA = 10000.0
PEER_N_KEYS = 128
PEER_EXPERTS = PEER_N_KEYS * PEER_N_KEYS
PEER_HEADS = 8
PEER_QDIM = 128
PEER_HALF = PEER_QDIM // 2
PEER_TOPK = 16
PEER_TOKEN_BLOCK = 128
EPS = 1e-6
NEG_INF = -1e30
PROJ_SPLITS = (GLA_HEADS * GLA_DK,
               GLA_HEADS * GLA_DK,
               GLA_WIDTH,
               GLA_WIDTH,
               GLA_GATE_RANK,
               SWA_WIDTH,
               SWA_KV_HEADS * SWA_HEAD_DIM,
               SWA_KV_HEADS * SWA_HEAD_DIM)
PROJ_WIDTH = sum(PROJ_SPLITS)

kernel_name = 'hymba_gla_swasink_peer_step'


def rms_norm(x, g):
    xf = x.astype(jnp.float32)
    y = xf * lax.rsqrt(jnp.mean(xf * xf, axis=-1, keepdims=True) + EPS)
    return (y * g.astype(jnp.float32)).astype(x.dtype)


def split_cols(p):
    out = []
    start = 0
    for w in PROJ_SPLITS:
        out.append(p[..., start:start + w])
        start += w
    return out


def rope(x, pos):
    hd = x.shape[-1]
    half = hd // 2
    inv = jnp.exp(-math.log(ROPE_THETA) * jnp.arange(half, dtype=jnp.float32) * (2.0 / hd))
    ang = pos.astype(jnp.float32)[:, None] * inv[None, :]
    cos = jnp.cos(ang)[:, None, :]
    sin = jnp.sin(ang)[:, None, :]
    xf = x.astype(jnp.float32)
    x1, x2 = xf[..., :half], xf[..., half:]
    return jnp.concatenate([x1 * cos - x2 * sin, x2 * cos + x1 * sin], axis=-1).astype(x.dtype)


def sink_softmax(s, mask, sink):
    s = jnp.where(mask, s, NEG_INF)
    m = jnp.maximum(jnp.max(s, axis=-1, keepdims=True), sink)
    p = jnp.exp(s - m)
    return p / (jnp.sum(p, axis=-1, keepdims=True) + jnp.exp(sink - m))


def gla_chunked(q, k, v, log_a):
    B, S, H, dk = q.shape
    dv = v.shape[-1]
    C = GLA_CHUNK
    N = S // C

    def blk(t):
        return t.reshape(B, N, C, H, t.shape[-1]).transpose(0, 3, 1, 2, 4)

    q, k, v, log_a = blk(q), blk(k), blk(v), blk(log_a)
    b = jnp.cumsum(log_a, axis=3)
    b_last = b[:, :, :, -1:, :]
    q_t = q * jnp.exp(b)
    k_t = k * jnp.exp(-b)
    k_e = k * jnp.exp(b_last - b)
    causal = jnp.tril(jnp.ones((C, C), dtype=bool))
    a_intra = jnp.where(causal, jnp.einsum('bhncd,bhnsd->bhncs', q_t, k_t), 0.0)
    o_intra = jnp.einsum('bhncs,bhnsv->bhncv', a_intra, v)
    d_state = jnp.einsum('bhncd,bhncv->bhndv', k_e, v)
    decay = jnp.exp(b_last[:, :, :, 0, :])

    def step(s_c, inp):
        dec, ds = inp
        return dec[..., None] * s_c + ds, s_c

    s0 = jnp.zeros((B, H, dk, dv), jnp.float32)
    s_fin, s_in = lax.scan(step, s0, (jnp.moveaxis(decay, 2, 0), jnp.moveaxis(d_state, 2, 0)))
    s_in = jnp.moveaxis(s_in, 0, 2)
    o_inter = jnp.einsum('bhncd,bhndv->bhncv', q_t, s_in)
    o = (o_intra + o_inter).transpose(0, 2, 3, 1, 4).reshape(B, S, H, dv)
    return o, s_fin


def gla_recurrent(q, k, v, log_a, s0):
    def step(s, inp):
        qt, kt, vt, gt = inp
        s = jnp.exp(gt)[..., None] * s + kt[..., :, None] * vt[..., None, :]
        return s, jnp.einsum('bhd,bhdv->bhv', qt, s)

    xs = tuple(jnp.moveaxis(t, 1, 0) for t in (q, k, v, log_a))
    s_fin, o = lax.scan(step, s0.astype(jnp.float32), xs)
    return jnp.moveaxis(o, 0, 1), s_fin


def gla_output(o, r, g_norm):
    B, S = o.shape[:2]
    o = o * lax.rsqrt(jnp.mean(o * o, axis=-1, keepdims=True) + EPS) * g_norm.astype(jnp.float32)
    o = o.reshape(B, S, GLA_WIDTH) * jax.nn.silu(r.astype(jnp.float32))
    return o.astype(r.dtype)


def swa_prompt(q, k, v, sinks):
    B, S, _, hd = q.shape
    L = SWA_BLOCK
    NB = S // L
    qb = q.reshape(B, NB, L, SWA_KV_HEADS, SWA_GROUP, hd).astype(jnp.float32)
    kb = k.reshape(B, NB, L, SWA_KV_HEADS, hd).astype(jnp.float32)
    vb = v.reshape(B, NB, L, SWA_KV_HEADS, hd).astype(jnp.float32)
    pad = ((0, 0), (1, 0), (0, 0), (0, 0), (0, 0))
    kk = jnp.concatenate([jnp.pad(kb, pad)[:, :NB], kb], axis=2)
    vv = jnp.concatenate([jnp.pad(vb, pad)[:, :NB], vb], axis=2)
    diff = (jnp.arange(L)[:, None] + L) - jnp.arange(2 * L)[None, :]
    band = (diff >= 0) & (diff < WINDOW)
    has_prev = (jnp.arange(NB)[:, None] > 0) | (jnp.arange(2 * L)[None, :] >= L)
    mask = (band[None, :, :] & has_prev[:, None, :])[None, :, None, None]
    s = jnp.einsum('bnqkgd,bnskd->bnkgqs', qb, kk) * (hd ** -0.5)
    sink = sinks.astype(jnp.float32).reshape(SWA_KV_HEADS, SWA_GROUP)[:, :, None, None]
    p = sink_softmax(s, mask, sink)
    o = jnp.einsum('bnkgqs,bnskd->bnqkgd', p, vv)
    return o.reshape(B, S, SWA_WIDTH).astype(q.dtype)


def swa_sample(q, k, v, cache_k, cache_v, sinks):
    B, T, _, hd = q.shape
    W = cache_k.shape[1]
    kk = jnp.concatenate([cache_k.astype(k.dtype), k], axis=1)
    vv = jnp.concatenate([cache_v.astype(v.dtype), v], axis=1)
    qpos = PAST_LEN + jnp.arange(T)
    kpos = jnp.concatenate([PAST_LEN - W + jnp.arange(W), PAST_LEN + jnp.arange(T)])
    diff = qpos[:, None] - kpos[None, :]
    mask = (diff >= 0) & (diff < WINDOW)
    qf = q.reshape(B, T, SWA_KV_HEADS, SWA_GROUP, hd).astype(jnp.float32)
    s = jnp.einsum('btkgd,bskd->bkgts', qf, kk.astype(jnp.float32)) * (hd ** -0.5)
    sink = sinks.astype(jnp.float32).reshape(SWA_KV_HEADS, SWA_GROUP)[:, :, None, None]
    p = sink_softmax(s, mask, sink)
    o = jnp.einsum('bkgts,bskd->btkgd', p, vv.astype(jnp.float32))
    return o.reshape(B, T, SWA_WIDTH).astype(q.dtype), kk[:, -W:], vv[:, -W:]


def peer(xn, w_q, sub_keys, u, v):
    T, D = xn.shape
    TB = PEER_TOKEN_BLOCK
    pad = (-T) % TB
    xb = jnp.pad(xn, ((0, pad), (0, 0))).reshape(-1, TB, D)
    K = PEER_TOPK

    def block(xt):
        q = (xt @ w_q).reshape(TB, PEER_HEADS, 2, PEER_HALF)
        s = jnp.einsum('thpd,hpnd->thpn', q, sub_keys).astype(jnp.float32)
        sv, si = lax.top_k(s, K)
        cand = sv[:, :, 0, :, None] + sv[:, :, 1, None, :]
        cand_idx = si[:, :, 0, :, None] * PEER_N_KEYS + si[:, :, 1, None, :]
        best, pos = lax.top_k(cand.reshape(TB, PEER_HEADS, K * K), K)
        idx = jnp.take_along_axis(cand_idx.reshape(TB, PEER_HEADS, K * K), pos, axis=-1)
        gate = jax.nn.softmax(best, axis=-1).reshape(TB, PEER_HEADS * K)
        idx = idx.reshape(TB, PEER_HEADS * K)
        u_e = u[idx]
        v_e = v[idx]
        act = jax.nn.gelu(jnp.einsum('td,ted->te', xt, u_e).astype(jnp.float32), approximate=False)
        coef = (gate * act).astype(xt.dtype)
        return jnp.einsum('te,ted->td', coef, v_e)

    return lax.map(block, xb).reshape(-1, D)[:T]


def mix_inputs(x, pos, g_attn, w_in, w_gate_up, b_gate):
    B, S, _ = x.shape
    xn = rms_norm(x, g_attn)
    gq, gk, gv, gr, ga, sq, sk, sv = split_cols(xn @ w_in)
    q = gq.astype(jnp.float32).reshape(B, S, GLA_HEADS, GLA_DK) * (GLA_DK ** -0.5)
    k = gk.astype(jnp.float32).reshape(B, S, GLA_HEADS, GLA_DK)
    v = gv.astype(jnp.float32).reshape(B, S, GLA_HEADS, GLA_DV)
    log_a = jax.nn.log_sigmoid((ga @ w_gate_up + b_gate).astype(jnp.float32)).reshape(
        B, S, GLA_HEADS, GLA_DK) / GLA_TAU
    sq = rope(sq.reshape(B, S, SWA_HEADS, SWA_HEAD_DIM), pos)
    sk = rope(sk.reshape(B, S, SWA_KV_HEADS, SWA_HEAD_DIM), pos)
    sv = sv.reshape(B, S, SWA_KV_HEADS, SWA_HEAD_DIM)
    return q, k, v, log_a, gr, sq, sk, sv


def merge_and_ffn(x, o_gla, gr, o_swa, g_gla_out, w_out, g_ffn, w_peer_q, peer_sub_keys, peer_u, peer_v):
    B, S, D = x.shape
    heads = jnp.concatenate([gla_output(o_gla, gr, g_gla_out), o_swa], axis=-1)
    h = x + heads @ w_out
    hn = rms_norm(h, g_ffn).reshape(B * S, D)
    return h + peer(hn, w_peer_q, peer_sub_keys, peer_u, peer_v).reshape(B, S, D)


def layer_prompt(x, g_attn, w_in, w_gate_up, b_gate, g_gla_out, attn_sinks, w_out,
                 g_ffn, w_peer_q, peer_sub_keys, peer_u, peer_v):
    pos = jnp.arange(x.shape[1])
    q, k, v, log_a, gr, sq, sk, sv = mix_inputs(x, pos, g_attn, w_in, w_gate_up, b_gate)
    o_gla, s_fin = gla_chunked(q, k, v, log_a)
    o_swa = swa_prompt(sq, sk, sv, attn_sinks)
    out = merge_and_ffn(x, o_gla, gr, o_swa, g_gla_out, w_out, g_ffn, w_peer_q, peer_sub_keys, peer_u, peer_v)
    return out, s_fin.astype(x.dtype), sk[:, -WINDOW:], sv[:, -WINDOW:]


def layer_sample(x, s_gla, c_k, c_v, g_attn, w_in, w_gate_up, b_gate, g_gla_out, attn_sinks, w_out,
                 g_ffn, w_peer_q, peer_sub_keys, peer_u, peer_v):
    pos = PAST_LEN + jnp.arange(x.shape[1])
    q, k, v, log_a, gr, sq, sk, sv = mix_inputs(x, pos, g_attn, w_in, w_gate_up, b_gate)
    o_gla, s_fin = gla_recurrent(q, k, v, log_a, s_gla)
    o_swa, new_k, new_v = swa_sample(sq, sk, sv, c_k, c_v, attn_sinks)
    out = merge_and_ffn(x, o_gla, gr, o_swa, g_gla_out, w_out, g_ffn, w_peer_q, peer_sub_keys, peer_u, peer_v)
    return out, s_fin.astype(s_gla.dtype), new_k.astype(c_k.dtype), new_v.astype(c_v.dtype)


def setup_inputs(seed: int = 0) -> dict:
    key = jax.random.key(seed)
    ks = jax.random.split(key, 20)
    f = jnp.float32
    n = lambda i, shape, scale: jax.random.normal(ks[i], shape, f) * scale
    win_c = min(WINDOW, PAST_LEN)
    return {
        'x_prompt': n(0, (BATCH, SEQ, D_MODEL), 1.0),
        'x_sample': n(1, (DEC_BATCH, DEC_SEQ, D_MODEL), 1.0),
        'state_gla': n(2, (DEPTH, DEC_BATCH, GLA_HEADS, GLA_DK, GLA_DV), 0.5),
        'cache_win_k': n(3, (DEPTH, DEC_BATCH, win_c, SWA_KV_HEADS, SWA_HEAD_DIM), 1.0),
        'cache_win_v': n(4, (DEPTH, DEC_BATCH, win_c, SWA_KV_HEADS, SWA_HEAD_DIM), 1.0),
        'g_attn': 1.0 + n(5, (DEPTH, D_MODEL), 0.02),
        'w_in': n(6, (DEPTH, D_MODEL, PROJ_WIDTH), D_MODEL ** -0.5),
        'w_gate_up': n(7, (DEPTH, GLA_GATE_RANK, GLA_HEADS * GLA_DK), GLA_GATE_RANK ** -0.5),
        'b_gate': n(8, (DEPTH, GLA_HEADS * GLA_DK), 0.1),
        'g_gla_out': 1.0 + n(9, (DEPTH, GLA_DV), 0.02),
        'attn_sinks': n(10, (DEPTH, SWA_HEADS), 0.5),
        'w_out': n(11, (DEPTH, MIX_WIDTH, D_MODEL), MIX_WIDTH ** -0.5),
        'g_ffn': 1.0 + n(12, (DEPTH, D_MODEL), 0.02),
        'w_peer_q': n(13, (DEPTH, D_MODEL, PEER_HEADS * PEER_QDIM), D_MODEL ** -0.5),
        'peer_sub_keys': n(14, (DEPTH, PEER_HEADS, 2, PEER_N_KEYS, PEER_HALF), PEER_HALF ** -0.5),
        'peer_u': n(15, (DEPTH, PEER_EXPERTS, D_MODEL), D_MODEL ** -0.5),
        'peer_v': n(16, (DEPTH, PEER_EXPERTS, D_MODEL), PEER_HEADS ** -0.5),
        'g_final': 1.0 + n(17, (D_MODEL,), 0.02),
    }


def reference(x_prompt, x_sample, state_gla, cache_win_k, cache_win_v, g_attn, w_in, w_gate_up, b_gate,
              g_gla_out, attn_sinks, w_out, g_ffn, w_peer_q, peer_sub_keys, peer_u, peer_v, g_final):
    xp, xs = x_prompt, x_sample
    sp_l, kp_l, vp_l, ss_l, ksl, vsl = [], [], [], [], [], []
    for l in range(DEPTH):
        p = (g_attn[l], w_in[l], w_gate_up[l], b_gate[l], g_gla_out[l], attn_sinks[l], w_out[l],
             g_ffn[l], w_peer_q[l], peer_sub_keys[l], peer_u[l], peer_v[l])
        xp, sp, kp, vp = layer_prompt(xp, *p)
        xs, ss, k_s, v_s = layer_sample(xs, state_gla[l], cache_win_k[l], cache_win_v[l], *p)
        sp_l.append(sp); kp_l.append(kp); vp_l.append(vp)
        ss_l.append(ss); ksl.append(k_s); vsl.append(v_s)
    y_prompt = rms_norm(xp, g_final)
    y_sample = rms_norm(xs, g_final)
    return (y_prompt, y_sample, jnp.stack(sp_l), jnp.stack(kp_l), jnp.stack(vp_l),
            jnp.stack(ss_l), jnp.stack(ksl), jnp.stack(vsl))
```

```python
import functools
import math

import jax
import jax.numpy as jnp
from jax import lax
from jax.experimental import pallas as pl
from jax.experimental.pallas import tpu as pltpu

F32 = jnp.float32
BF16 = jnp.bfloat16
HIGHEST = lax.Precision.HIGHEST

D_MODEL = 2048
GLA_HEADS = 4
GLA_DK = 128
GLA_DV = 256
GLA_WIDTH = GLA_HEADS * GLA_DV
GLA_GATE_RANK = 16
GLA_TAU = 16.0
GLA_CHUNK = 64
SWA_HEAD_DIM = 64
SWA_HEADS = 16
SWA_KV_HEADS = 2
SWA_GROUP = SWA_HEADS // SWA_KV_HEADS
SWA_WIDTH = SWA_HEADS * SWA_HEAD_DIM
WINDOW = 128
ROPE_THETA = 10000.0
PAST_LEN = 8192
PEER_N_KEYS = 128
PEER_EXPERTS = PEER_N_KEYS * PEER_N_KEYS
PEER_HEADS = 8
PEER_QDIM = 128
PEER_HALF = PEER_QDIM // 2
PEER_TOPK = 16
EPS = 1e-6
NEG_INF = -1e30

LANES = 128
PROJ_ROWS = 256
TOPK_TOKENS = 128
PEER_TOKENS = 512
PEER_CHUNK = 1024
SAMPLE_BATCH_BLOCK = 8
VMEM_LIMIT = 56 * 1024 * 1024

NT_DIMS = (((1,), (1,)), ((), ()))


def _resident(shape):
    nd = len(shape)
    return pl.BlockSpec(shape, lambda *_: (0,) * nd, pipeline_mode=pl.Buffered(1))


def _params(n_axes):
    return pltpu.CompilerParams(dimension_semantics=("arbitrary",) * n_axes,
                                vmem_limit_bytes=VMEM_LIMIT)


def _rms(x, g):
    return x * lax.rsqrt(jnp.mean(x * x, axis=-1, keepdims=True) + EPS) * g


def _proj_body(wsq, x_ref, g_ref, w_ref, wgu_ref, bg_ref, cos_ref, sin_ref,
               q_ref, k_ref, v_ref, r_ref, la_ref, sq_ref, sk_ref, sv_ref):
    xn = _rms(x_ref[...], g_ref[...]).astype(BF16)

    def proj(start, width):
        return jnp.dot(xn, w_ref[:, start:start + width], preferred_element_type=F32)

    nk = GLA_HEADS * GLA_DK
    q_ref[...] = proj(0, nk) * (GLA_DK ** -0.5)
    k_ref[...] = proj(nk, nk)
    v_ref[...] = proj(2 * nk, GLA_WIDTH)
    r_ref[...] = proj(2 * nk + GLA_WIDTH, GLA_WIDTH)
    o_sq = 2 * nk + 2 * GLA_WIDTH
    o_sk = o_sq + wsq
    o_sv = o_sk + LANES
    o_ga = o_sv + LANES

    ga = proj(o_ga, LANES)
    z = jnp.dot(ga, wgu_ref[...], precision=HIGHEST, preferred_element_type=F32) + bg_ref[...]
    la_ref[...] = (jnp.minimum(z, 0.0) - jnp.log1p(jnp.exp(-jnp.abs(z)))) * (1.0 / GLA_TAU)

    cos = cos_ref[...]
    sin = sin_ref[...]
    lane = lax.broadcasted_iota(jnp.int32, cos.shape, 1)
    first_half = (lane & (SWA_HEAD_DIM // 2)) == 0

    def rope(t):
        rot = jnp.where(first_half, pltpu.roll(t, LANES - SWA_HEAD_DIM // 2, 1),
                        pltpu.roll(t, SWA_HEAD_DIM // 2, 1))
        return t * cos + rot * sin

    for c in range(wsq // LANES):
        sq_ref[:, c * LANES:(c + 1) * LANES] = rope(proj(o_sq + c * LANES, LANES))
    sk_ref[...] = rope(proj(o_sk, LANES))
    sv_ref[...] = proj(o_sv, LANES)


def _proj(x, g, w, wgu, bg, cos, sin, wsq):
    t, d = x.shape
    tm = PROJ_ROWS
    n_tab = cos.shape[0] // tm
    row = lambda i: (i, 0)
    tab = lambda i: (i % n_tab, 0)
    widths = (512, 512, 1024, 1024, 512, wsq, LANES, LANES)
    return pl.pallas_call(
        functools.partial(_proj_body, wsq),
        grid=(t // tm,),
        in_specs=[pl.BlockSpec((tm, d), row), _resident(g.shape), _resident(w.shape),
                  _resident(wgu.shape), _resident(bg.shape),
                  pl.BlockSpec((tm, LANES), tab), pl.BlockSpec((tm, LANES), tab)],
        out_specs=[pl.BlockSpec((tm, n), row) for n in widths],
        out_shape=[jax.ShapeDtypeStruct((t, n), F32) for n in widths],
        compiler_params=_params(1),
        name="proj",
    )(x, g, w, wgu, bg, cos, sin)


def _gla_prompt_body(q_ref, k_ref, v_ref, la_ref, o_ref, sfin_ref, s_scr):
    n = pl.program_id(2)

    @pl.when(n == 0)
    def _():
        s_scr[...] = jnp.zeros_like(s_scr)

    c = GLA_CHUNK
    q, k, v, la = q_ref[...], k_ref[...], v_ref[...], la_ref[...]
    row = lax.broadcasted_iota(jnp.int32, (c, c), 0)
    col = lax.broadcasted_iota(jnp.int32, (c, c), 1)
    causal = row >= col
    b = jnp.dot(causal.astype(F32), la, precision=HIGHEST, preferred_element_type=F32)
    b_last = b[c - 1:c, :]
    q_t = (q * jnp.exp(b)).astype(BF16)
    k_t = (k * jnp.exp(-b)).astype(BF16)
    k_e = k * jnp.exp(b_last - b)
    vb = v.astype(BF16)
    a = lax.dot_general(q_t, k_t, NT_DIMS, preferred_element_type=F32)
    a = jnp.where(causal, a, 0.0).astype(BF16)
    s = s_scr[...]
    o_ref[...] = (jnp.dot(a, vb, preferred_element_type=F32)
                  + jnp.dot(q_t, s.astype(BF16), preferred_element_type=F32))
    decay = jnp.exp(jnp.broadcast_to(b_last, (GLA_DK, GLA_DK))).T
    decay = jnp.concatenate([decay, decay], axis=1)
    s_new = decay * s + jnp.dot(k_e.T.astype(BF16), vb, preferred_element_type=F32)
    s_scr[...] = s_new

    @pl.when(n == pl.num_programs(2) - 1)
    def _():
        sfin_ref[...] = s_new


def _gla_prompt(q, k, v, la):
    bsz, s, _ = q.shape
    c = GLA_CHUNK
    qk_spec = pl.BlockSpec((None, c, GLA_DK), lambda b, h, n: (b, n, h))
    v_spec = pl.BlockSpec((None, c, GLA_DV), lambda b, h, n: (b, n, h))
    return pl.pallas_call(
        _gla_prompt_body,
        grid=(bsz, GLA_HEADS, s // c),
        in_specs=[qk_spec, qk_spec, v_spec, qk_spec],
        out_specs=[v_spec, pl.BlockSpec((None, None, GLA_DK, GLA_DV), lambda b, h, n: (b, h, 0, 0))],
        out_shape=[jax.ShapeDtypeStruct((bsz, s, GLA_WIDTH), F32),
                   jax.ShapeDtypeStruct((bsz, GLA_HEADS, GLA_DK, GLA_DV), F32)],
        scratch_shapes=[pltpu.VMEM((GLA_DK, GLA_DV), F32)],
        compiler_params=_params(3),
        name="gla_prompt",
    )(q, k, v, la)


def _gla_sample_body(q_ref, k_ref, v_ref, la_ref, s0_ref, o_ref, sout_ref):
    bb, t, _ = q_ref.shape
    rows = bb * t
    q = q_ref[...].reshape(rows, GLA_DK)
    k = k_ref[...].reshape(rows, GLA_DK)
    la = la_ref[...].reshape(rows, GLA_DK)
    v = v_ref[...].reshape(rows, GLA_DV)
    row = lax.broadcasted_iota(jnp.int32, (rows, rows), 0)
    col = lax.broadcasted_iota(jnp.int32, (rows, rows), 1)
    same_seq = (row // t) == (col // t)
    causal = same_seq & (row >= col)
    b = jnp.dot(causal.astype(F32), la, precision=HIGHEST, preferred_element_type=F32)
    b_tot = jnp.dot(same_seq.astype(F32), la, precision=HIGHEST, preferred_element_type=F32)
    q_t = (q * jnp.exp(b)).astype(BF16)
    k_t = (k * jnp.exp(-b)).astype(BF16)
    k_e_t = (k * jnp.exp(b_tot - b)).T.astype(BF16)
    la_t = la.T
    a = lax.dot_general(q_t, k_t, NT_DIMS, preferred_element_type=F32)
    a = jnp.where(causal, a, 0.0).astype(BF16)
    o = jnp.dot(a, v.astype(BF16), preferred_element_type=F32)
    seq_of_row_v = lax.broadcasted_iota(jnp.int32, (rows, GLA_DV), 0) // t
    seq_of_row_k = lax.broadcasted_iota(jnp.int32, (rows, GLA_DK), 0) // t
    for i in range(bb):
        s0 = s0_ref[i]
        mine = seq_of_row_v == i
        o = o + jnp.where(mine, jnp.dot(q_t, s0.astype(BF16), preferred_element_type=F32), 0.0)
        v_i = jnp.where(mine, v, 0.0).astype(BF16)
        ones_i = jnp.where(seq_of_row_k == i, 1.0, 0.0)
        decay = jnp.exp(jnp.dot(la_t, ones_i, precision=HIGHEST, preferred_element_type=F32))
        decay = jnp.concatenate([decay, decay], axis=1)
        sout_ref[i] = decay * s0 + jnp.dot(k_e_t, v_i, preferred_element_type=F32)
    o_ref[...] = o.reshape(bb, t, GLA_DV)


def _gla_sample(q, k, v, la, s0):
    bsz, t, _ = q.shape
    bb = SAMPLE_BATCH_BLOCK
    qk_spec = pl.BlockSpec((bb, t, GLA_DK), lambda i, h: (i, 0, h))
    v_spec = pl.BlockSpec((bb, t, GLA_DV), lambda i, h: (i, 0, h))
    s_spec = pl.BlockSpec((bb, None, GLA_DK, GLA_DV), lambda i, h: (i, h, 0, 0))
    return pl.pallas_call(
        _gla_sample_body,
        grid=(bsz // bb, GLA_HEADS),
        in_specs=[qk_spec, qk_spec, v_spec, qk_spec, s_spec],
        out_specs=[v_spec, s_spec],
        out_shape=[jax.ShapeDtypeStruct((bsz, t, GLA_WIDTH), F32),
                   jax.ShapeDtypeStruct(s0.shape, F32)],
        compiler_params=_params(2),
        name="gla_sample",
    )(q, k, v, la, s0)


def _sink_softmax(s, mask, sink):
    s = jnp.where(mask, s, NEG_INF)
    m = jnp.maximum(jnp.max(s, axis=-1, keepdims=True), sink)
    p = jnp.exp(s - m)
    return p / (jnp.sum(p, axis=-1, keepdims=True) + jnp.exp(sink - m))


def _split_kv_lanes(x):
    lane = lax.broadcasted_iota(jnp.int32, x.shape, 1)
    low = lane < SWA_HEAD_DIM
    lo0 = jnp.where(low, x, 0.0)
    hi1 = jnp.where(low, 0.0, x)
    hi0 = pltpu.roll(lo0, SWA_HEAD_DIM, 1)
    lo1 = pltpu.roll(hi1, SWA_HEAD_DIM, 1)
    return (lo0.astype(BF16), lo1.astype(BF16)), (hi0.astype(BF16), hi1.astype(BF16))


def _swa_prompt_body(sink_ref, q_ref, kc_ref, kp_ref, vc_ref, vp_ref, o_ref):
    n = pl.program_id(1)
    blk = WINDOW
    k_parts = _split_kv_lanes(jnp.concatenate([kp_ref[...], kc_ref[...]], axis=0))
    v_parts = _split_kv_lanes(jnp.concatenate([vp_ref[...], vc_ref[...]], axis=0))
    qi = lax.broadcasted_iota(jnp.int32, (blk, 2 * blk), 0)
    kj = lax.broadcasted_iota(jnp.int32, (blk, 2 * blk), 1)
    diff = qi + blk - kj
    mask = (diff >= 0) & (diff < WINDOW) & ((n > 0) | (kj >= blk))
    scale = SWA_HEAD_DIM ** -0.5
    for pair in range(SWA_HEADS // 2):
        kv = (2 * pair) // SWA_GROUP
        qp = q_ref[:, pair * LANES:(pair + 1) * LANES].astype(BF16)
        acc = None
        for half in range(2):
            s = lax.dot_general(qp, k_parts[half][kv], NT_DIMS, preferred_element_type=F32) * scale
            p = _sink_softmax(s, mask, sink_ref[2 * pair + half])
            o = jnp.dot(p.astype(BF16), v_parts[half][kv], preferred_element_type=F32)
            acc = o if acc is None else acc + o
        o_ref[:, pair * LANES:(pair + 1) * LANES] = acc


def _swa_prompt(sq, sk, sv, sinks):
    bsz, s, _ = sq.shape
    blk = WINDOW
    cur = lambda b, n: (b, n, 0)
    prev = lambda b, n: (b, jnp.maximum(n - 1, 0), 0)
    kv_cur = pl.BlockSpec((None, blk, LANES), cur)
    kv_prev = pl.BlockSpec((None, blk, LANES), prev)
    q_spec = pl.BlockSpec((None, blk, SWA_WIDTH), cur)
    return pl.pallas_call(
        _swa_prompt_body,
        grid=(bsz, s // blk),
        in_specs=[pl.BlockSpec(memory_space=pltpu.SMEM), q_spec, kv_cur, kv_prev, kv_cur, kv_prev],
        out_specs=q_spec,
        out_shape=jax.ShapeDtypeStruct((bsz, s, SWA_WIDTH), F32),
        compiler_params=_params(2),
        name="swa_prompt",
    )(sinks, sq, sk, sk, sv, sv)


def _swa_sample_body(t_new, sink_ref, q_ref, ck_ref, cv_ref, kn_ref, vn_ref, o_ref, nk_ref, nv_ref):
    bb, rows, _ = q_ref.shape
    w = ck_ref.shape[1]
    pad = jnp.zeros((16 - t_new, LANES), F32)
    n_keys = w + 16
    tq = lax.broadcasted_iota(jnp.int32, (rows, n_keys), 0) // SWA_HEADS
    c = lax.broadcasted_iota(jnp.int32, (rows, n_keys), 1)
    mask = ((c < w) & (c > tq)) | ((c >= w) & (c - w <= tq) & (c < w + t_new))
    sink = sink_ref[...]
    scale = SWA_HEAD_DIM ** -0.5
    for i in range(bb):
        ck, cv, kn, vn = ck_ref[i], cv_ref[i], kn_ref[i], vn_ref[i]
        kcat = jnp.concatenate([ck, kn, pad], axis=0).astype(BF16)
        vcat = jnp.concatenate([cv, vn, pad], axis=0).astype(BF16)
        s = lax.dot_general(q_ref[i].astype(BF16), kcat, NT_DIMS, preferred_element_type=F32) * scale
        p = _sink_softmax(s, mask, sink)
        o_ref[i] = jnp.dot(p.astype(BF16), vcat, preferred_element_type=F32)
        nk_ref[i, 0:w - t_new, :] = ck[t_new:, :]
        nk_ref[i, w - t_new:w, :] = kn
        nv_ref[i, 0:w - t_new, :] = cv[t_new:, :]
        nv_ref[i, w - t_new:w, :] = vn


def _swa_sample(q_pad, ck, cv, kn, vn, sink_col):
    bsz, rows, _ = q_pad.shape
    t_new = kn.shape[1]
    w = ck.shape[1]
    bb = SAMPLE_BATCH_BLOCK
    blk = lambda r: pl.BlockSpec((bb, r, LANES), lambda i: (i, 0, 0))
    return pl.pallas_call(
        functools.partial(_swa_sample_body, t_new),
        grid=(bsz // bb,),
        in_specs=[_resident(sink_col.shape), blk(rows), blk(w), blk(w), blk(t_new), blk(t_new)],
        out_specs=[blk(rows), blk(w), blk(w)],
        out_shape=[jax.ShapeDtypeStruct((bsz, rows, LANES), F32),
                   jax.ShapeDtypeStruct(ck.shape, F32), jax.ShapeDtypeStruct(cv.shape, F32)],
        compiler_params=_params(1),
        name="swa_sample",
    )(sink_col, q_pad, ck, cv, kn, vn)


def _merge_body(og_ref, r_ref, os_ref, x_ref, gn_ref, wog_ref, wos_ref, gf_ref, wq_ref, keys_ref,
                h_ref, hn_ref, st_ref):
    og = og_ref[...]
    gn = gn_ref[...]
    heads = [_rms(og[:, i * GLA_DV:(i + 1) * GLA_DV], gn) for i in range(GLA_HEADS)]
    r = r_ref[...]
    gated = jnp.concatenate(heads, axis=1) * (r / (1.0 + jnp.exp(-r)))
    h = (x_ref[...]
         + jnp.dot(gated.astype(BF16), wog_ref[...], preferred_element_type=F32)
         + jnp.dot(os_ref[...].astype(BF16), wos_ref[...], preferred_element_type=F32))
    h_ref[...] = h
    hn = _rms(h, gf_ref[...]).astype(BF16)
    hn_ref[...] = hn
    pq = jnp.dot(hn, wq_ref[...], preferred_element_type=F32).astype(BF16)
    for i in range(PEER_HEADS):
        st_ref[i] = lax.dot_general(keys_ref[i], pq[:, i * PEER_QDIM:(i + 1) * PEER_QDIM], NT_DIMS,
                                    preferred_element_type=F32)


def _merge(o_gla, r, o_swa, x, gn, wog, wos, gf, wq, keys):
    t, d = x.shape
    tm = PROJ_ROWS
    row = lambda i: (i, 0)
    return pl.pallas_call(
        _merge_body,
        grid=(t // tm,),
        in_specs=[pl.BlockSpec((tm, GLA_WIDTH), row), pl.BlockSpec((tm, GLA_WIDTH), row),
                  pl.BlockSpec((tm, o_swa.shape[1]), row), pl.BlockSpec((tm, d), row),
                  _resident(gn.shape), _resident(wog.shape), _resident(wos.shape),
                  _resident(gf.shape), _resident(wq.shape), _resident(keys.shape)],
        out_specs=[pl.BlockSpec((tm, d), row), pl.BlockSpec((tm, d), row),
                   pl.BlockSpec((PEER_HEADS, 2 * PEER_N_KEYS, tm), lambda i: (0, 0, i))],
        out_shape=[jax.ShapeDtypeStruct((t, d), F32), jax.ShapeDtypeStruct((t, d), BF16),
                   jax.ShapeDtypeStruct((PEER_HEADS, 2 * PEER_N_KEYS, t), F32)],
        compiler_params=_params(1),
        name="merge",
    )(o_gla, r, o_swa, x, gn, wog, wos, gf, wq, keys)


N_SORTED = PEER_TOPK + 1


def _topk_body(st_ref, aux_ref, top0_scr, top1_scr):
    scr = (top0_scr, top1_scr)
    for h in range(PEER_HEADS):
        for half in range(2):
            arr = st_ref[h, half * PEER_N_KEYS:(half + 1) * PEER_N_KEYS, :]
            for a in range(N_SORTED):
                m = jnp.max(arr, axis=0, keepdims=True)
                scr[half][a, h:h + 1, :] = m
                if a + 1 < N_SORTED:
                    arr = jnp.where(arr == m, NEG_INF, arr)
    top0 = [top0_scr[a] for a in range(N_SORTED)]
    top1 = [top1_scr[a] for a in range(N_SORTED)]
    cands = [top0[a - 1] + top1[b - 1]
             for a in range(1, N_SORTED + 1) for b in range(1, N_SORTED // a + 1)]
    best = None
    z = None
    for rank in range(N_SORTED):
        m = functools.reduce(jnp.maximum, cands)
        if rank == 0:
            best = m
            z = jnp.ones_like(m)
        elif rank < PEER_TOPK:
            z = z + jnp.exp(m - best)
        if rank == PEER_TOPK - 1:
            kth = m
        if rank + 1 < N_SORTED:
            cands = [jnp.where(cnd == m, NEG_INF, cnd) for cnd in cands]
    aux_ref[0] = 0.5 * (kth + m)
    aux_ref[1] = top0[0]
    aux_ref[2] = top1[0]
    aux_ref[3] = 1.0 / z


def _topk(st):
    t = st.shape[2]
    tt = TOPK_TOKENS
    return pl.pallas_call(
        _topk_body,
        grid=(t // tt,),
        in_specs=[pl.BlockSpec((PEER_HEADS, 2 * PEER_N_KEYS, tt), lambda i: (0, 0, i))],
        out_specs=pl.BlockSpec((4, PEER_HEADS, tt), lambda i: (0, 0, i)),
        out_shape=jax.ShapeDtypeStruct((4, PEER_HEADS, t), F32),
        scratch_shapes=[pltpu.VMEM((N_SORTED, PEER_HEADS, tt), F32),
                        pltpu.VMEM((N_SORTED, PEER_HEADS, tt), F32)],
        compiler_params=_params(1),
        name="peer_topk",
    )(st)


def _peer_body(st_ref, aux_ref, hn_ref, u_ref, vt_ref, out_ref,
               thr_scr, w0_scr, e1_scr, a_scr, c_scr, acc_scr):
    ec = pl.program_id(1)
    n_rows = PEER_CHUNK // PEER_N_KEYS
    tb = hn_ref.shape[0]

    @pl.when(ec == 0)
    def _():
        grid_rows = (PEER_N_KEYS // n_rows, n_rows, tb)
        for h in range(PEER_HEADS):
            s0 = st_ref[h, 0:PEER_N_KEYS, :]
            s1 = st_ref[h, PEER_N_KEYS:2 * PEER_N_KEYS, :]
            thr_scr[h] = (aux_ref[0, h:h + 1, :] - s0).reshape(grid_rows)
            w0_scr[h] = (jnp.exp(s0 - aux_ref[1, h:h + 1, :]) * aux_ref[3, h:h + 1, :]).reshape(grid_rows)
            e1_scr[h] = jnp.exp(s1 - aux_ref[2, h:h + 1, :])
        acc_scr[...] = jnp.zeros_like(acc_scr)

    a_scr[...] = lax.dot_general(u_ref[...], hn_ref[...], NT_DIMS, preferred_element_type=F32)

    sub = 16
    inv_sqrt2 = 1.0 / math.sqrt(2.0)

    def lane_tile_body(lt, carry):
        ls = pl.ds(pl.multiple_of(lt * LANES, LANES), LANES)
        for ii in range(n_rows):
            thr_b = [thr_scr[h, ec, ii:ii + 1, ls] for h in range(PEER_HEADS)]
            w0_b = [w0_scr[h, ec, ii:ii + 1, ls] for h in range(PEER_HEADS)]
            for jb in range(PEER_N_KEYS // sub):
                rows = slice(ii * PEER_N_KEYS + jb * sub, ii * PEER_N_KEYS + (jb + 1) * sub)
                a = a_scr[rows, ls]
                act = 0.5 * a * (1.0 + lax.erf(a * inv_sqrt2))
                g = jnp.zeros_like(a)
                for h in range(PEER_HEADS):
                    s1 = st_ref[h, PEER_N_KEYS + jb * sub:PEER_N_KEYS + (jb + 1) * sub, ls]
                    e1 = e1_scr[h, jb * sub:(jb + 1) * sub, ls]
                    g = g + jnp.where(s1 >= thr_b[h], e1, 0.0) * w0_b[h]
                c_scr[rows, ls] = (g * act).astype(BF16)
        return carry

    lax.fori_loop(0, tb // LANES, lane_tile_body, 0)
    acc_scr[...] += jnp.dot(vt_ref[...], c_scr[...], preferred_element_type=F32)

    @pl.when(ec == pl.num_programs(1) - 1)
    def _():
        out_ref[...] = acc_scr[...].T


def _peer(st, aux, hn, u, vt):
    t, d = hn.shape
    tb = PEER_TOKENS
    ec = PEER_CHUNK
    n_exp = u.shape[0]
    return pl.pallas_call(
        _peer_body,
        grid=(t // tb, n_exp // ec),
        in_specs=[pl.BlockSpec((PEER_HEADS, 2 * PEER_N_KEYS, tb), lambda i, j: (0, 0, i),
                               pipeline_mode=pl.Buffered(1)),
                  pl.BlockSpec((4, PEER_HEADS, tb), lambda i, j: (0, 0, i)),
                  pl.BlockSpec((tb, d), lambda i, j: (i, 0), pipeline_mode=pl.Buffered(1)),
                  pl.BlockSpec((ec, d), lambda i, j: (j, 0)),
                  pl.BlockSpec((d, ec), lambda i, j: (0, j))],
        out_specs=pl.BlockSpec((tb, d), lambda i, j: (i, 0)),
        out_shape=jax.ShapeDtypeStruct((t, d), F32),
        scratch_shapes=[pltpu.VMEM((PEER_HEADS, n_exp // ec, ec // PEER_N_KEYS, tb), F32),
                        pltpu.VMEM((PEER_HEADS, n_exp // ec, ec // PEER_N_KEYS, tb), F32),
                        pltpu.VMEM((PEER_HEADS, PEER_N_KEYS, tb), F32),
                        pltpu.VMEM((ec, tb), F32),
                        pltpu.VMEM((ec, tb), BF16),
                        pltpu.VMEM((d, tb), F32)],
        compiler_params=_params(2),
        name="peer_dense",
    )(st, aux, hn, u, vt)


def _final_body(h_ref, p_ref, g_ref, y_ref):
    y_ref[...] = _rms(h_ref[...] + p_ref[...], g_ref[...])


def _final(h, p, g):
    t, d = h.shape
    tm = PROJ_ROWS
    row = lambda i: (i, 0)
    return pl.pallas_call(
        _final_body,
        grid=(t // tm,),
        in_specs=[pl.BlockSpec((tm, d), row), pl.BlockSpec((tm, d), row), _resident(g.shape)],
        out_specs=pl.BlockSpec((tm, d), row),
        out_shape=jax.ShapeDtypeStruct((t, d), F32),
        compiler_params=_params(1),
        name="final",
    )(h, p, g)


def _rope_tables(pos):
    half = SWA_HEAD_DIM // 2
    inv = jnp.exp(-math.log(ROPE_THETA) * jnp.arange(half, dtype=F32) * (2.0 / SWA_HEAD_DIM))
    ang = pos.astype(F32)[:, None] * inv[None, :]
    cos, sin = jnp.cos(ang), jnp.sin(ang)
    reps = LANES // SWA_HEAD_DIM
    return (jnp.tile(jnp.concatenate([cos, cos], axis=1), (1, reps)),
            jnp.tile(jnp.concatenate([-sin, sin], axis=1), (1, reps)))


def _split_w_in(w_in):
    out, start = [], 0
    for width in (512, 512, 1024, 1024, GLA_GATE_RANK, SWA_WIDTH, LANES, LANES):
        out.append(w_in[:, start:start + width])
        start += width
    return out


def _pad_heads_to_kv_half(w):
    lead = w.shape[:-1]
    w = w.reshape(lead + (SWA_KV_HEADS, SWA_GROUP, 1, SWA_HEAD_DIM))
    sel = jnp.eye(SWA_KV_HEADS, dtype=w.dtype).reshape(SWA_KV_HEADS, 1, SWA_KV_HEADS, 1)
    return (w * sel).reshape(lead + (SWA_HEADS * LANES,))


def kernel(x_prompt, x_sample, state_gla, cache_win_k, cache_win_v, g_attn, w_in, w_gate_up, b_gate,
           g_gla_out, attn_sinks, w_out, g_ffn, w_peer_q, peer_sub_keys, peer_u, peer_v, g_final):
    bsz, seq, d = x_prompt.shape
    dbsz, dseq, _ = x_sample.shape
    win = cache_win_k.shape[2]

    gq, gk, gv, gr, ga, wsq, wsk, wsv = _split_w_in(w_in[0])
    ga = jnp.pad(ga, ((0, 0), (0, LANES - GLA_GATE_RANK)))
    w_prompt = jnp.concatenate([gq, gk, gv, gr, wsq, wsk, wsv, ga], axis=1).astype(BF16)
    w_sample = jnp.concatenate([gq, gk, gv, gr, _pad_heads_to_kv_half(wsq), wsk, wsv, ga],
                               axis=1).astype(BF16)
    wgu = jnp.pad(w_gate_up[0], ((0, LANES - GLA_GATE_RANK), (0, 0)))
    bg = b_gate[0][None, :]
    g_a = g_attn[0][None, :]
    g_f = g_ffn[0][None, :]
    g_n = g_gla_out[0][None, :]
    wo_gla = w_out[0, :GLA_WIDTH].astype(BF16)
    wo_swa = w_out[0, GLA_WIDTH:]
    wo_swa_pad = _pad_heads_to_kv_half(wo_swa.T).T.astype(BF16)
    wo_swa = wo_swa.astype(BF16)
    wq = w_peer_q[0].astype(BF16)
    sk0 = jnp.pad(peer_sub_keys[0, :, 0], ((0, 0), (0, 0), (0, PEER_HALF)))
    sk1 = jnp.pad(peer_sub_keys[0, :, 1], ((0, 0), (0, 0), (PEER_HALF, 0)))
    keys = jnp.concatenate([sk0, sk1], axis=1).astype(BF16)
    u = peer_u[0].astype(BF16)
    vt = peer_v[0].T.astype(BF16)
    sinks = attn_sinks[0]
    sink_col = jnp.tile(sinks, dseq)[:, None]

    cos_p, sin_p = _rope_tables(jnp.arange(seq))
    cos_s, sin_s = _rope_tables(PAST_LEN + jnp.arange(dseq))
    cos_s = jnp.tile(cos_s, (PROJ_ROWS // dseq, 1))
    sin_s = jnp.tile(sin_s, (PROJ_ROWS // dseq, 1))

    def ffn(h, hn, st):
        aux = _topk(st)
        return _final(h, _peer(st, aux, hn, u, vt), g_final[None, :])

    tp = bsz * seq
    q, k, v, r, la, sq, sk, sv = _proj(x_prompt.reshape(tp, d), g_a, w_prompt, wgu, bg, cos_p, sin_p,
                                       SWA_WIDTH)
    b3 = lambda a: a.reshape(bsz, seq, a.shape[-1])
    o_gla, s_prompt = _gla_prompt(b3(q), b3(k), b3(v), b3(la))
    o_swa = _swa_prompt(b3(sq), b3(sk), b3(sv), sinks)
    h, hn, st = _merge(o_gla.reshape(tp, GLA_WIDTH), r, o_swa.reshape(tp, SWA_WIDTH),
                       x_prompt.reshape(tp, d), g_n, wo_gla, wo_swa, g_f, wq, keys)
    y_prompt = ffn(h, hn, st).reshape(bsz, seq, d)
    kv_shape = (1, bsz, WINDOW, SWA_KV_HEADS, SWA_HEAD_DIM)
    k_prompt = b3(sk)[:, seq - WINDOW:].reshape(kv_shape)
    v_prompt = b3(sv)[:, seq - WINDOW:].reshape(kv_shape)

    ts = dbsz * dseq
    q, k, v, r, la, sq, sk, sv = _proj(x_sample.reshape(ts, d), g_a, w_sample, wgu, bg, cos_s, sin_s,
                                       SWA_HEADS * LANES)
    d3 = lambda a: a.reshape(dbsz, dseq, a.shape[-1])
    o_gla, s_sample = _gla_sample(d3(q), d3(k), d3(v), d3(la), state_gla[0])
    o_swa, k_sample, v_sample = _swa_sample(
        sq.reshape(dbsz, dseq * SWA_HEADS, LANES),
        cache_win_k[0].reshape(dbsz, win, LANES), cache_win_v[0].reshape(dbsz, win, LANES),
        d3(sk), d3(sv), sink_col)
    h, hn, st = _merge(o_gla.reshape(ts, GLA_WIDTH), r, o_swa.reshape(ts, SWA_HEADS * LANES),
                       x_sample.reshape(ts, d), g_n, wo_gla, wo_swa_pad, g_f, wq, keys)
    y_sample = ffn(h, hn, st).reshape(dbsz, dseq, d)
    ckv_shape = (1, dbsz, win, SWA_KV_HEADS, SWA_HEAD_DIM)

    return (y_prompt, y_sample, s_prompt[None], k_prompt, v_prompt, s_sample[None],
            k_sample.reshape(ckv_shape), v_sample.reshape(ckv_shape))
```

```python
import functools
import math

import jax
import jax.numpy as jnp
from jax import lax
from jax.experimental import pallas as pl
from jax.experimental.pallas import tpu as pltpu

F32 = jnp.float32
BF16 = jnp.bfloat16
HIGHEST = lax.Precision.HIGHEST

D_MODEL = 2048
GLA_HEADS = 4
GLA_DK = 128
GLA_DV = 256
GLA_WIDTH = GLA_HEADS * GLA_DV
GLA_GATE_RANK = 16
GLA_TAU = 16.0
GLA_CHUNK = 64
SWA_HEAD_DIM = 64
SWA_HEADS = 16
SWA_KV_HEADS = 2
SWA_GROUP = SWA_HEADS // SWA_KV_HEADS
SWA_WIDTH = SWA_HEADS * SWA_HEAD_DIM
WINDOW = 128
ROPE_THETA = 10000.0
PAST_LEN = 8192
PEER_N_KEYS = 128
PEER_EXPERTS = PEER_N_KEYS * PEER_N_KEYS
PEER_HEADS = 8
PEER_QDIM = 128
PEER_HALF = PEER_QDIM // 2
PEER_TOPK = 16
EPS = 1e-6
NEG_INF = -1e30

LANES = 128
PROJ_ROWS = 256
TOPK_TOKENS = 128
PEER_TOKENS = 512
PEER_CHUNK = 1024
SAMPLE_BATCH_BLOCK = 8
VMEM_LIMIT = 56 * 1024 * 1024

NT_DIMS = (((1,), (1,)), ((), ()))


def _resident(shape):
    nd = len(shape)
    return pl.BlockSpec(shape, lambda *_: (0,) * nd, pipeline_mode=pl.Buffered(1))


def _params(n_axes):
    return pltpu.CompilerParams(dimension_semantics=("arbitrary",) * n_axes,
                                vmem_limit_bytes=VMEM_LIMIT)


def _rms(x, g):
    return x * lax.rsqrt(jnp.mean(x * x, axis=-1, keepdims=True) + EPS) * g


def _proj_body(wsq, x_ref, g_ref, w_ref, wgu_ref, bg_ref, cos_ref, sin_ref,
               q_ref, k_ref, v_ref, r_ref, la_ref, sq_ref, sk_ref, sv_ref):
    xn = _rms(x_ref[...], g_ref[...]).astype(BF16)

    def proj(start, width):
        return jnp.dot(xn, w_ref[:, start:start + width], preferred_element_type=F32)

    nk = GLA_HEADS * GLA_DK
    q_ref[...] = proj(0, nk) * (GLA_DK ** -0.5)
    k_ref[...] = proj(nk, nk)
    v_ref[...] = proj(2 * nk, GLA_WIDTH)
    r_ref[...] = proj(2 * nk + GLA_WIDTH, GLA_WIDTH)
    o_sq = 2 * nk + 2 * GLA_WIDTH
    o_sk = o_sq + wsq
    o_sv = o_sk + LANES
    o_ga = o_sv + LANES

    ga = proj(o_ga, LANES)
    z = jnp.dot(ga, wgu_ref[...], precision=HIGHEST, preferred_element_type=F32) + bg_ref[...]
    la_ref[...] = (jnp.minimum(z, 0.0) - jnp.log1p(jnp.exp(-jnp.abs(z)))) * (1.0 / GLA_TAU)

    cos = cos_ref[...]
    sin = sin_ref[...]
    lane = lax.broadcasted_iota(jnp.int32, cos.shape, 1)
    first_half = (lane & (SWA_HEAD_DIM // 2)) == 0

    def rope(t):
        rot = jnp.where(first_half, pltpu.roll(t, LANES - SWA_HEAD_DIM // 2, 1),
                        pltpu.roll(t, SWA_HEAD_DIM // 2, 1))
        return t * cos + rot * sin

    for c in range(wsq // LANES):
        sq_ref[:, c * LANES:(c + 1) * LANES] = rope(proj(o_sq + c * LANES, LANES))
    sk_ref[...] = rope(proj(o_sk, LANES))
    sv_ref[...] = proj(o_sv, LANES)


def _proj(x, g, w, wgu, bg, cos, sin, wsq):
    t, d = x.shape
    tm = PROJ_ROWS
    n_tab = cos.shape[0] // tm
    row = lambda i: (i, 0)
    tab = lambda i: (i % n_tab, 0)
    widths = (512, 512, 1024, 1024, 512, wsq, LANES, LANES)
    return pl.pallas_call(
        functools.partial(_proj_body, wsq),
        grid=(t // tm,),
        in_specs=[pl.BlockSpec((tm, d), row), _resident(g.shape), _resident(w.shape),
                  _resident(wgu.shape), _resident(bg.shape),
                  pl.BlockSpec((tm, LANES), tab), pl.BlockSpec((tm, LANES), tab)],
        out_specs=[pl.BlockSpec((tm, n), row) for n in widths],
        out_shape=[jax.ShapeDtypeStruct((t, n), F32) for n in widths],
        compiler_params=_params(1),
        name="proj",
    )(x, g, w, wgu, bg, cos, sin)


def _gla_prompt_body(q_ref, k_ref, v_ref, la_ref, o_ref, sfin_ref, s_scr):
    n = pl.program_id(2)

    @pl.when(n == 0)
    def _():
        s_scr[...] = jnp.zeros_like(s_scr)

    c = GLA_CHUNK
    q, k, v, la = q_ref[...], k_ref[...], v_ref[...], la_ref[...]
    row = lax.broadcasted_iota(jnp.int32, (c, c), 0)
    col = lax.broadcasted_iota(jnp.int32, (c, c), 1)
    causal = row >= col
    b = jnp.dot(causal.astype(F32), la, precision=HIGHEST, preferred_element_type=F32)
    b_last = b[c - 1:c, :]
    q_t = (q * jnp.exp(b)).astype(BF16)
    k_t = (k * jnp.exp(-b)).astype(BF16)
    k_e = k * jnp.exp(b_last - b)
    vb = v.astype(BF16)
    a = lax.dot_general(q_t, k_t, NT_DIMS, preferred_element_type=F32)
    a = jnp.where(causal, a, 0.0).astype(BF16)
    s = s_scr[...]
    o_ref[...] = (jnp.dot(a, vb, preferred_element_type=F32)
                  + jnp.dot(q_t, s.astype(BF16), preferred_element_type=F32))
    decay = jnp.exp(jnp.broadcast_to(b_last, (GLA_DK, GLA_DK))).T
    decay = jnp.concatenate([decay, decay], axis=1)
    s_new = decay * s + jnp.dot(k_e.T.astype(BF16), vb, preferred_element_type=F32)
    s_scr[...] = s_new

    @pl.when(n == pl.num_programs(2) - 1)
    def _():
        sfin_ref[...] = s_new


def _gla_prompt(q, k, v, la):
    bsz, s, _ = q.shape
    c = GLA_CHUNK
    qk_spec = pl.BlockSpec((None, c, GLA_DK), lambda b, h, n: (b, n, h))
    v_spec = pl.BlockSpec((None, c, GLA_DV), lambda b, h, n: (b, n, h))
    return pl.pallas_call(
        _gla_prompt_body,
        grid=(bsz, GLA_HEADS, s // c),
        in_specs=[qk_spec, qk_spec, v_spec, qk_spec],
        out_specs=[v_spec, pl.BlockSpec((None, None, GLA_DK, GLA_DV), lambda b, h, n: (b, h, 0, 0))],
        out_shape=[jax.ShapeDtypeStruct((bsz, s, GLA_WIDTH), F32),
                   jax.ShapeDtypeStruct((bsz, GLA_HEADS, GLA_DK, GLA_DV), F32)],
        scratch_shapes=[pltpu.VMEM((GLA_DK, GLA_DV), F32)],
        compiler_params=_params(3),
        name="gla_prompt",
    )(q, k, v, la)


def _gla_sample_body(q_ref, k_ref, v_ref, la_ref, s0_ref, o_ref, sout_ref):
    bb, t, _ = q_ref.shape
    rows = bb * t
    q = q_ref[...].reshape(rows, GLA_DK)
    k = k_ref[...].reshape(rows, GLA_DK)
    la = la_ref[...].reshape(rows, GLA_DK)
    v = v_ref[...].reshape(rows, GLA_DV)
    row = lax.broadcasted_iota(jnp.int32, (rows, rows), 0)
    col = lax.broadcasted_iota(jnp.int32, (rows, rows), 1)
    same_seq = (row // t) == (col // t)
    causal = same_seq & (row >= col)
    b = jnp.dot(causal.astype(F32), la, precision=HIGHEST, preferred_element_type=F32)
    b_tot = jnp.dot(same_seq.astype(F32), la, precision=HIGHEST, preferred_element_type=F32)
    q_t = (q * jnp.exp(b)).astype(BF16)
    k_t = (k * jnp.exp(-b)).astype(BF16)
    k_e_t = (k * jnp.exp(b_tot - b)).T.astype(BF16)
    la_t = la.T
    a = lax.dot_general(q_t, k_t, NT_DIMS, preferred_element_type=F32)
    a = jnp.where(causal, a, 0.0).astype(BF16)
    o = jnp.dot(a, v.astype(BF16), preferred_element_type=F32)
    seq_of_row_v = lax.broadcasted_iota(jnp.int32, (rows, GLA_DV), 0) // t
    seq_of_row_k = lax.broadcasted_iota(jnp.int32, (rows, GLA_DK), 0) // t
    for i in range(bb):
        s0 = s0_ref[i]
        mine = seq_of_row_v == i
        o = o + jnp.where(mine, jnp.dot(q_t, s0.astype(BF16), preferred_element_type=F32), 0.0)
        v_i = jnp.where(mine, v, 0.0).astype(BF16)
        ones_i = jnp.where(seq_of_row_k == i, 1.0, 0.0)
        decay = jnp.exp(jnp.dot(la_t, ones_i, precision=HIGHEST, preferred_element_type=F32))
        decay = jnp.concatenate([decay, decay], axis=1)
        sout_ref[i] = decay * s0 + jnp.dot(k_e_t, v_i, preferred_element_type=F32)
    o_ref[...] = o.reshape(bb, t, GLA_DV)


def _gla_sample(q, k, v, la, s0):
    bsz, t, _ = q.shape
    bb = SAMPLE_BATCH_BLOCK
    qk_spec = pl.BlockSpec((bb, t, GLA_DK), lambda i, h: (i, 0, h))
    v_spec = pl.BlockSpec((bb, t, GLA_DV), lambda i, h: (i, 0, h))
    s_spec = pl.BlockSpec((bb, None, GLA_DK, GLA_DV), lambda i, h: (i, h, 0, 0))
    return pl.pallas_call(
        _gla_sample_body,
        grid=(bsz // bb, GLA_HEADS),
        in_specs=[qk_spec, qk_spec, v_spec, qk_spec, s_spec],
        out_specs=[v_spec, s_spec],
        out_shape=[jax.ShapeDtypeStruct((bsz, t, GLA_WIDTH), F32),
                   jax.ShapeDtypeStruct(s0.shape, F32)],
        compiler_params=_params(2),
        name="gla_sample",
    )(q, k, v, la, s0)


def _sink_softmax(s, mask, sink):
    s = jnp.where(mask, s, NEG_INF)
    m = jnp.maximum(jnp.max(s, axis=-1, keepdims=True), sink)
    p = jnp.exp(s - m)
    return p / (jnp.sum(p, axis=-1, keepdims=True) + jnp.exp(sink - m))


def _split_kv_lanes(x):
    lane = lax.broadcasted_iota(jnp.int32, x.shape, 1)
    low = lane < SWA_HEAD_DIM
    lo0 = jnp.where(low, x, 0.0)
    hi1 = jnp.where(low, 0.0, x)
    hi0 = pltpu.roll(lo0, SWA_HEAD_DIM, 1)
    lo1 = pltpu.roll(hi1, SWA_HEAD_DIM, 1)
    return (lo0.astype(BF16), lo1.astype(BF16)), (hi0.astype(BF16), hi1.astype(BF16))


def _swa_prompt_body(sink_ref, q_ref, kc_ref, kp_ref, vc_ref, vp_ref, o_ref):
    n = pl.program_id(1)
    blk = WINDOW
    k_parts = _split_kv_lanes(jnp.concatenate([kp_ref[...], kc_ref[...]], axis=0))
    v_parts = _split_kv_lanes(jnp.concatenate([vp_ref[...], vc_ref[...]], axis=0))
    qi = lax.broadcasted_iota(jnp.int32, (blk, 2 * blk), 0)
    kj = lax.broadcasted_iota(jnp.int32, (blk, 2 * blk), 1)
    diff = qi + blk - kj
    mask = (diff >= 0) & (diff < WINDOW) & ((n > 0) | (kj >= blk))
    scale = SWA_HEAD_DIM ** -0.5
    for pair in range(SWA_HEADS // 2):
        kv = (2 * pair) // SWA_GROUP
        qp = q_ref[:, pair * LANES:(pair + 1) * LANES].astype(BF16)
        acc = None
        for half in range(2):
            s = lax.dot_general(qp, k_parts[half][kv], NT_DIMS, preferred_element_type=F32) * scale
            p = _sink_softmax(s, mask, sink_ref[2 * pair + half])
            o = jnp.dot(p.astype(BF16), v_parts[half][kv], preferred_element_type=F32)
            acc = o if acc is None else acc + o
        o_ref[:, pair * LANES:(pair + 1) * LANES] = acc


def _swa_prompt(sq, sk, sv, sinks):
    bsz, s, _ = sq.shape
    blk = WINDOW
    cur = lambda b, n: (b, n, 0)
    prev = lambda b, n: (b, jnp.maximum(n - 1, 0), 0)
    kv_cur = pl.BlockSpec((None, blk, LANES), cur)
    kv_prev = pl.BlockSpec((None, blk, LANES), prev)
    q_spec = pl.BlockSpec((None, blk, SWA_WIDTH), cur)
    return pl.pallas_call(
        _swa_prompt_body,
        grid=(bsz, s // blk),
        in_specs=[pl.BlockSpec(memory_space=pltpu.SMEM), q_spec, kv_cur, kv_prev, kv_cur, kv_prev],
        out_specs=q_spec,
        out_shape=jax.ShapeDtypeStruct((bsz, s, SWA_WIDTH), F32),
        compiler_params=_params(2),
        name="swa_prompt",
    )(sinks, sq, sk, sk, sv, sv)


def _swa_sample_body(t_new, sink_ref, q_ref, ck_ref, cv_ref, kn_ref, vn_ref, o_ref, nk_ref, nv_ref):
    bb, rows, _ = q_ref.shape
    w = ck_ref.shape[1]
    pad = jnp.zeros((16 - t_new, LANES), F32)
    n_keys = w + 16
    tq = lax.broadcasted_iota(jnp.int32, (rows, n_keys), 0) // SWA_HEADS
    c = lax.broadcasted_iota(jnp.int32, (rows, n_keys), 1)
    mask = ((c < w) & (c > tq)) | ((c >= w) & (c - w <= tq) & (c < w + t_new))
    sink = sink_ref[...]
    scale = SWA_HEAD_DIM ** -0.5
    for i in range(bb):
        ck, cv, kn, vn = ck_ref[i], cv_ref[i], kn_ref[i], vn_ref[i]
        kcat = jnp.concatenate([ck, kn, pad], axis=0).astype(BF16)
        vcat = jnp.concatenate([cv, vn, pad], axis=0).astype(BF16)
        s = lax.dot_general(q_ref[i].astype(BF16), kcat, NT_DIMS, preferred_element_type=F32) * scale
        p = _sink_softmax(s, mask, sink)
        o_ref[i] = jnp.dot(p.astype(BF16), vcat, preferred_element_type=F32)
        nk_ref[i, 0:w - t_new, :] = ck[t_new:, :]
        nk_ref[i, w - t_new:w, :] = kn
        nv_ref[i, 0:w - t_new, :] = cv[t_new:, :]
        nv_ref[i, w - t_new:w, :] = vn


def _swa_sample(q_pad, ck, cv, kn, vn, sink_col):
    bsz, rows, _ = q_pad.shape
    t_new = kn.shape[1]
    w = ck.shape[1]
    bb = SAMPLE_BATCH_BLOCK
    blk = lambda r: pl.BlockSpec((bb, r, LANES), lambda i: (i, 0, 0))
    return pl.pallas_call(
        functools.partial(_swa_sample_body, t_new),
        grid=(bsz // bb,),
        in_specs=[_resident(sink_col.shape), blk(rows), blk(w), blk(w), blk(t_new), blk(t_new)],
        out_specs=[blk(rows), blk(w), blk(w)],
        out_shape=[jax.ShapeDtypeStruct((bsz, rows, LANES), F32),
                   jax.ShapeDtypeStruct(ck.shape, F32), jax.ShapeDtypeStruct(cv.shape, F32)],
        compiler_params=_params(1),
        name="swa_sample",
    )(sink_col, q_pad, ck, cv, kn, vn)


def _merge_body(og_ref, r_ref, os_ref, x_ref, gn_ref, wog_ref, wos_ref, gf_ref, wq_ref, keys_ref,
                h_ref, hn_ref, st_ref):
    og = og_ref[...]
    gn = gn_ref[...]
    heads = [_rms(og[:, i * GLA_DV:(i + 1) * GLA_DV], gn) for i in range(GLA_HEADS)]
    r = r_ref[...]
    gated = jnp.concatenate(heads, axis=1) * (r / (1.0 + jnp.exp(-r)))
    h = (x_ref[...]
         + jnp.dot(gated.astype(BF16), wog_ref[...], preferred_element_type=F32)
         + jnp.dot(os_ref[...].astype(BF16), wos_ref[...], preferred_element_type=F32))
    h_ref[...] = h
    hn_f32 = _rms(h, gf_ref[...])
    hn = hn_f32.astype(BF16)
    hn_ref[...] = hn_f32.T.astype(BF16)
    pq = jnp.dot(hn, wq_ref[...], preferred_element_type=F32).astype(BF16)
    for i in range(PEER_HEADS):
        st_ref[i] = lax.dot_general(keys_ref[i], pq[:, i * PEER_QDIM:(i + 1) * PEER_QDIM], NT_DIMS,
                                    preferred_element_type=F32)


def _merge(o_gla, r, o_swa, x, gn, wog, wos, gf, wq, keys):
    t, d = x.shape
    tm = PROJ_ROWS
    row = lambda i: (i, 0)
    return pl.pallas_call(
        _merge_body,
        grid=(t // tm,),
        in_specs=[pl.BlockSpec((tm, GLA_WIDTH), row), pl.BlockSpec((tm, GLA_WIDTH), row),
                  pl.BlockSpec((tm, o_swa.shape[1]), row), pl.BlockSpec((tm, d), row),
                  _resident(gn.shape), _resident(wog.shape), _resident(wos.shape),
                  _resident(gf.shape), _resident(wq.shape), _resident(keys.shape)],
        out_specs=[pl.BlockSpec((tm, d), row), pl.BlockSpec((d, tm), lambda i: (0, i)),
                   pl.BlockSpec((PEER_HEADS, 2 * PEER_N_KEYS, tm), lambda i: (0, 0, i))],
        out_shape=[jax.ShapeDtypeStruct((t, d), F32), jax.ShapeDtypeStruct((d, t), BF16),
                   jax.ShapeDtypeStruct((PEER_HEADS, 2 * PEER_N_KEYS, t), F32)],
        compiler_params=_params(1),
        name="merge",
    )(o_gla, r, o_swa, x, gn, wog, wos, gf, wq, keys)


N_SORTED = PEER_TOPK + 1


def _topk_body(st_ref, aux_ref, top0_scr, top1_scr):
    scr = (top0_scr, top1_scr)
    for h in range(PEER_HEADS):
        for half in range(2):
            arr = st_ref[h, half * PEER_N_KEYS:(half + 1) * PEER_N_KEYS, :]
            for a in range(N_SORTED):
                m = jnp.max(arr, axis=0, keepdims=True)
                scr[half][a, h:h + 1, :] = m
                if a + 1 < N_SORTED:
                    arr = jnp.where(arr == m, NEG_INF, arr)
    top0 = [top0_scr[a] for a in range(N_SORTED)]
    top1 = [top1_scr[a] for a in range(N_SORTED)]
    cands = [top0[a - 1] + top1[b - 1]
             for a in range(1, N_SORTED + 1) for b in range(1, N_SORTED // a + 1)]
    best = None
    z = None
    for rank in range(N_SORTED):
        m = functools.reduce(jnp.maximum, cands)
        if rank == 0:
            best = m
            z = jnp.ones_like(m)
        elif rank < PEER_TOPK:
            z = z + jnp.exp(m - best)
        if rank == PEER_TOPK - 1:
            kth = m
        if rank + 1 < N_SORTED:
            cands = [jnp.where(cnd == m, NEG_INF, cnd) for cnd in cands]
    aux_ref[0] = 0.5 * (kth + m)
    aux_ref[1] = top0[0]
    aux_ref[2] = top1[0]
    aux_ref[3] = 1.0 / z


def _topk(st):
    t = st.shape[2]
    tt = TOPK_TOKENS
    return pl.pallas_call(
        _topk_body,
        grid=(t // tt,),
        in_specs=[pl.BlockSpec((PEER_HEADS, 2 * PEER_N_KEYS, tt), lambda i: (0, 0, i))],
        out_specs=pl.BlockSpec((4, PEER_HEADS, tt), lambda i: (0, 0, i)),
        out_shape=jax.ShapeDtypeStruct((4, PEER_HEADS, t), F32),
        scratch_shapes=[pltpu.VMEM((N_SORTED, PEER_HEADS, tt), F32),
                        pltpu.VMEM((N_SORTED, PEER_HEADS, tt), F32)],
        compiler_params=_params(1),
        name="peer_topk",
    )(st)


def _peer_body(st_ref, aux_ref, hnt_ref, u_ref, vt_ref, out_ref,
               thr_scr, w0_scr, e1_scr, thr_cur, w0_cur, a_scr, c_scr, acc_scr):
    ec = pl.program_id(1)
    n_rows = PEER_CHUNK // PEER_N_KEYS
    tb = hnt_ref.shape[1]

    @pl.when(ec == 0)
    def _():
        grid_rows = (PEER_N_KEYS // n_rows, n_rows, tb)
        for h in range(PEER_HEADS):
            s0 = st_ref[h, 0:PEER_N_KEYS, :]
            s1 = st_ref[h, PEER_N_KEYS:2 * PEER_N_KEYS, :]
            thr_scr[h] = jnp.exp(aux_ref[0, h:h + 1, :] - s0 - aux_ref[2, h:h + 1, :]).reshape(grid_rows)
            w0_scr[h] = (jnp.exp(s0 - aux_ref[1, h:h + 1, :]) * aux_ref[3, h:h + 1, :]).reshape(grid_rows)
            e1_scr[h] = jnp.exp(s1 - aux_ref[2, h:h + 1, :])
        acc_scr[...] = jnp.zeros_like(acc_scr)

    sub = 32
    inv_sqrt2 = 1.0 / math.sqrt(2.0)
    half = PEER_CHUNK // 2

    for h in range(PEER_HEADS):
        thr_cur[h] = thr_scr[h, ec]
        w0_cur[h] = w0_scr[h, ec]

    def scores(p):
        rows = slice(p * half, (p + 1) * half)
        a_scr[rows, :] = jnp.dot(u_ref[rows, :], hnt_ref[...], preferred_element_type=F32)

    def coefficients(p):
        for ii in range(p * n_rows // 2, (p + 1) * n_rows // 2):
            for lt in range(tb // LANES):
                ls = slice(lt * LANES, (lt + 1) * LANES)
                thr_b = [thr_cur[h, ii:ii + 1, ls] for h in range(PEER_HEADS)]
                w0_b = [w0_cur[h, ii:ii + 1, ls] for h in range(PEER_HEADS)]
                for jb in range(PEER_N_KEYS // sub):
                    rows = slice(ii * PEER_N_KEYS + jb * sub, ii * PEER_N_KEYS + (jb + 1) * sub)
                    a = a_scr[rows, ls]
                    act = 0.5 * a * (1.0 + lax.erf(a * inv_sqrt2))
                    g = None
                    for h in range(PEER_HEADS):
                        e1 = e1_scr[h, jb * sub:(jb + 1) * sub, ls]
                        term = jnp.where(e1 >= thr_b[h], e1, 0.0) * w0_b[h]
                        g = term if g is None else g + term
                    c_scr[rows, ls] = (g * act).astype(BF16)

    def accumulate(p):
        rows = slice(p * half, (p + 1) * half)
        acc_scr[...] += jnp.dot(vt_ref[:, rows], c_scr[rows, :], preferred_element_type=F32)

    scores(0)
    scores(1)
    coefficients(0)
    accumulate(0)
    coefficients(1)
    accumulate(1)

    @pl.when(ec == pl.num_programs(1) - 1)
    def _():
        out_ref[...] = acc_scr[...].T


def _peer(st, aux, hnt, u, vt):
    d, t = hnt.shape
    tb = PEER_TOKENS
    ec = PEER_CHUNK
    n_exp = u.shape[0]
    return pl.pallas_call(
        _peer_body,
        grid=(t // tb, n_exp // ec),
        in_specs=[pl.BlockSpec((PEER_HEADS, 2 * PEER_N_KEYS, tb), lambda i, j: (0, 0, i),
                               pipeline_mode=pl.Buffered(1)),
                  pl.BlockSpec((4, PEER_HEADS, tb), lambda i, j: (0, 0, i)),
                  pl.BlockSpec((d, tb), lambda i, j: (0, i), pipeline_mode=pl.Buffered(1)),
                  pl.BlockSpec((ec, d), lambda i, j: (j, 0)),
                  pl.BlockSpec((d, ec), lambda i, j: (0, j))],
        out_specs=pl.BlockSpec((tb, d), lambda i, j: (i, 0)),
        out_shape=jax.ShapeDtypeStruct((t, d), F32),
        scratch_shapes=[pltpu.VMEM((PEER_HEADS, n_exp // ec, ec // PEER_N_KEYS, tb), F32),
                        pltpu.VMEM((PEER_HEADS, n_exp // ec, ec // PEER_N_KEYS, tb), F32),
                        pltpu.VMEM((PEER_HEADS, PEER_N_KEYS, tb), F32),
                        pltpu.VMEM((PEER_HEADS, ec // PEER_N_KEYS, tb), F32),
                        pltpu.VMEM((PEER_HEADS, ec // PEER_N_KEYS, tb), F32),
                        pltpu.VMEM((ec, tb), F32),
                        pltpu.VMEM((ec, tb), BF16),
                        pltpu.VMEM((d, tb), F32)],
        compiler_params=_params(2),
        name="peer_dense",
    )(st, aux, hnt, u, vt)


def _final_body(h_ref, p_ref, g_ref, y_ref):
    y_ref[...] = _rms(h_ref[...] + p_ref[...], g_ref[...])


def _final(h, p, g):
    t, d = h.shape
    tm = PROJ_ROWS
    row = lambda i: (i, 0)
    return pl.pallas_call(
        _final_body,
        grid=(t // tm,),
        in_specs=[pl.BlockSpec((tm, d), row), pl.BlockSpec((tm, d), row), _resident(g.shape)],
        out_specs=pl.BlockSpec((tm, d), row),
        out_shape=jax.ShapeDtypeStruct((t, d), F32),
        compiler_params=_params(1),
        name="final",
    )(h, p, g)


def _rope_tables(pos):
    half = SWA_HEAD_DIM // 2
    inv = jnp.exp(-math.log(ROPE_THETA) * jnp.arange(half, dtype=F32) * (2.0 / SWA_HEAD_DIM))
    ang = pos.astype(F32)[:, None] * inv[None, :]
    cos, sin = jnp.cos(ang), jnp.sin(ang)
    reps = LANES // SWA_HEAD_DIM
    return (jnp.tile(jnp.concatenate([cos, cos], axis=1), (1, reps)),
            jnp.tile(jnp.concatenate([-sin, sin], axis=1), (1, reps)))


def _split_w_in(w_in):
    out, start = [], 0
    for width in (512, 512, 1024, 1024, GLA_GATE_RANK, SWA_WIDTH, LANES, LANES):
        out.append(w_in[:, start:start + width])
        start += width
    return out


def _pad_heads_to_kv_half(w):
    lead = w.shape[:-1]
    w = w.reshape(lead + (SWA_KV_HEADS, SWA_GROUP, 1, SWA_HEAD_DIM))
    sel = jnp.eye(SWA_KV_HEADS, dtype=w.dtype).reshape(SWA_KV_HEADS, 1, SWA_KV_HEADS, 1)
    return (w * sel).reshape(lead + (SWA_HEADS * LANES,))


def kernel(x_prompt, x_sample, state_gla, cache_win_k, cache_win_v, g_attn, w_in, w_gate_up, b_gate,
           g_gla_out, attn_sinks, w_out, g_ffn, w_peer_q, peer_sub_keys, peer_u, peer_v, g_final):
    bsz, seq, d = x_prompt.shape
    dbsz, dseq, _ = x_sample.shape
    win = cache_win_k.shape[2]

    gq, gk, gv, gr, ga, wsq, wsk, wsv = _split_w_in(w_in[0])
    ga = jnp.pad(ga, ((0, 0), (0, LANES - GLA_GATE_RANK)))
    w_prompt = jnp.concatenate([gq, gk, gv, gr, wsq, wsk, wsv, ga], axis=1).astype(BF16)
    w_sample = jnp.concatenate([gq, gk, gv, gr, _pad_heads_to_kv_half(wsq), wsk, wsv, ga],
                               axis=1).astype(BF16)
    wgu = jnp.pad(w_gate_up[0], ((0, LANES - GLA_GATE_RANK), (0, 0)))
    bg = b_gate[0][None, :]
    g_a = g_attn[0][None, :]
    g_f = g_ffn[0][None, :]
    g_n = g_gla_out[0][None, :]
    wo_gla = w_out[0, :GLA_WIDTH].astype(BF16)
    wo_swa = w_out[0, GLA_WIDTH:]
    wo_swa_pad = _pad_heads_to_kv_half(wo_swa.T).T.astype(BF16)
    wo_swa = wo_swa.astype(BF16)
    wq = w_peer_q[0].astype(BF16)
    sk0 = jnp.pad(peer_sub_keys[0, :, 0], ((0, 0), (0, 0), (0, PEER_HALF)))
    sk1 = jnp.pad(peer_sub_keys[0, :, 1], ((0, 0), (0, 0), (PEER_HALF, 0)))
    keys = jnp.concatenate([sk0, sk1], axis=1).astype(BF16)
    u = peer_u[0].astype(BF16)
    vt = peer_v[0].T.astype(BF16)
    sinks = attn_sinks[0]
    sink_col = jnp.tile(sinks, dseq)[:, None]

    cos_p, sin_p = _rope_tables(jnp.arange(seq))
    cos_s, sin_s = _rope_tables(PAST_LEN + jnp.arange(dseq))
    cos_s = jnp.tile(cos_s, (PROJ_ROWS // dseq, 1))
    sin_s = jnp.tile(sin_s, (PROJ_ROWS // dseq, 1))

    def ffn(h, hn, st):
        aux = _topk(st)
        return _final(h, _peer(st, aux, hn, u, vt), g_final[None, :])

    tp = bsz * seq
    q, k, v, r, la, sq, sk, sv = _proj(x_prompt.reshape(tp, d), g_a, w_prompt, wgu, bg, cos_p, sin_p,
                                       SWA_WIDTH)
    b3 = lambda a: a.reshape(bsz, seq, a.shape[-1])
    o_gla, s_prompt = _gla_prompt(b3(q), b3(k), b3(v), b3(la))
    o_swa = _swa_prompt(b3(sq), b3(sk), b3(sv), sinks)
    h, hn, st = _merge(o_gla.reshape(tp, GLA_WIDTH), r, o_swa.reshape(tp, SWA_WIDTH),
                       x_prompt.reshape(tp, d), g_n, wo_gla, wo_swa, g_f, wq, keys)
    y_prompt = ffn(h, hn, st).reshape(bsz, seq, d)
    kv_shape = (1, bsz, WINDOW, SWA_KV_HEADS, SWA_HEAD_DIM)
    k_prompt = b3(sk)[:, seq - WINDOW:].reshape(kv_shape)
    v_prompt = b3(sv)[:, seq - WINDOW:].reshape(kv_shape)

    ts = dbsz * dseq
    q, k, v, r, la, sq, sk, sv = _proj(x_sample.reshape(ts, d), g_a, w_sample, wgu, bg, cos_s, sin_s,
                                       SWA_HEADS * LANES)
    d3 = lambda a: a.reshape(dbsz, dseq, a.shape[-1])
    o_gla, s_sample = _gla_sample(d3(q), d3(k), d3(v), d3(la), state_gla[0])
    o_swa, k_sample, v_sample = _swa_sample(
        sq.reshape(dbsz, dseq * SWA_HEADS, LANES),
        cache_win_k[0].reshape(dbsz, win, LANES), cache_win_v[0].reshape(dbsz, win, LANES),
        d3(sk), d3(sv), sink_col)
    h, hn, st = _merge(o_gla.reshape(ts, GLA_WIDTH), r, o_swa.reshape(ts, SWA_HEADS * LANES),
                       x_sample.reshape(ts, d), g_n, wo_gla, wo_swa_pad, g_f, wq, keys)
    y_sample = ffn(h, hn, st).reshape(dbsz, dseq, d)
    ckv_shape = (1, dbsz, win, SWA_KV_HEADS, SWA_HEAD_DIM)

    return (y_prompt, y_sample, s_prompt[None], k_prompt, v_prompt, s_sample[None],
            k_sample.reshape(ckv_shape), v_sample.reshape(ckv_shape))
```

```python
import functools
import math

import jax
import jax.numpy as jnp
from jax import lax
from jax.experimental import pallas as pl
from jax.experimental.pallas import tpu as pltpu

F32 = jnp.float32
BF16 = jnp.bfloat16
HIGHEST = lax.Precision.HIGHEST

D_MODEL = 2048
GLA_HEADS = 4
GLA_DK = 128
GLA_DV = 256
GLA_WIDTH = GLA_HEADS * GLA_DV
GLA_GATE_RANK = 16
GLA_TAU = 16.0
GLA_CHUNK = 64
SWA_HEAD_DIM = 64
SWA_HEADS = 16
SWA_KV_HEADS = 2
SWA_GROUP = SWA_HEADS // SWA_KV_HEADS
SWA_WIDTH = SWA_HEADS * SWA_HEAD_DIM
WINDOW = 128
ROPE_THETA = 10000.0
PAST_LEN = 8192
PEER_N_KEYS = 128
PEER_EXPERTS = PEER_N_KEYS * PEER_N_KEYS
PEER_HEADS = 8
PEER_QDIM = 128
PEER_HALF = PEER_QDIM // 2
PEER_TOPK = 16
EPS = 1e-6
NEG_INF = -1e30

LANES = 128
PROJ_ROWS = 256
TOPK_TOKENS = 128
PEER_TOKENS = 512
PEER_CHUNK = 1024
PEER_PARTS = 2
SAMPLE_BATCH_BLOCK = 8
VMEM_LIMIT = 60 * 1024 * 1024

NT_DIMS = (((1,), (1,)), ((), ()))


def _resident(shape):
    nd = len(shape)
    return pl.BlockSpec(shape, lambda *_: (0,) * nd, pipeline_mode=pl.Buffered(1))


def _params(n_axes):
    return pltpu.CompilerParams(dimension_semantics=("arbitrary",) * n_axes,
                                vmem_limit_bytes=VMEM_LIMIT)


def _rms(x, g):
    return x * lax.rsqrt(jnp.mean(x * x, axis=-1, keepdims=True) + EPS) * g


def _proj_body(wsq, x_ref, g_ref, w_ref, wgu_ref, bg_ref, cos_ref, sin_ref,
               q_ref, k_ref, v_ref, r_ref, la_ref, sq_ref, sk_ref, sv_ref):
    xn = _rms(x_ref[...], g_ref[...]).astype(BF16)

    def proj(start, width):
        return jnp.dot(xn, w_ref[:, start:start + width], preferred_element_type=F32)

    nk = GLA_HEADS * GLA_DK
    q_ref[...] = proj(0, nk) * (GLA_DK ** -0.5)
    k_ref[...] = proj(nk, nk)
    v_ref[...] = proj(2 * nk, GLA_WIDTH)
    r_ref[...] = proj(2 * nk + GLA_WIDTH, GLA_WIDTH)
    o_sq = 2 * nk + 2 * GLA_WIDTH
    o_sk = o_sq + wsq
    o_sv = o_sk + LANES
    o_ga = o_sv + LANES

    ga = proj(o_ga, LANES)
    z = jnp.dot(ga, wgu_ref[...], precision=HIGHEST, preferred_element_type=F32) + bg_ref[...]
    la_ref[...] = (jnp.minimum(z, 0.0) - jnp.log1p(jnp.exp(-jnp.abs(z)))) * (1.0 / GLA_TAU)

    cos = cos_ref[...]
    sin = sin_ref[...]
    lane = lax.broadcasted_iota(jnp.int32, cos.shape, 1)
    first_half = (lane & (SWA_HEAD_DIM // 2)) == 0

    def rope(t):
        rot = jnp.where(first_half, pltpu.roll(t, LANES - SWA_HEAD_DIM // 2, 1),
                        pltpu.roll(t, SWA_HEAD_DIM // 2, 1))
        return t * cos + rot * sin

    for c in range(wsq // LANES):
        sq_ref[:, c * LANES:(c + 1) * LANES] = rope(proj(o_sq + c * LANES, LANES))
    sk_ref[...] = rope(proj(o_sk, LANES))
    sv_ref[...] = proj(o_sv, LANES)


def _proj(x, g, w, wgu, bg, cos, sin, wsq):
    t, d = x.shape
    tm = PROJ_ROWS
    n_tab = cos.shape[0] // tm
    row = lambda i: (i, 0)
    tab = lambda i: (i % n_tab, 0)
    widths = (512, 512, 1024, 1024, 512, wsq, LANES, LANES)
    return pl.pallas_call(
        functools.partial(_proj_body, wsq),
        grid=(t // tm,),
        in_specs=[pl.BlockSpec((tm, d), row), _resident(g.shape), _resident(w.shape),
                  _resident(wgu.shape), _resident(bg.shape),
                  pl.BlockSpec((tm, LANES), tab), pl.BlockSpec((tm, LANES), tab)],
        out_specs=[pl.BlockSpec((tm, n), row) for n in widths],
        out_shape=[jax.ShapeDtypeStruct((t, n), F32) for n in widths],
        compiler_params=_params(1),
        name="proj",
    )(x, g, w, wgu, bg, cos, sin)


def _gla_prompt_body(q_ref, k_ref, v_ref, la_ref, o_ref, sfin_ref, s_scr):
    n = pl.program_id(0)
    bsz = q_ref.shape[0]

    @pl.when(n == 0)
    def _():
        s_scr[...] = jnp.zeros_like(s_scr)

    c = GLA_CHUNK
    row = lax.broadcasted_iota(jnp.int32, (c, c), 0)
    col = lax.broadcasted_iota(jnp.int32, (c, c), 1)
    causal = row >= col
    tri = causal.astype(F32)
    for ch in range(q_ref.shape[1] // c):
        rows = slice(ch * c, (ch + 1) * c)
        for bi in range(bsz):
            b_all = jnp.dot(tri, la_ref[bi, rows, :], precision=HIGHEST, preferred_element_type=F32)
            for h in range(GLA_HEADS):
                kl = slice(h * GLA_DK, (h + 1) * GLA_DK)
                vl = slice(h * GLA_DV, (h + 1) * GLA_DV)
                b = b_all[:, kl]
                b_last = b[c - 1:c, :]
                k = k_ref[bi, rows, kl]
                q_t = (q_ref[bi, rows, kl] * jnp.exp(b)).astype(BF16)
                k_t = (k * jnp.exp(-b)).astype(BF16)
                k_e = k * jnp.exp(b_last - b)
                vb = v_ref[bi, rows, vl].astype(BF16)
                a = lax.dot_general(q_t, k_t, NT_DIMS, preferred_element_type=F32)
                a = jnp.where(causal, a, 0.0).astype(BF16)
                s = s_scr[bi * GLA_HEADS + h]
                o_ref[bi, rows, vl] = (jnp.dot(a, vb, preferred_element_type=F32)
                                       + jnp.dot(q_t, s.astype(BF16), preferred_element_type=F32))
                decay = jnp.exp(jnp.broadcast_to(b_last, (GLA_DK, GLA_DK))).T
                decay = jnp.concatenate([decay, decay], axis=1)
                s_scr[bi * GLA_HEADS + h] = decay * s + jnp.dot(k_e.T.astype(BF16), vb,
                                                                preferred_element_type=F32)

    @pl.when(n == pl.num_programs(0) - 1)
    def _():
        sfin_ref[...] = s_scr[...].reshape(sfin_ref.shape)


GLA_CHUNKS_PER_STEP = 2


def _gla_prompt(q, k, v, la):
    bsz, s, _ = q.shape
    rows = GLA_CHUNK * GLA_CHUNKS_PER_STEP
    qk_spec = pl.BlockSpec((bsz, rows, GLA_HEADS * GLA_DK), lambda n: (0, n, 0))
    v_spec = pl.BlockSpec((bsz, rows, GLA_WIDTH), lambda n: (0, n, 0))
    return pl.pallas_call(
        _gla_prompt_body,
        grid=(s // rows,),
        in_specs=[qk_spec, qk_spec, v_spec, qk_spec],
        out_specs=[v_spec, pl.BlockSpec((bsz, GLA_HEADS, GLA_DK, GLA_DV), lambda n: (0, 0, 0, 0))],
        out_shape=[jax.ShapeDtypeStruct((bsz, s, GLA_WIDTH), F32),
                   jax.ShapeDtypeStruct((bsz, GLA_HEADS, GLA_DK, GLA_DV), F32)],
        scratch_shapes=[pltpu.VMEM((bsz * GLA_HEADS, GLA_DK, GLA_DV), F32)],
        compiler_params=_params(1),
        name="gla_prompt",
    )(q, k, v, la)


def _gla_sample_body(q_ref, k_ref, v_ref, la_ref, s0_ref, o_ref, sout_ref):
    bb, t, _ = q_ref.shape
    rows = bb * t
    q = q_ref[...].reshape(rows, GLA_DK)
    k = k_ref[...].reshape(rows, GLA_DK)
    la = la_ref[...].reshape(rows, GLA_DK)
    v = v_ref[...].reshape(rows, GLA_DV)
    row = lax.broadcasted_iota(jnp.int32, (rows, rows), 0)
    col = lax.broadcasted_iota(jnp.int32, (rows, rows), 1)
    same_seq = (row // t) == (col // t)
    causal = same_seq & (row >= col)
    b = jnp.dot(causal.astype(F32), la, precision=HIGHEST, preferred_element_type=F32)
    b_tot = jnp.dot(same_seq.astype(F32), la, precision=HIGHEST, preferred_element_type=F32)
    q_t = (q * jnp.exp(b)).astype(BF16)
    k_t = (k * jnp.exp(-b)).astype(BF16)
    k_e_t = (k * jnp.exp(b_tot - b)).T.astype(BF16)
    la_t = la.T
    a = lax.dot_general(q_t, k_t, NT_DIMS, preferred_element_type=F32)
    a = jnp.where(causal, a, 0.0).astype(BF16)
    o = jnp.dot(a, v.astype(BF16), preferred_element_type=F32)
    seq_of_row_v = lax.broadcasted_iota(jnp.int32, (rows, GLA_DV), 0) // t
    seq_of_row_k = lax.broadcasted_iota(jnp.int32, (rows, GLA_DK), 0) // t
    for i in range(bb):
        s0 = s0_ref[i]
        mine = seq_of_row_v == i
        o = o + jnp.where(mine, jnp.dot(q_t, s0.astype(BF16), preferred_element_type=F32), 0.0)
        v_i = jnp.where(mine, v, 0.0).astype(BF16)
        ones_i = jnp.where(seq_of_row_k == i, 1.0, 0.0)
        decay = jnp.exp(jnp.dot(la_t, ones_i, precision=HIGHEST, preferred_element_type=F32))
        decay = jnp.concatenate([decay, decay], axis=1)
        sout_ref[i] = decay * s0 + jnp.dot(k_e_t, v_i, preferred_element_type=F32)
    o_ref[...] = o.reshape(bb, t, GLA_DV)


def _gla_sample(q, k, v, la, s0):
    bsz, t, _ = q.shape
    bb = SAMPLE_BATCH_BLOCK
    qk_spec = pl.BlockSpec((bb, t, GLA_DK), lambda i, h: (i, 0, h))
    v_spec = pl.BlockSpec((bb, t, GLA_DV), lambda i, h: (i, 0, h))
    s_spec = pl.BlockSpec((bb, None, GLA_DK, GLA_DV), lambda i, h: (i, h, 0, 0))
    return pl.pallas_call(
        _gla_sample_body,
        grid=(bsz // bb, GLA_HEADS),
        in_specs=[qk_spec, qk_spec, v_spec, qk_spec, s_spec],
        out_specs=[v_spec, s_spec],
        out_shape=[jax.ShapeDtypeStruct((bsz, t, GLA_WIDTH), F32),
                   jax.ShapeDtypeStruct(s0.shape, F32)],
        compiler_params=_params(2),
        name="gla_sample",
    )(q, k, v, la, s0)


def _sink_softmax(s, mask, sink):
    s = jnp.where(mask, s, NEG_INF)
    m = jnp.maximum(jnp.max(s, axis=-1, keepdims=True), sink)
    p = jnp.exp(s - m)
    return p / (jnp.sum(p, axis=-1, keepdims=True) + jnp.exp(sink - m))


def _split_kv_lanes(x):
    lane = lax.broadcasted_iota(jnp.int32, x.shape, 1)
    low = lane < SWA_HEAD_DIM
    lo0 = jnp.where(low, x, 0.0)
    hi1 = jnp.where(low, 0.0, x)
    hi0 = pltpu.roll(lo0, SWA_HEAD_DIM, 1)
    lo1 = pltpu.roll(hi1, SWA_HEAD_DIM, 1)
    return (lo0.astype(BF16), lo1.astype(BF16)), (hi0.astype(BF16), hi1.astype(BF16))


def _swa_prompt_body(sink_ref, q_ref, kc_ref, kp_ref, vc_ref, vp_ref, o_ref):
    n = pl.program_id(1)
    blk = WINDOW
    k_parts = _split_kv_lanes(jnp.concatenate([kp_ref[...], kc_ref[...]], axis=0))
    v_parts = _split_kv_lanes(jnp.concatenate([vp_ref[...], vc_ref[...]], axis=0))
    qi = lax.broadcasted_iota(jnp.int32, (blk, 2 * blk), 0)
    kj = lax.broadcasted_iota(jnp.int32, (blk, 2 * blk), 1)
    diff = qi + blk - kj
    mask = (diff >= 0) & (diff < WINDOW) & ((n > 0) | (kj >= blk))
    scale = SWA_HEAD_DIM ** -0.5
    for pair in range(SWA_HEADS // 2):
        kv = (2 * pair) // SWA_GROUP
        qp = q_ref[:, pair * LANES:(pair + 1) * LANES].astype(BF16)
        acc = None
        for half in range(2):
            s = lax.dot_general(qp, k_parts[half][kv], NT_DIMS, preferred_element_type=F32) * scale
            p = _sink_softmax(s, mask, sink_ref[2 * pair + half])
            o = jnp.dot(p.astype(BF16), v_parts[half][kv], preferred_element_type=F32)
            acc = o if acc is None else acc + o
        o_ref[:, pair * LANES:(pair + 1) * LANES] = acc


def _swa_prompt(sq, sk, sv, sinks):
    bsz, s, _ = sq.shape
    blk = WINDOW
    cur = lambda b, n: (b, n, 0)
    prev = lambda b, n: (b, jnp.maximum(n - 1, 0), 0)
    kv_cur = pl.BlockSpec((None, blk, LANES), cur)
    kv_prev = pl.BlockSpec((None, blk, LANES), prev)
    q_spec = pl.BlockSpec((None, blk, SWA_WIDTH), cur)
    return pl.pallas_call(
        _swa_prompt_body,
        grid=(bsz, s // blk),
        in_specs=[pl.BlockSpec(memory_space=pltpu.SMEM), q_spec, kv_cur, kv_prev, kv_cur, kv_prev],
        out_specs=q_spec,
        out_shape=jax.ShapeDtypeStruct((bsz, s, SWA_WIDTH), F32),
        compiler_params=_params(2),
        name="swa_prompt",
    )(sinks, sq, sk, sk, sv, sv)


def _swa_sample_body(t_new, sink_ref, q_ref, ck_ref, cv_ref, kn_ref, vn_ref, o_ref, nk_ref, nv_ref):
    bb, rows, _ = q_ref.shape
    w = ck_ref.shape[1]
    pad = jnp.zeros((16 - t_new, LANES), F32)
    n_keys = w + 16
    tq = lax.broadcasted_iota(jnp.int32, (rows, n_keys), 0) // SWA_HEADS
    c = lax.broadcasted_iota(jnp.int32, (rows, n_keys), 1)
    mask = ((c < w) & (c > tq)) | ((c >= w) & (c - w <= tq) & (c < w + t_new))
    sink = sink_ref[...]
    scale = SWA_HEAD_DIM ** -0.5
    for i in range(bb):
        ck, cv, kn, vn = ck_ref[i], cv_ref[i], kn_ref[i], vn_ref[i]
        kcat = jnp.concatenate([ck, kn, pad], axis=0).astype(BF16)
        vcat = jnp.concatenate([cv, vn, pad], axis=0).astype(BF16)
        s = lax.dot_general(q_ref[i].astype(BF16), kcat, NT_DIMS, preferred_element_type=F32) * scale
        p = _sink_softmax(s, mask, sink)
        o_ref[i] = jnp.dot(p.astype(BF16), vcat, preferred_element_type=F32)
        nk_ref[i, 0:w - t_new, :] = ck[t_new:, :]
        nk_ref[i, w - t_new:w, :] = kn
        nv_ref[i, 0:w - t_new, :] = cv[t_new:, :]
        nv_ref[i, w - t_new:w, :] = vn


def _swa_sample(q_pad, ck, cv, kn, vn, sink_col):
    bsz, rows, _ = q_pad.shape
    t_new = kn.shape[1]
    w = ck.shape[1]
    bb = SAMPLE_BATCH_BLOCK
    blk = lambda r: pl.BlockSpec((bb, r, LANES), lambda i: (i, 0, 0))
    return pl.pallas_call(
        functools.partial(_swa_sample_body, t_new),
        grid=(bsz // bb,),
        in_specs=[_resident(sink_col.shape), blk(rows), blk(w), blk(w), blk(t_new), blk(t_new)],
        out_specs=[blk(rows), blk(w), blk(w)],
        out_shape=[jax.ShapeDtypeStruct((bsz, rows, LANES), F32),
                   jax.ShapeDtypeStruct(ck.shape, F32), jax.ShapeDtypeStruct(cv.shape, F32)],
        compiler_params=_params(1),
        name="swa_sample",
    )(sink_col, q_pad, ck, cv, kn, vn)


def _merge_body(og_ref, r_ref, os_ref, x_ref, gn_ref, wog_ref, wos_ref, gf_ref, wq_ref, keys_ref,
                h_ref, hn_ref, st_ref):
    og = og_ref[...]
    gn = gn_ref[...]
    heads = [_rms(og[:, i * GLA_DV:(i + 1) * GLA_DV], gn) for i in range(GLA_HEADS)]
    r = r_ref[...]
    gated = jnp.concatenate(heads, axis=1) * (r / (1.0 + jnp.exp(-r)))
    h = (x_ref[...]
         + jnp.dot(gated.astype(BF16), wog_ref[...], preferred_element_type=F32)
         + jnp.dot(os_ref[...].astype(BF16), wos_ref[...], preferred_element_type=F32))
    h_ref[...] = h
    hn_f32 = _rms(h, gf_ref[...])
    hn = hn_f32.astype(BF16)
    hn_ref[...] = hn_f32.T.astype(BF16)
    pq = jnp.dot(hn, wq_ref[...], preferred_element_type=F32).astype(BF16)
    for i in range(PEER_HEADS):
        st_ref[i] = lax.dot_general(keys_ref[i], pq[:, i * PEER_QDIM:(i + 1) * PEER_QDIM], NT_DIMS,
                                    preferred_element_type=F32)


def _merge(o_gla, r, o_swa, x, gn, wog, wos, gf, wq, keys):
    t, d = x.shape
    tm = PROJ_ROWS
    row = lambda i: (i, 0)
    return pl.pallas_call(
        _merge_body,
        grid=(t // tm,),
        in_specs=[pl.BlockSpec((tm, GLA_WIDTH), row), pl.BlockSpec((tm, GLA_WIDTH), row),
                  pl.BlockSpec((tm, o_swa.shape[1]), row), pl.BlockSpec((tm, d), row),
                  _resident(gn.shape), _resident(wog.shape), _resident(wos.shape),
                  _resident(gf.shape), _resident(wq.shape), _resident(keys.shape)],
        out_specs=[pl.BlockSpec((tm, d), row), pl.BlockSpec((d, tm), lambda i: (0, i)),
                   pl.BlockSpec((PEER_HEADS, 2 * PEER_N_KEYS, tm), lambda i: (0, 0, i))],
        out_shape=[jax.ShapeDtypeStruct((t, d), F32), jax.ShapeDtypeStruct((d, t), BF16),
                   jax.ShapeDtypeStruct((PEER_HEADS, 2 * PEER_N_KEYS, t), F32)],
        compiler_params=_params(1),
        name="merge",
    )(o_gla, r, o_swa, x, gn, wog, wos, gf, wq, keys)


N_SORTED = PEER_TOPK + 1


def _topk_body(st_ref, aux_ref, top0_scr, top1_scr):
    scr = (top0_scr, top1_scr)
    for h in range(PEER_HEADS):
        for half in range(2):
            arr = st_ref[h, half * PEER_N_KEYS:(half + 1) * PEER_N_KEYS, :]
            for a in range(N_SORTED):
                m = jnp.max(arr, axis=0, keepdims=True)
                scr[half][a, h:h + 1, :] = m
                if a + 1 < N_SORTED:
                    arr = jnp.where(arr == m, NEG_INF, arr)
    top0 = [top0_scr[a] for a in range(N_SORTED)]
    top1 = [top1_scr[a] for a in range(N_SORTED)]
    cands = [top0[a - 1] + top1[b - 1]
             for a in range(1, N_SORTED + 1) for b in range(1, N_SORTED // a + 1)]
    best = None
    z = None
    for rank in range(N_SORTED):
        m = functools.reduce(jnp.maximum, cands)
        if rank == 0:
            best = m
            z = jnp.ones_like(m)
        elif rank < PEER_TOPK:
            z = z + jnp.exp(m - best)
        if rank == PEER_TOPK - 1:
            kth = m
        if rank + 1 < N_SORTED:
            cands = [jnp.where(cnd == m, NEG_INF, cnd) for cnd in cands]
    aux_ref[0] = 0.5 * (kth + m)
    aux_ref[1] = top0[0]
    aux_ref[2] = top1[0]
    aux_ref[3] = 1.0 / z


def _topk(st):
    t = st.shape[2]
    tt = TOPK_TOKENS
    return pl.pallas_call(
        _topk_body,
        grid=(t // tt,),
        in_specs=[pl.BlockSpec((PEER_HEADS, 2 * PEER_N_KEYS, tt), lambda i: (0, 0, i))],
        out_specs=pl.BlockSpec((4, PEER_HEADS, tt), lambda i: (0, 0, i)),
        out_shape=jax.ShapeDtypeStruct((4, PEER_HEADS, t), F32),
        scratch_shapes=[pltpu.VMEM((N_SORTED, PEER_HEADS, tt), F32),
                        pltpu.VMEM((N_SORTED, PEER_HEADS, tt), F32)],
        compiler_params=_params(1),
        name="peer_topk",
    )(st)


def _peer_body(st_ref, aux_ref, hnt_ref, u_ref, vt_ref, out_ref,
               thr_scr, w0_scr, e1_scr, thr_cur, w0_cur, acc_scr, *piece_scr):
    ec = pl.program_id(1)
    n_rows = PEER_CHUNK // PEER_N_KEYS
    tb = hnt_ref.shape[1]
    a_scr = piece_scr[:PEER_PARTS]
    c_scr = piece_scr[PEER_PARTS:]

    @pl.when(ec == 0)
    def _():
        grid_rows = (PEER_N_KEYS // n_rows, n_rows, tb)
        for h in range(PEER_HEADS):
            s0 = st_ref[h, 0:PEER_N_KEYS, :]
            s1 = st_ref[h, PEER_N_KEYS:2 * PEER_N_KEYS, :]
            thr_scr[h] = jnp.exp(aux_ref[0, h:h + 1, :] - s0 - aux_ref[2, h:h + 1, :]).reshape(grid_rows)
            w0_scr[h] = (jnp.exp(s0 - aux_ref[1, h:h + 1, :]) * aux_ref[3, h:h + 1, :]).reshape(grid_rows)
            e1_scr[h] = jnp.exp(s1 - aux_ref[2, h:h + 1, :])
        acc_scr[...] = jnp.zeros_like(acc_scr)

    sub = 32
    inv_sqrt2 = 1.0 / math.sqrt(2.0)
    e_part = PEER_CHUNK // PEER_PARTS
    rows_per_part = n_rows // PEER_PARTS

    z = lax.shift_right_logical(ec, 30)

    for h in range(PEER_HEADS):
        thr_cur[h] = thr_scr[h, ec]
        w0_cur[h] = w0_scr[h, ec]

    def scores(p):
        a_scr[p][z] = jnp.dot(u_ref[p * e_part:(p + 1) * e_part, :], hnt_ref[...],
                              preferred_element_type=F32)

    def coefficients(p):
        for r in range(rows_per_part):
            ii = p * rows_per_part + r
            for lt in range(tb // LANES):
                ls = slice(lt * LANES, (lt + 1) * LANES)
                thr_b = [thr_cur[h, ii:ii + 1, ls] for h in range(PEER_HEADS)]
                w0_b = [w0_cur[h, ii:ii + 1, ls] for h in range(PEER_HEADS)]
                for jb in range(PEER_N_KEYS // sub):
                    rows = slice(r * PEER_N_KEYS + jb * sub, r * PEER_N_KEYS + (jb + 1) * sub)
                    a = a_scr[p][z, rows, ls]
                    act = 0.5 * a * (1.0 + lax.erf(a * inv_sqrt2))
                    g = None
                    for h in range(PEER_HEADS):
                        e1 = e1_scr[h, jb * sub:(jb + 1) * sub, ls]
                        term = jnp.where(e1 >= thr_b[h], e1, 0.0) * w0_b[h]
                        g = term if g is None else g + term
                    c_scr[p][z, rows, ls] = (g * act).astype(BF16)

    def values(p):
        acc_scr[z] += jnp.dot(vt_ref[:, p * e_part:(p + 1) * e_part], c_scr[p][z],
                              preferred_element_type=F32)

    for p in range(PEER_PARTS):
        scores(p)
    for p in range(PEER_PARTS):
        coefficients(p)
        values(p)

    @pl.when(ec == pl.num_programs(1) - 1)
    def _():
        out_ref[...] = acc_scr[0].T


def _peer(st, aux, hnt, u, vt):
    d, t = hnt.shape
    tb = PEER_TOKENS
    ec = PEER_CHUNK
    n_chunks = u.shape[0] // ec
    rows = ec // PEER_N_KEYS
    return pl.pallas_call(
        _peer_body,
        grid=(t // tb, n_chunks),
        in_specs=[pl.BlockSpec((PEER_HEADS, 2 * PEER_N_KEYS, tb), lambda i, j: (0, 0, i),
                               pipeline_mode=pl.Buffered(1)),
                  pl.BlockSpec((4, PEER_HEADS, tb), lambda i, j: (0, 0, i)),
                  pl.BlockSpec((d, tb), lambda i, j: (0, i), pipeline_mode=pl.Buffered(1)),
                  pl.BlockSpec((ec, d), lambda i, j: (j, 0)),
                  pl.BlockSpec((d, ec), lambda i, j: (0, j))],
        out_specs=pl.BlockSpec((tb, d), lambda i, j: (i, 0)),
        out_shape=jax.ShapeDtypeStruct((t, d), F32),
        scratch_shapes=([pltpu.VMEM((PEER_HEADS, n_chunks, rows, tb), F32),
                         pltpu.VMEM((PEER_HEADS, n_chunks, rows, tb), F32),
                         pltpu.VMEM((PEER_HEADS, PEER_N_KEYS, tb), F32),
                         pltpu.VMEM((PEER_HEADS, rows, tb), F32),
                         pltpu.VMEM((PEER_HEADS, rows, tb), F32),
                         pltpu.VMEM((1, d, tb), F32)]
                        + [pltpu.VMEM((1, ec // PEER_PARTS, tb), F32)] * PEER_PARTS
                        + [pltpu.VMEM((1, ec // PEER_PARTS, tb), BF16)] * PEER_PARTS),
        compiler_params=_params(2),
        name="peer_dense",
    )(st, aux, hnt, u, vt)


def _final_body(h_ref, p_ref, g_ref, y_ref):
    y_ref[...] = _rms(h_ref[...] + p_ref[...], g_ref[...])


def _final(h, p, g):
    t, d = h.shape
    tm = PROJ_ROWS
    row = lambda i: (i, 0)
    return pl.pallas_call(
        _final_body,
        grid=(t // tm,),
        in_specs=[pl.BlockSpec((tm, d), row), pl.BlockSpec((tm, d), row), _resident(g.shape)],
        out_specs=pl.BlockSpec((tm, d), row),
        out_shape=jax.ShapeDtypeStruct((t, d), F32),
        compiler_params=_params(1),
        name="final",
    )(h, p, g)


def _rope_tables(pos):
    half = SWA_HEAD_DIM // 2
    inv = jnp.exp(-math.log(ROPE_THETA) * jnp.arange(half, dtype=F32) * (2.0 / SWA_HEAD_DIM))
    ang = pos.astype(F32)[:, None] * inv[None, :]
    cos, sin = jnp.cos(ang), jnp.sin(ang)
    reps = LANES // SWA_HEAD_DIM
    return (jnp.tile(jnp.concatenate([cos, cos], axis=1), (1, reps)),
            jnp.tile(jnp.concatenate([-sin, sin], axis=1), (1, reps)))


def _split_w_in(w_in):
    out, start = [], 0
    for width in (512, 512, 1024, 1024, GLA_GATE_RANK, SWA_WIDTH, LANES, LANES):
        out.append(w_in[:, start:start + width])
        start += width
    return out


def _pad_heads_to_kv_half(w):
    lead = w.shape[:-1]
    w = w.reshape(lead + (SWA_KV_HEADS, SWA_GROUP, 1, SWA_HEAD_DIM))
    sel = jnp.eye(SWA_KV_HEADS, dtype=w.dtype).reshape(SWA_KV_HEADS, 1, SWA_KV_HEADS, 1)
    return (w * sel).reshape(lead + (SWA_HEADS * LANES,))


def kernel(x_prompt, x_sample, state_gla, cache_win_k, cache_win_v, g_attn, w_in, w_gate_up, b_gate,
           g_gla_out, attn_sinks, w_out, g_ffn, w_peer_q, peer_sub_keys, peer_u, peer_v, g_final):
    bsz, seq, d = x_prompt.shape
    dbsz, dseq, _ = x_sample.shape
    win = cache_win_k.shape[2]

    gq, gk, gv, gr, ga, wsq, wsk, wsv = _split_w_in(w_in[0])
    ga = jnp.pad(ga, ((0, 0), (0, LANES - GLA_GATE_RANK)))
    w_prompt = jnp.concatenate([gq, gk, gv, gr, wsq, wsk, wsv, ga], axis=1).astype(BF16)
    w_sample = jnp.concatenate([gq, gk, gv, gr, _pad_heads_to_kv_half(wsq), wsk, wsv, ga],
                               axis=1).astype(BF16)
    wgu = jnp.pad(w_gate_up[0], ((0, LANES - GLA_GATE_RANK), (0, 0)))
    bg = b_gate[0][None, :]
    g_a = g_attn[0][None, :]
    g_f = g_ffn[0][None, :]
    g_n = g_gla_out[0][None, :]
    wo_gla = w_out[0, :GLA_WIDTH].astype(BF16)
    wo_swa = w_out[0, GLA_WIDTH:]
    wo_swa_pad = _pad_heads_to_kv_half(wo_swa.T).T.astype(BF16)
    wo_swa = wo_swa.astype(BF16)
    wq = w_peer_q[0].astype(BF16)
    sk0 = jnp.pad(peer_sub_keys[0, :, 0], ((0, 0), (0, 0), (0, PEER_HALF)))
    sk1 = jnp.pad(peer_sub_keys[0, :, 1], ((0, 0), (0, 0), (PEER_HALF, 0)))
    keys = jnp.concatenate([sk0, sk1], axis=1).astype(BF16)
    u = peer_u[0].astype(BF16)
    vt = peer_v[0].T.astype(BF16)
    sinks = attn_sinks[0]
    sink_col = jnp.tile(sinks, dseq)[:, None]

    cos_p, sin_p = _rope_tables(jnp.arange(seq))
    cos_s, sin_s = _rope_tables(PAST_LEN + jnp.arange(dseq))
    cos_s = jnp.tile(cos_s, (PROJ_ROWS // dseq, 1))
    sin_s = jnp.tile(sin_s, (PROJ_ROWS // dseq, 1))

    def ffn(h, hn, st):
        aux = _topk(st)
        return _final(h, _peer(st, aux, hn, u, vt), g_final[None, :])

    tp = bsz * seq
    q, k, v, r, la, sq, sk, sv = _proj(x_prompt.reshape(tp, d), g_a, w_prompt, wgu, bg, cos_p, sin_p,
                                       SWA_WIDTH)
    b3 = lambda a: a.reshape(bsz, seq, a.shape[-1])
    o_gla, s_prompt = _gla_prompt(b3(q), b3(k), b3(v), b3(la))
    o_swa = _swa_prompt(b3(sq), b3(sk), b3(sv), sinks)
    h, hn, st = _merge(o_gla.reshape(tp, GLA_WIDTH), r, o_swa.reshape(tp, SWA_WIDTH),
                       x_prompt.reshape(tp, d), g_n, wo_gla, wo_swa, g_f, wq, keys)
    y_prompt = ffn(h, hn, st).reshape(bsz, seq, d)
    kv_shape = (1, bsz, WINDOW, SWA_KV_HEADS, SWA_HEAD_DIM)
    k_prompt = b3(sk)[:, seq - WINDOW:].reshape(kv_shape)
    v_prompt = b3(sv)[:, seq - WINDOW:].reshape(kv_shape)

    ts = dbsz * dseq
    q, k, v, r, la, sq, sk, sv = _proj(x_sample.reshape(ts, d), g_a, w_sample, wgu, bg, cos_s, sin_s,
                                       SWA_HEADS * LANES)
    d3 = lambda a: a.reshape(dbsz, dseq, a.shape[-1])
    o_gla, s_sample = _gla_sample(d3(q), d3(k), d3(v), d3(la), state_gla[0])
    o_swa, k_sample, v_sample = _swa_sample(
        sq.reshape(dbsz, dseq * SWA_HEADS, LANES),
        cache_win_k[0].reshape(dbsz, win, LANES), cache_win_v[0].reshape(dbsz, win, LANES),
        d3(sk), d3(sv), sink_col)
    h, hn, st = _merge(o_gla.reshape(ts, GLA_WIDTH), r, o_swa.reshape(ts, SWA_HEADS * LANES),
                       x_sample.reshape(ts, d), g_n, wo_gla, wo_swa_pad, g_f, wq, keys)
    y_sample = ffn(h, hn, st).reshape(dbsz, dseq, d)
    ckv_shape = (1, dbsz, win, SWA_KV_HEADS, SWA_HEAD_DIM)

    return (y_prompt, y_sample, s_prompt[None], k_prompt, v_prompt, s_sample[None],
            k_sample.reshape(ckv_shape), v_sample.reshape(ckv_shape))
```

```python
import functools
import math

import jax
import jax.numpy as jnp
from jax import lax
from jax.experimental import pallas as pl
from jax.experimental.pallas import tpu as pltpu

F32 = jnp.float32
BF16 = jnp.bfloat16
HIGHEST = lax.Precision.HIGHEST

D_MODEL = 2048
GLA_HEADS = 4
GLA_DK = 128
GLA_DV = 256
GLA_WIDTH = GLA_HEADS * GLA_DV
GLA_GATE_RANK = 16
GLA_TAU = 16.0
GLA_CHUNK = 64
SWA_HEAD_DIM = 64
SWA_HEADS = 16
SWA_KV_HEADS = 2
SWA_GROUP = SWA_HEADS // SWA_KV_HEADS
SWA_WIDTH = SWA_HEADS * SWA_HEAD_DIM
WINDOW = 128
ROPE_THETA = 10000.0
PAST_LEN = 8192
PEER_N_KEYS = 128
PEER_EXPERTS = PEER_N_KEYS * PEER_N_KEYS
PEER_HEADS = 8
PEER_QDIM = 128
PEER_HALF = PEER_QDIM // 2
PEER_TOPK = 16
EPS = 1e-6
NEG_INF = -1e30

LANES = 128
BF16_SUBLANES = 16
PROJ_ROWS = 256
TOPK_TOKENS = 128
PEER_TOKENS = 512
PEER_CHUNK = 1024
PEER_PARTS = 2
SAMPLE_BATCH_BLOCK = 8
VMEM_LIMIT = 60 * 1024 * 1024

NT_DIMS = (((1,), (1,)), ((), ()))


def _resident(shape):
    nd = len(shape)
    return pl.BlockSpec(shape, lambda *_: (0,) * nd, pipeline_mode=pl.Buffered(1))


def _params(n_axes):
    return pltpu.CompilerParams(dimension_semantics=("arbitrary",) * n_axes,
                                vmem_limit_bytes=VMEM_LIMIT)


def _rms(x, g):
    return x * lax.rsqrt(jnp.mean(x * x, axis=-1, keepdims=True) + EPS) * g


def _proj_body(wsq, x_ref, g_ref, w_ref, wgu_ref, bg_ref, cos_ref, sin_ref,
               q_ref, k_ref, v_ref, r_ref, la_ref, sq_ref, sk_ref, sv_ref):
    xn = _rms(x_ref[...], g_ref[...]).astype(BF16)

    def proj(start, width):
        return jnp.dot(xn, w_ref[:, start:start + width], preferred_element_type=F32)

    nk = GLA_HEADS * GLA_DK
    q_ref[...] = proj(0, nk) * (GLA_DK ** -0.5)
    k_ref[...] = proj(nk, nk)
    v_ref[...] = proj(2 * nk, GLA_WIDTH)
    r_ref[...] = proj(2 * nk + GLA_WIDTH, GLA_WIDTH)
    o_sq = 2 * nk + 2 * GLA_WIDTH
    o_sk = o_sq + wsq
    o_sv = o_sk + LANES
    o_ga = o_sv + LANES

    ga = proj(o_ga, LANES)
    z = jnp.dot(ga, wgu_ref[...], precision=HIGHEST, preferred_element_type=F32) + bg_ref[...]
    la_ref[...] = (jnp.minimum(z, 0.0) - jnp.log1p(jnp.exp(-jnp.abs(z)))) * (1.0 / GLA_TAU)

    cos = cos_ref[...]
    sin = sin_ref[...]
    lane = lax.broadcasted_iota(jnp.int32, cos.shape, 1)
    first_half = (lane & (SWA_HEAD_DIM // 2)) == 0

    def rope(t):
        rot = jnp.where(first_half, pltpu.roll(t, LANES - SWA_HEAD_DIM // 2, 1),
                        pltpu.roll(t, SWA_HEAD_DIM // 2, 1))
        return t * cos + rot * sin

    for c in range(wsq // LANES):
        sq_ref[:, c * LANES:(c + 1) * LANES] = rope(proj(o_sq + c * LANES, LANES))
    sk_ref[...] = rope(proj(o_sk, LANES))
    sv_ref[...] = proj(o_sv, LANES)


def _proj(x, g, w, wgu, bg, cos, sin, wsq):
    t, d = x.shape
    tm = PROJ_ROWS
    n_tab = cos.shape[0] // tm
    row = lambda i: (i, 0)
    tab = lambda i: (i % n_tab, 0)
    widths = (512, 512, 1024, 1024, 512, wsq, LANES, LANES)
    return pl.pallas_call(
        functools.partial(_proj_body, wsq),
        grid=(t // tm,),
        in_specs=[pl.BlockSpec((tm, d), row), _resident(g.shape), _resident(w.shape),
                  _resident(wgu.shape), _resident(bg.shape),
                  pl.BlockSpec((tm, LANES), tab), pl.BlockSpec((tm, LANES), tab)],
        out_specs=[pl.BlockSpec((tm, n), row) for n in widths],
        out_shape=[jax.ShapeDtypeStruct((t, n), F32) for n in widths],
        compiler_params=_params(1),
        name="proj",
    )(x, g, w, wgu, bg, cos, sin)


def _gla_prompt_body(q_ref, k_ref, v_ref, la_ref, o_ref, sfin_ref, s_scr):
    n = pl.program_id(0)
    bsz = q_ref.shape[0]

    @pl.when(n == 0)
    def _():
        s_scr[...] = jnp.zeros_like(s_scr)

    c = GLA_CHUNK
    row = lax.broadcasted_iota(jnp.int32, (c, c), 0)
    col = lax.broadcasted_iota(jnp.int32, (c, c), 1)
    causal = row >= col
    tri = causal.astype(F32)
    for ch in range(q_ref.shape[1] // c):
        rows = slice(ch * c, (ch + 1) * c)
        for bi in range(bsz):
            b_all = jnp.dot(tri, la_ref[bi, rows, :], precision=HIGHEST, preferred_element_type=F32)
            for h in range(GLA_HEADS):
                kl = slice(h * GLA_DK, (h + 1) * GLA_DK)
                vl = slice(h * GLA_DV, (h + 1) * GLA_DV)
                b = b_all[:, kl]
                b_last = b[c - 1:c, :]
                k = k_ref[bi, rows, kl]
                q_t = (q_ref[bi, rows, kl] * jnp.exp(b)).astype(BF16)
                k_t = (k * jnp.exp(-b)).astype(BF16)
                k_e = k * jnp.exp(b_last - b)
                vb = v_ref[bi, rows, vl].astype(BF16)
                a = lax.dot_general(q_t, k_t, NT_DIMS, preferred_element_type=F32)
                a = jnp.where(causal, a, 0.0).astype(BF16)
                s = s_scr[bi * GLA_HEADS + h]
                o_ref[bi, rows, vl] = (jnp.dot(a, vb, preferred_element_type=F32)
                                       + jnp.dot(q_t, s.astype(BF16), preferred_element_type=F32))
                decay = jnp.exp(jnp.broadcast_to(b_last, (GLA_DK, GLA_DK))).T
                decay = jnp.concatenate([decay, decay], axis=1)
                s_scr[bi * GLA_HEADS + h] = decay * s + jnp.dot(k_e.T.astype(BF16), vb,
                                                                preferred_element_type=F32)

    @pl.when(n == pl.num_programs(0) - 1)
    def _():
        sfin_ref[...] = s_scr[...].reshape(sfin_ref.shape)


GLA_CHUNKS_PER_STEP = 2


def _gla_prompt(q, k, v, la):
    bsz, s, _ = q.shape
    rows = GLA_CHUNK * GLA_CHUNKS_PER_STEP
    qk_spec = pl.BlockSpec((bsz, rows, GLA_HEADS * GLA_DK), lambda n: (0, n, 0))
    v_spec = pl.BlockSpec((bsz, rows, GLA_WIDTH), lambda n: (0, n, 0))
    return pl.pallas_call(
        _gla_prompt_body,
        grid=(s // rows,),
        in_specs=[qk_spec, qk_spec, v_spec, qk_spec],
        out_specs=[v_spec, pl.BlockSpec((bsz, GLA_HEADS, GLA_DK, GLA_DV), lambda n: (0, 0, 0, 0))],
        out_shape=[jax.ShapeDtypeStruct((bsz, s, GLA_WIDTH), F32),
                   jax.ShapeDtypeStruct((bsz, GLA_HEADS, GLA_DK, GLA_DV), F32)],
        scratch_shapes=[pltpu.VMEM((bsz * GLA_HEADS, GLA_DK, GLA_DV), F32)],
        compiler_params=_params(1),
        name="gla_prompt",
    )(q, k, v, la)


def _gla_sample_body(q_ref, k_ref, v_ref, la_ref, s0_ref, o_ref, sout_ref):
    bb, t, _ = q_ref.shape
    rows = bb * t
    q = q_ref[...].reshape(rows, GLA_DK)
    k = k_ref[...].reshape(rows, GLA_DK)
    la = la_ref[...].reshape(rows, GLA_DK)
    v = v_ref[...].reshape(rows, GLA_DV)
    row = lax.broadcasted_iota(jnp.int32, (rows, rows), 0)
    col = lax.broadcasted_iota(jnp.int32, (rows, rows), 1)
    same_seq = (row // t) == (col // t)
    causal = same_seq & (row >= col)
    b = jnp.dot(causal.astype(F32), la, precision=HIGHEST, preferred_element_type=F32)
    b_tot = jnp.dot(same_seq.astype(F32), la, precision=HIGHEST, preferred_element_type=F32)
    q_t = (q * jnp.exp(b)).astype(BF16)
    k_t = (k * jnp.exp(-b)).astype(BF16)
    k_e_t = (k * jnp.exp(b_tot - b)).T.astype(BF16)
    la_t = la.T
    a = lax.dot_general(q_t, k_t, NT_DIMS, preferred_element_type=F32)
    a = jnp.where(causal, a, 0.0).astype(BF16)
    o = jnp.dot(a, v.astype(BF16), preferred_element_type=F32)
    seq_of_row_v = lax.broadcasted_iota(jnp.int32, (rows, GLA_DV), 0) // t
    seq_of_row_k = lax.broadcasted_iota(jnp.int32, (rows, GLA_DK), 0) // t
    for i in range(bb):
        s0 = s0_ref[i]
        mine = seq_of_row_v == i
        o = o + jnp.where(mine, jnp.dot(q_t, s0.astype(BF16), preferred_element_type=F32), 0.0)
        v_i = jnp.where(mine, v, 0.0).astype(BF16)
        ones_i = jnp.where(seq_of_row_k == i, 1.0, 0.0)
        decay = jnp.exp(jnp.dot(la_t, ones_i, precision=HIGHEST, preferred_element_type=F32))
        decay = jnp.concatenate([decay, decay], axis=1)
        sout_ref[i] = decay * s0 + jnp.dot(k_e_t, v_i, preferred_element_type=F32)
    o_ref[...] = o.reshape(bb, t, GLA_DV)


def _gla_sample(q, k, v, la, s0):
    bsz, t, _ = q.shape
    bb = SAMPLE_BATCH_BLOCK
    qk_spec = pl.BlockSpec((bb, t, GLA_DK), lambda i, h: (i, 0, h))
    v_spec = pl.BlockSpec((bb, t, GLA_DV), lambda i, h: (i, 0, h))
    s_spec = pl.BlockSpec((bb, None, GLA_DK, GLA_DV), lambda i, h: (i, h, 0, 0))
    return pl.pallas_call(
        _gla_sample_body,
        grid=(bsz // bb, GLA_HEADS),
        in_specs=[qk_spec, qk_spec, v_spec, qk_spec, s_spec],
        out_specs=[v_spec, s_spec],
        out_shape=[jax.ShapeDtypeStruct((bsz, t, GLA_WIDTH), F32),
                   jax.ShapeDtypeStruct(s0.shape, F32)],
        compiler_params=_params(2),
        name="gla_sample",
    )(q, k, v, la, s0)


def _sink_softmax(s, mask, sink):
    s = jnp.where(mask, s, NEG_INF)
    m = jnp.maximum(jnp.max(s, axis=-1, keepdims=True), sink)
    p = jnp.exp(s - m)
    return p / (jnp.sum(p, axis=-1, keepdims=True) + jnp.exp(sink - m))


def _split_kv_lanes(x):
    lane = lax.broadcasted_iota(jnp.int32, x.shape, 1)
    low = lane < SWA_HEAD_DIM
    lo0 = jnp.where(low, x, 0.0)
    hi1 = jnp.where(low, 0.0, x)
    hi0 = pltpu.roll(lo0, SWA_HEAD_DIM, 1)
    lo1 = pltpu.roll(hi1, SWA_HEAD_DIM, 1)
    return (lo0.astype(BF16), lo1.astype(BF16)), (hi0.astype(BF16), hi1.astype(BF16))


def _swa_prompt_body(sink_ref, q_ref, kc_ref, kp_ref, vc_ref, vp_ref, o_ref):
    n = pl.program_id(1)
    blk = WINDOW
    k_parts = _split_kv_lanes(jnp.concatenate([kp_ref[...], kc_ref[...]], axis=0))
    v_parts = _split_kv_lanes(jnp.concatenate([vp_ref[...], vc_ref[...]], axis=0))
    qi = lax.broadcasted_iota(jnp.int32, (blk, 2 * blk), 0)
    kj = lax.broadcasted_iota(jnp.int32, (blk, 2 * blk), 1)
    diff = qi + blk - kj
    mask = (diff >= 0) & (diff < WINDOW) & ((n > 0) | (kj >= blk))
    scale = SWA_HEAD_DIM ** -0.5
    for pair in range(SWA_HEADS // 2):
        kv = (2 * pair) // SWA_GROUP
        qp = q_ref[:, pair * LANES:(pair + 1) * LANES].astype(BF16)
        acc = None
        for half in range(2):
            s = lax.dot_general(qp, k_parts[half][kv], NT_DIMS, preferred_element_type=F32) * scale
            p = _sink_softmax(s, mask, sink_ref[2 * pair + half])
            o = jnp.dot(p.astype(BF16), v_parts[half][kv], preferred_element_type=F32)
            acc = o if acc is None else acc + o
        o_ref[:, pair * LANES:(pair + 1) * LANES] = acc


def _swa_prompt(sq, sk, sv, sinks):
    bsz, s, _ = sq.shape
    blk = WINDOW
    cur = lambda b, n: (b, n, 0)
    prev = lambda b, n: (b, jnp.maximum(n - 1, 0), 0)
    kv_cur = pl.BlockSpec((None, blk, LANES), cur)
    kv_prev = pl.BlockSpec((None, blk, LANES), prev)
    q_spec = pl.BlockSpec((None, blk, SWA_WIDTH), cur)
    return pl.pallas_call(
        _swa_prompt_body,
        grid=(bsz, s // blk),
        in_specs=[pl.BlockSpec(memory_space=pltpu.SMEM), q_spec, kv_cur, kv_prev, kv_cur, kv_prev],
        out_specs=q_spec,
        out_shape=jax.ShapeDtypeStruct((bsz, s, SWA_WIDTH), F32),
        compiler_params=_params(2),
        name="swa_prompt",
    )(sinks, sq, sk, sk, sv, sv)


def _swa_sample_body(t_new, sink_ref, q_ref, ck_ref, cv_ref, kn_ref, vn_ref, o_ref, nk_ref, nv_ref):
    bb, rows, _ = q_ref.shape
    w = ck_ref.shape[1]
    pad = jnp.zeros((16 - t_new, LANES), F32)
    n_keys = w + 16
    tq = lax.broadcasted_iota(jnp.int32, (rows, n_keys), 0) // SWA_HEADS
    c = lax.broadcasted_iota(jnp.int32, (rows, n_keys), 1)
    mask = ((c < w) & (c > tq)) | ((c >= w) & (c - w <= tq) & (c < w + t_new))
    sink = sink_ref[...]
    scale = SWA_HEAD_DIM ** -0.5
    for i in range(bb):
        ck, cv, kn, vn = ck_ref[i], cv_ref[i], kn_ref[i], vn_ref[i]
        kcat = jnp.concatenate([ck, kn, pad], axis=0).astype(BF16)
        vcat = jnp.concatenate([cv, vn, pad], axis=0).astype(BF16)
        s = lax.dot_general(q_ref[i].astype(BF16), kcat, NT_DIMS, preferred_element_type=F32) * scale
        p = _sink_softmax(s, mask, sink)
        o_ref[i] = jnp.dot(p.astype(BF16), vcat, preferred_element_type=F32)
        nk_ref[i, 0:w - t_new, :] = ck[t_new:, :]
        nk_ref[i, w - t_new:w, :] = kn
        nv_ref[i, 0:w - t_new, :] = cv[t_new:, :]
        nv_ref[i, w - t_new:w, :] = vn


def _swa_sample(q_pad, ck, cv, kn, vn, sink_col):
    bsz, rows, _ = q_pad.shape
    t_new = kn.shape[1]
    w = ck.shape[1]
    bb = SAMPLE_BATCH_BLOCK
    blk = lambda r: pl.BlockSpec((bb, r, LANES), lambda i: (i, 0, 0))
    return pl.pallas_call(
        functools.partial(_swa_sample_body, t_new),
        grid=(bsz // bb,),
        in_specs=[_resident(sink_col.shape), blk(rows), blk(w), blk(w), blk(t_new), blk(t_new)],
        out_specs=[blk(rows), blk(w), blk(w)],
        out_shape=[jax.ShapeDtypeStruct((bsz, rows, LANES), F32),
                   jax.ShapeDtypeStruct(ck.shape, F32), jax.ShapeDtypeStruct(cv.shape, F32)],
        compiler_params=_params(1),
        name="swa_sample",
    )(sink_col, q_pad, ck, cv, kn, vn)


def _merge_body(og_ref, r_ref, os_ref, x_ref, gn_ref, wog_ref, wos_ref, gf_ref, wq_ref, keys_ref,
                h_ref, hn_ref, st_ref):
    og = og_ref[...]
    gn = gn_ref[...]
    heads = [_rms(og[:, i * GLA_DV:(i + 1) * GLA_DV], gn) for i in range(GLA_HEADS)]
    r = r_ref[...]
    gated = jnp.concatenate(heads, axis=1) * (r / (1.0 + jnp.exp(-r)))
    h = (x_ref[...]
         + jnp.dot(gated.astype(BF16), wog_ref[...], preferred_element_type=F32)
         + jnp.dot(os_ref[...].astype(BF16), wos_ref[...], preferred_element_type=F32))
    h_ref[...] = h
    hn_f32 = _rms(h, gf_ref[...])
    hn = hn_f32.astype(BF16)
    hn_ref[...] = hn_f32.T.astype(BF16)
    pq = jnp.dot(hn, wq_ref[...], preferred_element_type=F32).astype(BF16)
    for i in range(PEER_HEADS):
        st_ref[i] = lax.dot_general(keys_ref[i], pq[:, i * PEER_QDIM:(i + 1) * PEER_QDIM], NT_DIMS,
                                    preferred_element_type=F32)


def _merge(o_gla, r, o_swa, x, gn, wog, wos, gf, wq, keys):
    t, d = x.shape
    tm = PROJ_ROWS
    row = lambda i: (i, 0)
    return pl.pallas_call(
        _merge_body,
        grid=(t // tm,),
        in_specs=[pl.BlockSpec((tm, GLA_WIDTH), row), pl.BlockSpec((tm, GLA_WIDTH), row),
                  pl.BlockSpec((tm, o_swa.shape[1]), row), pl.BlockSpec((tm, d), row),
                  _resident(gn.shape), _resident(wog.shape), _resident(wos.shape),
                  _resident(gf.shape), _resident(wq.shape), _resident(keys.shape)],
        out_specs=[pl.BlockSpec((tm, d), row), pl.BlockSpec((d, tm), lambda i: (0, i)),
                   pl.BlockSpec((PEER_HEADS, 2 * PEER_N_KEYS, tm), lambda i: (0, 0, i))],
        out_shape=[jax.ShapeDtypeStruct((t, d), F32), jax.ShapeDtypeStruct((d, t), BF16),
                   jax.ShapeDtypeStruct((PEER_HEADS, 2 * PEER_N_KEYS, t), F32)],
        compiler_params=_params(1),
        name="merge",
    )(o_gla, r, o_swa, x, gn, wog, wos, gf, wq, keys)


N_SORTED = PEER_TOPK + 1


def _topk_body(st_ref, aux_ref, top0_scr, top1_scr):
    scr = (top0_scr, top1_scr)
    for h in range(PEER_HEADS):
        for half in range(2):
            arr = st_ref[h, half * PEER_N_KEYS:(half + 1) * PEER_N_KEYS, :]
            for a in range(N_SORTED):
                m = jnp.max(arr, axis=0, keepdims=True)
                scr[half][a, h:h + 1, :] = m
                if a + 1 < N_SORTED:
                    arr = jnp.where(arr == m, NEG_INF, arr)
    top0 = [top0_scr[a] for a in range(N_SORTED)]
    top1 = [top1_scr[a] for a in range(N_SORTED)]
    cands = [top0[a - 1] + top1[b - 1]
             for a in range(1, N_SORTED + 1) for b in range(1, N_SORTED // a + 1)]
    best = None
    z = None
    for rank in range(N_SORTED):
        m = functools.reduce(jnp.maximum, cands)
        if rank == 0:
            best = m
            z = jnp.ones_like(m)
        elif rank < PEER_TOPK:
            z = z + jnp.exp(m - best)
        if rank == PEER_TOPK - 1:
            kth = m
        if rank + 1 < N_SORTED:
            cands = [jnp.where(cnd == m, NEG_INF, cnd) for cnd in cands]
    aux_ref[0] = 0.5 * (kth + m)
    aux_ref[1] = top0[0]
    aux_ref[2] = top1[0]
    aux_ref[3] = 1.0 / z


def _topk(st):
    t = st.shape[2]
    tt = TOPK_TOKENS
    return pl.pallas_call(
        _topk_body,
        grid=(t // tt,),
        in_specs=[pl.BlockSpec((PEER_HEADS, 2 * PEER_N_KEYS, tt), lambda i: (0, 0, i))],
        out_specs=pl.BlockSpec((4, PEER_HEADS, tt), lambda i: (0, 0, i)),
        out_shape=jax.ShapeDtypeStruct((4, PEER_HEADS, t), F32),
        scratch_shapes=[pltpu.VMEM((N_SORTED, PEER_HEADS, tt), F32),
                        pltpu.VMEM((N_SORTED, PEER_HEADS, tt), F32)],
        compiler_params=_params(1),
        name="peer_topk",
    )(st)


def _peer_body(st_ref, aux_ref, hnt_ref, u_ref, vt_ref, out_ref,
               thr_scr, w0_scr, e1_scr, thr_cur, w0_cur, acc_scr, *piece_scr):
    ec = pl.program_id(1)
    n_rows = PEER_CHUNK // PEER_N_KEYS
    tb = hnt_ref.shape[1]
    a_scr = piece_scr[:PEER_PARTS]
    c_scr = piece_scr[PEER_PARTS:]

    @pl.when(ec == 0)
    def _():
        grid_rows = (PEER_N_KEYS // n_rows, n_rows, tb)
        for h in range(PEER_HEADS):
            s0 = st_ref[h, 0:PEER_N_KEYS, :]
            s1 = st_ref[h, PEER_N_KEYS:2 * PEER_N_KEYS, :]
            thr_scr[h] = jnp.exp(aux_ref[0, h:h + 1, :] - s0 - aux_ref[2, h:h + 1, :]).reshape(grid_rows)
            w0_scr[h] = (jnp.exp(s0 - aux_ref[1, h:h + 1, :]) * aux_ref[3, h:h + 1, :]).reshape(grid_rows)
            e1_scr[h] = jnp.exp(s1 - aux_ref[2, h:h + 1, :]).astype(BF16)
        acc_scr[...] = jnp.zeros_like(acc_scr)

    sub = BF16_SUBLANES
    inv_sqrt2 = 1.0 / math.sqrt(2.0)
    e_part = PEER_CHUNK // PEER_PARTS
    rows_per_part = n_rows // PEER_PARTS

    z = lax.shift_right_logical(ec, 30)

    for h in range(PEER_HEADS):
        for ii in range(n_rows):
            thr_cur[h, ii] = jnp.broadcast_to(thr_scr[h, ec, ii:ii + 1, :], (sub, tb)).astype(BF16)
            w0_cur[h, ii] = jnp.broadcast_to(w0_scr[h, ec, ii:ii + 1, :], (sub, tb)).astype(BF16)

    def scores(p):
        a_scr[p][z] = jnp.dot(u_ref[p * e_part:(p + 1) * e_part, :], hnt_ref[...],
                              preferred_element_type=F32)

    def coefficients(p):
        for r in range(rows_per_part):
            ii = p * rows_per_part + r
            for lt in range(tb // LANES):
                ls = slice(lt * LANES, (lt + 1) * LANES)
                for jb in range(PEER_N_KEYS // sub):
                    rows = slice(r * PEER_N_KEYS + jb * sub, r * PEER_N_KEYS + (jb + 1) * sub)
                    a = a_scr[p][z, rows, ls]
                    act = (0.5 * a * (1.0 + lax.erf(a * inv_sqrt2))).astype(BF16)
                    g = None
                    for h in range(PEER_HEADS):
                        e1 = e1_scr[h, jb * sub:(jb + 1) * sub, ls]
                        term = jnp.where(e1 >= thr_cur[h, ii, :, ls], e1, jnp.zeros_like(e1)) * w0_cur[h, ii, :, ls]
                        g = term if g is None else g + term
                    c_scr[p][z, rows, ls] = g * act

    def values(p):
        acc_scr[z] += jnp.dot(vt_ref[p], c_scr[p][z], preferred_element_type=F32)

    for p in range(PEER_PARTS):
        scores(p)
    for p in range(PEER_PARTS):
        coefficients(p)
        values(p)

    @pl.when(ec == pl.num_programs(1) - 1)
    def _():
        out_ref[...] = acc_scr[0].T


def _peer(st, aux, hnt, u, vt):
    d, t = hnt.shape
    tb = PEER_TOKENS
    ec = PEER_CHUNK
    n_chunks = u.shape[0] // ec
    rows = ec // PEER_N_KEYS
    return pl.pallas_call(
        _peer_body,
        grid=(t // tb, n_chunks),
        in_specs=[pl.BlockSpec((PEER_HEADS, 2 * PEER_N_KEYS, tb), lambda i, j: (0, 0, i),
                               pipeline_mode=pl.Buffered(1)),
                  pl.BlockSpec((4, PEER_HEADS, tb), lambda i, j: (0, 0, i)),
                  pl.BlockSpec((d, tb), lambda i, j: (0, i), pipeline_mode=pl.Buffered(1)),
                  pl.BlockSpec((ec, d), lambda i, j: (j, 0)),
                  pl.BlockSpec((PEER_PARTS, d, ec // PEER_PARTS), lambda i, j: (j, 0, 0))],
        out_specs=pl.BlockSpec((tb, d), lambda i, j: (i, 0)),
        out_shape=jax.ShapeDtypeStruct((t, d), F32),
        scratch_shapes=([pltpu.VMEM((PEER_HEADS, n_chunks, rows, tb), F32),
                         pltpu.VMEM((PEER_HEADS, n_chunks, rows, tb), F32),
                         pltpu.VMEM((PEER_HEADS, PEER_N_KEYS, tb), BF16),
                         pltpu.VMEM((PEER_HEADS, rows, BF16_SUBLANES, tb), BF16),
                         pltpu.VMEM((PEER_HEADS, rows, BF16_SUBLANES, tb), BF16),
                         pltpu.VMEM((1, d, tb), F32)]
                        + [pltpu.VMEM((1, ec // PEER_PARTS, tb), F32)] * PEER_PARTS
                        + [pltpu.VMEM((1, ec // PEER_PARTS, tb), BF16)] * PEER_PARTS),
        compiler_params=_params(2),
        name="peer_dense",
    )(st, aux, hnt, u, vt)


def _final_body(h_ref, p_ref, g_ref, y_ref):
    y_ref[...] = _rms(h_ref[...] + p_ref[...], g_ref[...])


def _final(h, p, g):
    t, d = h.shape
    tm = PROJ_ROWS
    row = lambda i: (i, 0)
    return pl.pallas_call(
        _final_body,
        grid=(t // tm,),
        in_specs=[pl.BlockSpec((tm, d), row), pl.BlockSpec((tm, d), row), _resident(g.shape)],
        out_specs=pl.BlockSpec((tm, d), row),
        out_shape=jax.ShapeDtypeStruct((t, d), F32),
        compiler_params=_params(1),
        name="final",
    )(h, p, g)


def _rope_tables(pos):
    half = SWA_HEAD_DIM // 2
    inv = jnp.exp(-math.log(ROPE_THETA) * jnp.arange(half, dtype=F32) * (2.0 / SWA_HEAD_DIM))
    ang = pos.astype(F32)[:, None] * inv[None, :]
    cos, sin = jnp.cos(ang), jnp.sin(ang)
    reps = LANES // SWA_HEAD_DIM
    return (jnp.tile(jnp.concatenate([cos, cos], axis=1), (1, reps)),
            jnp.tile(jnp.concatenate([-sin, sin], axis=1), (1, reps)))


def _split_w_in(w_in):
    out, start = [], 0
    for width in (512, 512, 1024, 1024, GLA_GATE_RANK, SWA_WIDTH, LANES, LANES):
        out.append(w_in[:, start:start + width])
        start += width
    return out


def _pad_heads_to_kv_half(w):
    lead = w.shape[:-1]
    w = w.reshape(lead + (SWA_KV_HEADS, SWA_GROUP, 1, SWA_HEAD_DIM))
    sel = jnp.eye(SWA_KV_HEADS, dtype=w.dtype).reshape(SWA_KV_HEADS, 1, SWA_KV_HEADS, 1)
    return (w * sel).reshape(lead + (SWA_HEADS * LANES,))


def kernel(x_prompt, x_sample, state_gla, cache_win_k, cache_win_v, g_attn, w_in, w_gate_up, b_gate,
           g_gla_out, attn_sinks, w_out, g_ffn, w_peer_q, peer_sub_keys, peer_u, peer_v, g_final):
    bsz, seq, d = x_prompt.shape
    dbsz, dseq, _ = x_sample.shape
    win = cache_win_k.shape[2]

    gq, gk, gv, gr, ga, wsq, wsk, wsv = _split_w_in(w_in[0])
    ga = jnp.pad(ga, ((0, 0), (0, LANES - GLA_GATE_RANK)))
    w_prompt = jnp.concatenate([gq, gk, gv, gr, wsq, wsk, wsv, ga], axis=1).astype(BF16)
    w_sample = jnp.concatenate([gq, gk, gv, gr, _pad_heads_to_kv_half(wsq), wsk, wsv, ga],
                               axis=1).astype(BF16)
    wgu = jnp.pad(w_gate_up[0], ((0, LANES - GLA_GATE_RANK), (0, 0)))
    bg = b_gate[0][None, :]
    g_a = g_attn[0][None, :]
    g_f = g_ffn[0][None, :]
    g_n = g_gla_out[0][None, :]
    wo_gla = w_out[0, :GLA_WIDTH].astype(BF16)
    wo_swa = w_out[0, GLA_WIDTH:]
    wo_swa_pad = _pad_heads_to_kv_half(wo_swa.T).T.astype(BF16)
    wo_swa = wo_swa.astype(BF16)
    wq = w_peer_q[0].astype(BF16)
    sk0 = jnp.pad(peer_sub_keys[0, :, 0], ((0, 0), (0, 0), (0, PEER_HALF)))
    sk1 = jnp.pad(peer_sub_keys[0, :, 1], ((0, 0), (0, 0), (PEER_HALF, 0)))
    keys = jnp.concatenate([sk0, sk1], axis=1).astype(BF16)
    u = peer_u[0].astype(BF16)
    e_piece = PEER_CHUNK // PEER_PARTS
    vt = peer_v[0].reshape(PEER_EXPERTS // e_piece, e_piece, d).transpose(0, 2, 1).astype(BF16)
    sinks = attn_sinks[0]
    sink_col = jnp.tile(sinks, dseq)[:, None]

    cos_p, sin_p = _rope_tables(jnp.arange(seq))
    cos_s, sin_s = _rope_tables(PAST_LEN + jnp.arange(dseq))
    cos_s = jnp.tile(cos_s, (PROJ_ROWS // dseq, 1))
    sin_s = jnp.tile(sin_s, (PROJ_ROWS // dseq, 1))

    def ffn(h, hn, st):
        aux = _topk(st)
        return _final(h, _peer(st, aux, hn, u, vt), g_final[None, :])

    tp = bsz * seq
    q, k, v, r, la, sq, sk, sv = _proj(x_prompt.reshape(tp, d), g_a, w_prompt, wgu, bg, cos_p, sin_p,
                                       SWA_WIDTH)
    b3 = lambda a: a.reshape(bsz, seq, a.shape[-1])
    o_gla, s_prompt = _gla_prompt(b3(q), b3(k), b3(v), b3(la))
    o_swa = _swa_prompt(b3(sq), b3(sk), b3(sv), sinks)
    h, hn, st = _merge(o_gla.reshape(tp, GLA_WIDTH), r, o_swa.reshape(tp, SWA_WIDTH),
                       x_prompt.reshape(tp, d), g_n, wo_gla, wo_swa, g_f, wq, keys)
    y_prompt = ffn(h, hn, st).reshape(bsz, seq, d)
    kv_shape = (1, bsz, WINDOW, SWA_KV_HEADS, SWA_HEAD_DIM)
    k_prompt = b3(sk)[:, seq - WINDOW:].reshape(kv_shape)
    v_prompt = b3(sv)[:, seq - WINDOW:].reshape(kv_shape)

    ts = dbsz * dseq
    q, k, v, r, la, sq, sk, sv = _proj(x_sample.reshape(ts, d), g_a, w_sample, wgu, bg, cos_s, sin_s,
                                       SWA_HEADS * LANES)
    d3 = lambda a: a.reshape(dbsz, dseq, a.shape[-1])
    o_gla, s_sample = _gla_sample(d3(q), d3(k), d3(v), d3(la), state_gla[0])
    o_swa, k_sample, v_sample = _swa_sample(
        sq.reshape(dbsz, dseq * SWA_HEADS, LANES),
        cache_win_k[0].reshape(dbsz, win, LANES), cache_win_v[0].reshape(dbsz, win, LANES),
        d3(sk), d3(sv), sink_col)
    h, hn, st = _merge(o_gla.reshape(ts, GLA_WIDTH), r, o_swa.reshape(ts, SWA_HEADS * LANES),
                       x_sample.reshape(ts, d), g_n, wo_gla, wo_swa_pad, g_f, wq, keys)
    y_sample = ffn(h, hn, st).reshape(dbsz, dseq, d)
    ckv_shape = (1, dbsz, win, SWA_KV_HEADS, SWA_HEAD_DIM)

    return (y_prompt, y_sample, s_prompt[None], k_prompt, v_prompt, s_sample[None],
            k_sample.reshape(ckv_shape), v_sample.reshape(ckv_shape))
```

```python
import functools
import math

import jax
import jax.numpy as jnp
from jax import lax
from jax.experimental import pallas as pl
from jax.experimental.pallas import tpu as pltpu

F32 = jnp.float32
BF16 = jnp.bfloat16
HIGHEST = lax.Precision.HIGHEST

D_MODEL = 2048
GLA_HEADS = 4
GLA_DK = 128
GLA_DV = 256
GLA_WIDTH = GLA_HEADS * GLA_DV
GLA_GATE_RANK = 16
GLA_TAU = 16.0
GLA_CHUNK = 64
SWA_HEAD_DIM = 64
SWA_HEADS = 16
SWA_KV_HEADS = 2
SWA_GROUP = SWA_HEADS // SWA_KV_HEADS
SWA_WIDTH = SWA_HEADS * SWA_HEAD_DIM
WINDOW = 128
ROPE_THETA = 10000.0
PAST_LEN = 8192
PEER_N_KEYS = 128
PEER_EXPERTS = PEER_N_KEYS * PEER_N_KEYS
PEER_HEADS = 8
PEER_QDIM = 128
PEER_HALF = PEER_QDIM // 2
PEER_TOPK = 16
EPS = 1e-6
NEG_INF = -1e30

LANES = 128
BF16_SUBLANES = 16
PROJ_ROWS = 256
TOPK_TOKENS = 128
PEER_TOKENS = 512
PEER_CHUNK = 1024
PEER_PARTS = 2
SAMPLE_BATCH_BLOCK = 8
VMEM_LIMIT = 60 * 1024 * 1024

NT_DIMS = (((1,), (1,)), ((), ()))


def _resident(shape):
    nd = len(shape)
    return pl.BlockSpec(shape, lambda *_: (0,) * nd, pipeline_mode=pl.Buffered(1))


def _params(n_axes):
    return pltpu.CompilerParams(dimension_semantics=("arbitrary",) * n_axes,
                                vmem_limit_bytes=VMEM_LIMIT)


def _rms(x, g):
    return x * lax.rsqrt(jnp.mean(x * x, axis=-1, keepdims=True) + EPS) * g


def _proj_body(wsq, x_ref, g_ref, w_ref, wgu_ref, bg_ref, cos_ref, sin_ref,
               q_ref, k_ref, v_ref, r_ref, la_ref, sq_ref, sk_ref, sv_ref):
    xn = _rms(x_ref[...], g_ref[...]).astype(BF16)

    def proj(start, width):
        return jnp.dot(xn, w_ref[:, start:start + width], preferred_element_type=F32)

    nk = GLA_HEADS * GLA_DK
    q_ref[...] = proj(0, nk) * (GLA_DK ** -0.5)
    k_ref[...] = proj(nk, nk)
    v_ref[...] = proj(2 * nk, GLA_WIDTH)
    r_ref[...] = proj(2 * nk + GLA_WIDTH, GLA_WIDTH)
    o_sq = 2 * nk + 2 * GLA_WIDTH
    o_sk = o_sq + wsq
    o_sv = o_sk + LANES
    o_ga = o_sv + LANES

    ga = proj(o_ga, LANES)
    z = jnp.dot(ga, wgu_ref[...], precision=HIGHEST, preferred_element_type=F32) + bg_ref[...]
    la_ref[...] = (jnp.minimum(z, 0.0) - jnp.log1p(jnp.exp(-jnp.abs(z)))) * (1.0 / GLA_TAU)

    cos = cos_ref[...]
    sin = sin_ref[...]
    lane = lax.broadcasted_iota(jnp.int32, cos.shape, 1)
    first_half = (lane & (SWA_HEAD_DIM // 2)) == 0

    def rope(t):
        rot = jnp.where(first_half, pltpu.roll(t, LANES - SWA_HEAD_DIM // 2, 1),
                        pltpu.roll(t, SWA_HEAD_DIM // 2, 1))
        return t * cos + rot * sin

    for c in range(wsq // LANES):
        sq_ref[:, c * LANES:(c + 1) * LANES] = rope(proj(o_sq + c * LANES, LANES))
    sk_ref[...] = rope(proj(o_sk, LANES))
    sv_ref[...] = proj(o_sv, LANES)


def _proj(x, g, w, wgu, bg, cos, sin, wsq):
    t, d = x.shape
    tm = PROJ_ROWS
    n_tab = cos.shape[0] // tm
    row = lambda i: (i, 0)
    tab = lambda i: (i % n_tab, 0)
    widths = (512, 512, 1024, 1024, 512, wsq, LANES, LANES)
    return pl.pallas_call(
        functools.partial(_proj_body, wsq),
        grid=(t // tm,),
        in_specs=[pl.BlockSpec((tm, d), row), _resident(g.shape), _resident(w.shape),
                  _resident(wgu.shape), _resident(bg.shape),
                  pl.BlockSpec((tm, LANES), tab), pl.BlockSpec((tm, LANES), tab)],
        out_specs=[pl.BlockSpec((tm, n), row) for n in widths],
        out_shape=[jax.ShapeDtypeStruct((t, n), F32) for n in widths],
        compiler_params=_params(1),
        name="proj",
    )(x, g, w, wgu, bg, cos, sin)


def _gla_prompt_body(q_ref, k_ref, v_ref, la_ref, o_ref, sfin_ref, s_scr):
    n = pl.program_id(0)
    bsz = q_ref.shape[0]

    @pl.when(n == 0)
    def _():
        s_scr[...] = jnp.zeros_like(s_scr)

    c = GLA_CHUNK
    row = lax.broadcasted_iota(jnp.int32, (c, c), 0)
    col = lax.broadcasted_iota(jnp.int32, (c, c), 1)
    causal = row >= col
    tri = causal.astype(F32)
    for ch in range(q_ref.shape[1] // c):
        rows = slice(ch * c, (ch + 1) * c)
        for bi in range(bsz):
            b_all = jnp.dot(tri, la_ref[bi, rows, :], precision=HIGHEST, preferred_element_type=F32)
            for h in range(GLA_HEADS):
                kl = slice(h * GLA_DK, (h + 1) * GLA_DK)
                vl = slice(h * GLA_DV, (h + 1) * GLA_DV)
                b = b_all[:, kl]
                b_last = b[c - 1:c, :]
                k = k_ref[bi, rows, kl]
                q_t = (q_ref[bi, rows, kl] * jnp.exp(b)).astype(BF16)
                k_t = (k * jnp.exp(-b)).astype(BF16)
                k_e = k * jnp.exp(b_last - b)
                vb = v_ref[bi, rows, vl].astype(BF16)
                a = lax.dot_general(q_t, k_t, NT_DIMS, preferred_element_type=F32)
                a = jnp.where(causal, a, 0.0).astype(BF16)
                s = s_scr[bi * GLA_HEADS + h]
                o_ref[bi, rows, vl] = (jnp.dot(a, vb, preferred_element_type=F32)
                                       + jnp.dot(q_t, s.astype(BF16), preferred_element_type=F32))
                decay = jnp.exp(jnp.broadcast_to(b_last, (GLA_DK, GLA_DK))).T
                decay = jnp.concatenate([decay, decay], axis=1)
                s_scr[bi * GLA_HEADS + h] = decay * s + jnp.dot(k_e.T.astype(BF16), vb,
                                                                preferred_element_type=F32)

    @pl.when(n == pl.num_programs(0) - 1)
    def _():
        sfin_ref[...] = s_scr[...].reshape(sfin_ref.shape)


GLA_CHUNKS_PER_STEP = 2


def _gla_prompt(q, k, v, la):
    bsz, s, _ = q.shape
    rows = GLA_CHUNK * GLA_CHUNKS_PER_STEP
    qk_spec = pl.BlockSpec((bsz, rows, GLA_HEADS * GLA_DK), lambda n: (0, n, 0))
    v_spec = pl.BlockSpec((bsz, rows, GLA_WIDTH), lambda n: (0, n, 0))
    return pl.pallas_call(
        _gla_prompt_body,
        grid=(s // rows,),
        in_specs=[qk_spec, qk_spec, v_spec, qk_spec],
        out_specs=[v_spec, pl.BlockSpec((bsz, GLA_HEADS, GLA_DK, GLA_DV), lambda n: (0, 0, 0, 0))],
        out_shape=[jax.ShapeDtypeStruct((bsz, s, GLA_WIDTH), F32),
                   jax.ShapeDtypeStruct((bsz, GLA_HEADS, GLA_DK, GLA_DV), F32)],
        scratch_shapes=[pltpu.VMEM((bsz * GLA_HEADS, GLA_DK, GLA_DV), F32)],
        compiler_params=_params(1),
        name="gla_prompt",
    )(q, k, v, la)


def _gla_sample_body(q_ref, k_ref, v_ref, la_ref, s0_ref, o_ref, sout_ref):
    bb, t, _ = q_ref.shape
    rows = bb * t
    q = q_ref[...].reshape(rows, GLA_DK)
    k = k_ref[...].reshape(rows, GLA_DK)
    la = la_ref[...].reshape(rows, GLA_DK)
    v = v_ref[...].reshape(rows, GLA_DV)
    row = lax.broadcasted_iota(jnp.int32, (rows, rows), 0)
    col = lax.broadcasted_iota(jnp.int32, (rows, rows), 1)
    same_seq = (row // t) == (col // t)
    causal = same_seq & (row >= col)
    b = jnp.dot(causal.astype(F32), la, precision=HIGHEST, preferred_element_type=F32)
    b_tot = jnp.dot(same_seq.astype(F32), la, precision=HIGHEST, preferred_element_type=F32)
    q_t = (q * jnp.exp(b)).astype(BF16)
    k_t = (k * jnp.exp(-b)).astype(BF16)
    k_e_t = (k * jnp.exp(b_tot - b)).T.astype(BF16)
    la_t = la.T
    a = lax.dot_general(q_t, k_t, NT_DIMS, preferred_element_type=F32)
    a = jnp.where(causal, a, 0.0).astype(BF16)
    o = jnp.dot(a, v.astype(BF16), preferred_element_type=F32)
    seq_of_row_v = lax.broadcasted_iota(jnp.int32, (rows, GLA_DV), 0) // t
    seq_of_row_k = lax.broadcasted_iota(jnp.int32, (rows, GLA_DK), 0) // t
    for i in range(bb):
        s0 = s0_ref[i]
        mine = seq_of_row_v == i
        o = o + jnp.where(mine, jnp.dot(q_t, s0.astype(BF16), preferred_element_type=F32), 0.0)
        v_i = jnp.where(mine, v, 0.0).astype(BF16)
        ones_i = jnp.where(seq_of_row_k == i, 1.0, 0.0)
        decay = jnp.exp(jnp.dot(la_t, ones_i, precision=HIGHEST, preferred_element_type=F32))
        decay = jnp.concatenate([decay, decay], axis=1)
        sout_ref[i] = decay * s0 + jnp.dot(k_e_t, v_i, preferred_element_type=F32)
    o_ref[...] = o.reshape(bb, t, GLA_DV)


def _gla_sample(q, k, v, la, s0):
    bsz, t, _ = q.shape
    bb = SAMPLE_BATCH_BLOCK
    qk_spec = pl.BlockSpec((bb, t, GLA_DK), lambda i, h: (i, 0, h))
    v_spec = pl.BlockSpec((bb, t, GLA_DV), lambda i, h: (i, 0, h))
    s_spec = pl.BlockSpec((bb, None, GLA_DK, GLA_DV), lambda i, h: (i, h, 0, 0))
    return pl.pallas_call(
        _gla_sample_body,
        grid=(bsz // bb, GLA_HEADS),
        in_specs=[qk_spec, qk_spec, v_spec, qk_spec, s_spec],
        out_specs=[v_spec, s_spec],
        out_shape=[jax.ShapeDtypeStruct((bsz, t, GLA_WIDTH), F32),
                   jax.ShapeDtypeStruct(s0.shape, F32)],
        compiler_params=_params(2),
        name="gla_sample",
    )(q, k, v, la, s0)


def _sink_softmax(s, mask, sink):
    s = jnp.where(mask, s, NEG_INF)
    m = jnp.maximum(jnp.max(s, axis=-1, keepdims=True), sink)
    p = jnp.exp(s - m)
    return p / (jnp.sum(p, axis=-1, keepdims=True) + jnp.exp(sink - m))


def _split_kv_lanes(x):
    lane = lax.broadcasted_iota(jnp.int32, x.shape, 1)
    low = lane < SWA_HEAD_DIM
    lo0 = jnp.where(low, x, 0.0)
    hi1 = jnp.where(low, 0.0, x)
    hi0 = pltpu.roll(lo0, SWA_HEAD_DIM, 1)
    lo1 = pltpu.roll(hi1, SWA_HEAD_DIM, 1)
    return (lo0.astype(BF16), lo1.astype(BF16)), (hi0.astype(BF16), hi1.astype(BF16))


def _swa_prompt_body(sink_ref, q_ref, kc_ref, kp_ref, vc_ref, vp_ref, o_ref, st_scr, p_scr):
    n = pl.program_id(1)
    blk = WINDOW
    k_parts = _split_kv_lanes(jnp.concatenate([kp_ref[...], kc_ref[...]], axis=0))
    v_parts = _split_kv_lanes(jnp.concatenate([vp_ref[...], vc_ref[...]], axis=0))
    kj = lax.broadcasted_iota(jnp.int32, (2 * blk, blk), 0)
    qi = lax.broadcasted_iota(jnp.int32, (2 * blk, blk), 1)
    diff = qi + blk - kj
    mask = (diff >= 0) & (diff < WINDOW) & ((n > 0) | (kj >= blk))
    scale = SWA_HEAD_DIM ** -0.5
    contract_rows = (((0,), (0,)), ((), ()))
    z = lax.shift_right_logical(n, 30)
    for head in range(SWA_HEADS):
        pair, half = divmod(head, 2)
        qp = q_ref[:, pair * LANES:(pair + 1) * LANES].astype(BF16)
        st_scr[z, head] = lax.dot_general(k_parts[half][head // SWA_GROUP], qp, NT_DIMS,
                                          preferred_element_type=F32)
    for head in range(SWA_HEADS):
        sink = sink_ref[head]
        st = jnp.where(mask, st_scr[z, head] * scale, NEG_INF)
        m = jnp.maximum(jnp.max(st, axis=0, keepdims=True), sink)
        p = jnp.exp(st - m)
        denom = jnp.sum(p, axis=0, keepdims=True) + jnp.exp(sink - m)
        p_scr[z, head] = (p * (1.0 / denom)).astype(BF16)
    for pair in range(SWA_HEADS // 2):
        kv = (2 * pair) // SWA_GROUP
        o_ref[:, pair * LANES:(pair + 1) * LANES] = (
            lax.dot_general(p_scr[z, 2 * pair], v_parts[0][kv], contract_rows, preferred_element_type=F32)
            + lax.dot_general(p_scr[z, 2 * pair + 1], v_parts[1][kv], contract_rows,
                              preferred_element_type=F32))


def _swa_prompt(sq, sk, sv, sinks):
    bsz, s, _ = sq.shape
    blk = WINDOW
    cur = lambda b, n: (b, n, 0)
    prev = lambda b, n: (b, jnp.maximum(n - 1, 0), 0)
    kv_cur = pl.BlockSpec((None, blk, LANES), cur)
    kv_prev = pl.BlockSpec((None, blk, LANES), prev)
    q_spec = pl.BlockSpec((None, blk, SWA_WIDTH), cur)
    return pl.pallas_call(
        _swa_prompt_body,
        grid=(bsz, s // blk),
        in_specs=[pl.BlockSpec(memory_space=pltpu.SMEM), q_spec, kv_cur, kv_prev, kv_cur, kv_prev],
        out_specs=q_spec,
        out_shape=jax.ShapeDtypeStruct((bsz, s, SWA_WIDTH), F32),
        scratch_shapes=[pltpu.VMEM((1, SWA_HEADS, 2 * blk, blk), F32),
                        pltpu.VMEM((1, SWA_HEADS, 2 * blk, blk), BF16)],
        compiler_params=_params(2),
        name="swa_prompt",
    )(sinks, sq, sk, sk, sv, sv)


def _swa_sample_body(t_new, sink_ref, q_ref, ck_ref, cv_ref, kn_ref, vn_ref, o_ref, nk_ref, nv_ref):
    bb, rows, _ = q_ref.shape
    w = ck_ref.shape[1]
    pad = jnp.zeros((16 - t_new, LANES), F32)
    n_keys = w + 16
    tq = lax.broadcasted_iota(jnp.int32, (rows, n_keys), 0) // SWA_HEADS
    c = lax.broadcasted_iota(jnp.int32, (rows, n_keys), 1)
    mask = ((c < w) & (c > tq)) | ((c >= w) & (c - w <= tq) & (c < w + t_new))
    sink = sink_ref[...]
    scale = SWA_HEAD_DIM ** -0.5
    for i in range(bb):
        ck, cv, kn, vn = ck_ref[i], cv_ref[i], kn_ref[i], vn_ref[i]
        kcat = jnp.concatenate([ck, kn, pad], axis=0).astype(BF16)
        vcat = jnp.concatenate([cv, vn, pad], axis=0).astype(BF16)
        s = lax.dot_general(q_ref[i].astype(BF16), kcat, NT_DIMS, preferred_element_type=F32) * scale
        p = _sink_softmax(s, mask, sink)
        o_ref[i] = jnp.dot(p.astype(BF16), vcat, preferred_element_type=F32)
        nk_ref[i, 0:w - t_new, :] = ck[t_new:, :]
        nk_ref[i, w - t_new:w, :] = kn
        nv_ref[i, 0:w - t_new, :] = cv[t_new:, :]
        nv_ref[i, w - t_new:w, :] = vn


def _swa_sample(q_pad, ck, cv, kn, vn, sink_col):
    bsz, rows, _ = q_pad.shape
    t_new = kn.shape[1]
    w = ck.shape[1]
    bb = SAMPLE_BATCH_BLOCK
    blk = lambda r: pl.BlockSpec((bb, r, LANES), lambda i: (i, 0, 0))
    return pl.pallas_call(
        functools.partial(_swa_sample_body, t_new),
        grid=(bsz // bb,),
        in_specs=[_resident(sink_col.shape), blk(rows), blk(w), blk(w), blk(t_new), blk(t_new)],
        out_specs=[blk(rows), blk(w), blk(w)],
        out_shape=[jax.ShapeDtypeStruct((bsz, rows, LANES), F32),
                   jax.ShapeDtypeStruct(ck.shape, F32), jax.ShapeDtypeStruct(cv.shape, F32)],
        compiler_params=_params(1),
        name="swa_sample",
    )(sink_col, q_pad, ck, cv, kn, vn)


def _merge_body(og_ref, r_ref, os_ref, x_ref, gn_ref, wog_ref, wos_ref, gf_ref, wq_ref, keys_ref,
                h_ref, hn_ref, st_ref):
    og = og_ref[...]
    gn = gn_ref[...]
    heads = [_rms(og[:, i * GLA_DV:(i + 1) * GLA_DV], gn) for i in range(GLA_HEADS)]
    r = r_ref[...]
    gated = jnp.concatenate(heads, axis=1) * (r / (1.0 + jnp.exp(-r)))
    h = (x_ref[...]
         + jnp.dot(gated.astype(BF16), wog_ref[...], preferred_element_type=F32)
         + jnp.dot(os_ref[...].astype(BF16), wos_ref[...], preferred_element_type=F32))
    h_ref[...] = h
    hn_f32 = _rms(h, gf_ref[...])
    hn = hn_f32.astype(BF16)
    hn_ref[...] = hn_f32.T.astype(BF16)
    pq = jnp.dot(hn, wq_ref[...], preferred_element_type=F32).astype(BF16)
    for i in range(PEER_HEADS):
        st_ref[i] = lax.dot_general(keys_ref[i], pq[:, i * PEER_QDIM:(i + 1) * PEER_QDIM], NT_DIMS,
                                    preferred_element_type=F32)


def _merge(o_gla, r, o_swa, x, gn, wog, wos, gf, wq, keys):
    t, d = x.shape
    tm = PROJ_ROWS
    row = lambda i: (i, 0)
    return pl.pallas_call(
        _merge_body,
        grid=(t // tm,),
        in_specs=[pl.BlockSpec((tm, GLA_WIDTH), row), pl.BlockSpec((tm, GLA_WIDTH), row),
                  pl.BlockSpec((tm, o_swa.shape[1]), row), pl.BlockSpec((tm, d), row),
                  _resident(gn.shape), _resident(wog.shape), _resident(wos.shape),
                  _resident(gf.shape), _resident(wq.shape), _resident(keys.shape)],
        out_specs=[pl.BlockSpec((tm, d), row), pl.BlockSpec((d, tm), lambda i: (0, i)),
                   pl.BlockSpec((PEER_HEADS, 2 * PEER_N_KEYS, tm), lambda i: (0, 0, i))],
        out_shape=[jax.ShapeDtypeStruct((t, d), F32), jax.ShapeDtypeStruct((d, t), BF16),
                   jax.ShapeDtypeStruct((PEER_HEADS, 2 * PEER_N_KEYS, t), F32)],
        compiler_params=_params(1),
        name="merge",
    )(o_gla, r, o_swa, x, gn, wog, wos, gf, wq, keys)


N_SORTED = PEER_TOPK + 1


def _sorting_network(n):
    pairs = []
    p = 1
    while p < n:
        k = p
        while k >= 1:
            for j in range(k % p, n - k, 2 * k):
                for i in range(min(k, n - j - k)):
                    if (i + j) // (2 * p) == (i + j + k) // (2 * p):
                        pairs.append((i + j, i + j + k))
            k //= 2
        p *= 2
    return pairs


def _topk_body(st_ref, aux_ref, top0_scr, top1_scr):
    scr = (top0_scr, top1_scr)
    sublanes = 8
    n_tiles = PEER_N_KEYS // sublanes
    network = _sorting_network(n_tiles)
    for h in range(PEER_HEADS):
        for half in range(2):
            base = half * PEER_N_KEYS
            col = [st_ref[h, base + k * sublanes:base + (k + 1) * sublanes, :] for k in range(n_tiles)]
            for i, j in network:
                col[i], col[j] = jnp.maximum(col[i], col[j]), jnp.minimum(col[i], col[j])
            for a in range(N_SORTED):
                m = col[0]
                for shift in (4, 2, 1):
                    m = jnp.maximum(m, pltpu.roll(m, shift, 0))
                scr[half][a, h:h + 1, :] = m[0:1, :]
                if a + 1 < N_SORTED:
                    hit = col[0] == m
                    for k in range(min(n_tiles, N_SORTED - 1 - a)):
                        below = col[k + 1] if k + 1 < n_tiles else NEG_INF
                        col[k] = jnp.where(hit, below, col[k])
    top0 = [top0_scr[a] for a in range(N_SORTED)]
    top1 = [top1_scr[a] for a in range(N_SORTED)]
    cands = [top0[a - 1] + top1[b - 1]
             for a in range(1, N_SORTED + 1) for b in range(1, N_SORTED // a + 1)]
    best = None
    z = None
    for rank in range(N_SORTED):
        m = functools.reduce(jnp.maximum, cands)
        if rank == 0:
            best = m
            z = jnp.ones_like(m)
        elif rank < PEER_TOPK:
            z = z + jnp.exp(m - best)
        if rank == PEER_TOPK - 1:
            kth = m
        if rank + 1 < N_SORTED:
            cands = [jnp.where(cnd == m, NEG_INF, cnd) for cnd in cands]
    aux_ref[0] = 0.5 * (kth + m)
    aux_ref[1] = top0[0]
    aux_ref[2] = top1[0]
    aux_ref[3] = 1.0 / z


def _topk(st):
    t = st.shape[2]
    tt = TOPK_TOKENS
    return pl.pallas_call(
        _topk_body,
        grid=(t // tt,),
        in_specs=[pl.BlockSpec((PEER_HEADS, 2 * PEER_N_KEYS, tt), lambda i: (0, 0, i))],
        out_specs=pl.BlockSpec((4, PEER_HEADS, tt), lambda i: (0, 0, i)),
        out_shape=jax.ShapeDtypeStruct((4, PEER_HEADS, t), F32),
        scratch_shapes=[pltpu.VMEM((N_SORTED, PEER_HEADS, tt), F32),
                        pltpu.VMEM((N_SORTED, PEER_HEADS, tt), F32)],
        compiler_params=_params(1),
        name="peer_topk",
    )(st)


def _peer_body(st_ref, aux_ref, hnt_ref, u_ref, vt_ref, out_ref,
               thr_scr, w0_scr, e1_scr, thr_cur, w0_cur, acc_scr, *piece_scr):
    ec = pl.program_id(1)
    n_rows = PEER_CHUNK // PEER_N_KEYS
    tb = hnt_ref.shape[1]
    a_scr = piece_scr[:PEER_PARTS]
    c_scr = piece_scr[PEER_PARTS:]

    @pl.when(ec == 0)
    def _():
        grid_rows = (PEER_N_KEYS // n_rows, n_rows, tb)
        for h in range(PEER_HEADS):
            s0 = st_ref[h, 0:PEER_N_KEYS, :]
            s1 = st_ref[h, PEER_N_KEYS:2 * PEER_N_KEYS, :]
            thr_scr[h] = jnp.exp(aux_ref[0, h:h + 1, :] - s0 - aux_ref[2, h:h + 1, :]).reshape(grid_rows)
            w0_scr[h] = (jnp.exp(s0 - aux_ref[1, h:h + 1, :]) * aux_ref[3, h:h + 1, :]).reshape(grid_rows)
            e1_scr[h] = jnp.exp(s1 - aux_ref[2, h:h + 1, :])
        acc_scr[...] = jnp.zeros_like(acc_scr)

    sub = 2 * BF16_SUBLANES
    inv_sqrt2 = 1.0 / math.sqrt(2.0)
    e_part = PEER_CHUNK // PEER_PARTS
    rows_per_part = n_rows // PEER_PARTS

    z = lax.shift_right_logical(ec, 30)

    for h in range(PEER_HEADS):
        thr_cur[h] = thr_scr[h, ec]
        w0_cur[h] = w0_scr[h, ec]

    def scores(p):
        a_scr[p][z] = jnp.dot(u_ref[p * e_part:(p + 1) * e_part, :], hnt_ref[...],
                              preferred_element_type=F32)

    def coefficients(p):
        for r in range(rows_per_part):
            ii = p * rows_per_part + r
            for lt in range(tb // LANES):
                ls = slice(lt * LANES, (lt + 1) * LANES)
                thr_b = [thr_cur[h, ii:ii + 1, ls] for h in range(PEER_HEADS)]
                w0_b = [w0_cur[h, ii:ii + 1, ls] for h in range(PEER_HEADS)]
                for jb in range(PEER_N_KEYS // sub):
                    rows = slice(r * PEER_N_KEYS + jb * sub, r * PEER_N_KEYS + (jb + 1) * sub)
                    a = a_scr[p][z, rows, ls]
                    act = 0.5 * a * (1.0 + lax.erf(a * inv_sqrt2))
                    g = None
                    for h in range(PEER_HEADS):
                        e1 = e1_scr[h, jb * sub:(jb + 1) * sub, ls]
                        term = jnp.where(e1 >= thr_b[h], e1, 0.0) * w0_b[h]
                        g = term if g is None else g + term
                    c_scr[p][z, rows, ls] = (g * act).astype(BF16)

    def values(p):
        acc_scr[z] += jnp.dot(vt_ref[p], c_scr[p][z], preferred_element_type=F32)

    for p in range(PEER_PARTS):
        scores(p)
    for p in range(PEER_PARTS):
        coefficients(p)
        values(p)

    @pl.when(ec == pl.num_programs(1) - 1)
    def _():
        out_ref[...] = acc_scr[0].T


def _peer(st, aux, hnt, u, vt):
    d, t = hnt.shape
    tb = PEER_TOKENS
    ec = PEER_CHUNK
    n_chunks = u.shape[0] // ec
    rows = ec // PEER_N_KEYS
    return pl.pallas_call(
        _peer_body,
        grid=(t // tb, n_chunks),
        in_specs=[pl.BlockSpec((PEER_HEADS, 2 * PEER_N_KEYS, tb), lambda i, j: (0, 0, i),
                               pipeline_mode=pl.Buffered(1)),
                  pl.BlockSpec((4, PEER_HEADS, tb), lambda i, j: (0, 0, i)),
                  pl.BlockSpec((d, tb), lambda i, j: (0, i), pipeline_mode=pl.Buffered(1)),
                  pl.BlockSpec((ec, d), lambda i, j: (j, 0)),
                  pl.BlockSpec((PEER_PARTS, d, ec // PEER_PARTS), lambda i, j: (j, 0, 0))],
        out_specs=pl.BlockSpec((tb, d), lambda i, j: (i, 0)),
        out_shape=jax.ShapeDtypeStruct((t, d), F32),
        scratch_shapes=([pltpu.VMEM((PEER_HEADS, n_chunks, rows, tb), F32),
                         pltpu.VMEM((PEER_HEADS, n_chunks, rows, tb), F32),
                         pltpu.VMEM((PEER_HEADS, PEER_N_KEYS, tb), F32),
                         pltpu.VMEM((PEER_HEADS, rows, tb), F32),
                         pltpu.VMEM((PEER_HEADS, rows, tb), F32),
                         pltpu.VMEM((1, d, tb), F32)]
                        + [pltpu.VMEM((1, ec // PEER_PARTS, tb), F32)] * PEER_PARTS
                        + [pltpu.VMEM((1, ec // PEER_PARTS, tb), BF16)] * PEER_PARTS),
        compiler_params=_params(2),
        name="peer_dense",
    )(st, aux, hnt, u, vt)


def _final_body(h_ref, p_ref, g_ref, y_ref):
    y_ref[...] = _rms(h_ref[...] + p_ref[...], g_ref[...])


def _final(h, p, g):
    t, d = h.shape
    tm = PROJ_ROWS
    row = lambda i: (i, 0)
    return pl.pallas_call(
        _final_body,
        grid=(t // tm,),
        in_specs=[pl.BlockSpec((tm, d), row), pl.BlockSpec((tm, d), row), _resident(g.shape)],
        out_specs=pl.BlockSpec((tm, d), row),
        out_shape=jax.ShapeDtypeStruct((t, d), F32),
        compiler_params=_params(1),
        name="final",
    )(h, p, g)


def _rope_tables(pos):
    half = SWA_HEAD_DIM // 2
    inv = jnp.exp(-math.log(ROPE_THETA) * jnp.arange(half, dtype=F32) * (2.0 / SWA_HEAD_DIM))
    ang = pos.astype(F32)[:, None] * inv[None, :]
    cos, sin = jnp.cos(ang), jnp.sin(ang)
    reps = LANES // SWA_HEAD_DIM
    return (jnp.tile(jnp.concatenate([cos, cos], axis=1), (1, reps)),
            jnp.tile(jnp.concatenate([-sin, sin], axis=1), (1, reps)))


def _split_w_in(w_in):
    out, start = [], 0
    for width in (512, 512, 1024, 1024, GLA_GATE_RANK, SWA_WIDTH, LANES, LANES):
        out.append(w_in[:, start:start + width])
        start += width
    return out


def _pad_heads_to_kv_half(w):
    lead = w.shape[:-1]
    w = w.reshape(lead + (SWA_KV_HEADS, SWA_GROUP, 1, SWA_HEAD_DIM))
    sel = jnp.eye(SWA_KV_HEADS, dtype=w.dtype).reshape(SWA_KV_HEADS, 1, SWA_KV_HEADS, 1)
    return (w * sel).reshape(lead + (SWA_HEADS * LANES,))


def kernel(x_prompt, x_sample, state_gla, cache_win_k, cache_win_v, g_attn, w_in, w_gate_up, b_gate,
           g_gla_out, attn_sinks, w_out, g_ffn, w_peer_q, peer_sub_keys, peer_u, peer_v, g_final):
    bsz, seq, d = x_prompt.shape
    dbsz, dseq, _ = x_sample.shape
    win = cache_win_k.shape[2]

    gq, gk, gv, gr, ga, wsq, wsk, wsv = _split_w_in(w_in[0])
    ga = jnp.pad(ga, ((0, 0), (0, LANES - GLA_GATE_RANK)))
    w_prompt = jnp.concatenate([gq, gk, gv, gr, wsq, wsk, wsv, ga], axis=1).astype(BF16)
    w_sample = jnp.concatenate([gq, gk, gv, gr, _pad_heads_to_kv_half(wsq), wsk, wsv, ga],
                               axis=1).astype(BF16)
    wgu = jnp.pad(w_gate_up[0], ((0, LANES - GLA_GATE_RANK), (0, 0)))
    bg = b_gate[0][None, :]
    g_a = g_attn[0][None, :]
    g_f = g_ffn[0][None, :]
    g_n = g_gla_out[0][None, :]
    wo_gla = w_out[0, :GLA_WIDTH].astype(BF16)
    wo_swa = w_out[0, GLA_WIDTH:]
    wo_swa_pad = _pad_heads_to_kv_half(wo_swa.T).T.astype(BF16)
    wo_swa = wo_swa.astype(BF16)
    wq = w_peer_q[0].astype(BF16)
    sk0 = jnp.pad(peer_sub_keys[0, :, 0], ((0, 0), (0, 0), (0, PEER_HALF)))
    sk1 = jnp.pad(peer_sub_keys[0, :, 1], ((0, 0), (0, 0), (PEER_HALF, 0)))
    keys = jnp.concatenate([sk0, sk1], axis=1).astype(BF16)
    u = peer_u[0].astype(BF16)
    e_piece = PEER_CHUNK // PEER_PARTS
    vt = peer_v[0].reshape(PEER_EXPERTS // e_piece, e_piece, d).transpose(0, 2, 1).astype(BF16)
    sinks = attn_sinks[0]
    sink_col = jnp.tile(sinks, dseq)[:, None]

    cos_p, sin_p = _rope_tables(jnp.arange(seq))
    cos_s, sin_s = _rope_tables(PAST_LEN + jnp.arange(dseq))
    cos_s = jnp.tile(cos_s, (PROJ_ROWS // dseq, 1))
    sin_s = jnp.tile(sin_s, (PROJ_ROWS // dseq, 1))

    def ffn(h, hn, st):
        aux = _topk(st)
        return _final(h, _peer(st, aux, hn, u, vt), g_final[None, :])

    tp = bsz * seq
    q, k, v, r, la, sq, sk, sv = _proj(x_prompt.reshape(tp, d), g_a, w_prompt, wgu, bg, cos_p, sin_p,
                                       SWA_WIDTH)
    b3 = lambda a: a.reshape(bsz, seq, a.shape[-1])
    o_gla, s_prompt = _gla_prompt(b3(q), b3(k), b3(v), b3(la))
    o_swa = _swa_prompt(b3(sq), b3(sk), b3(sv), sinks)
    h, hn, st = _merge(o_gla.reshape(tp, GLA_WIDTH), r, o_swa.reshape(tp, SWA_WIDTH),
                       x_prompt.reshape(tp, d), g_n, wo_gla, wo_swa, g_f, wq, keys)
    y_prompt = ffn(h, hn, st).reshape(bsz, seq, d)
    kv_shape = (1, bsz, WINDOW, SWA_KV_HEADS, SWA_HEAD_DIM)
    k_prompt = b3(sk)[:, seq - WINDOW:].reshape(kv_shape)
    v_prompt = b3(sv)[:, seq - WINDOW:].reshape(kv_shape)

    ts = dbsz * dseq
    q, k, v, r, la, sq, sk, sv = _proj(x_sample.reshape(ts, d), g_a, w_sample, wgu, bg, cos_s, sin_s,
                                       SWA_HEADS * LANES)
    d3 = lambda a: a.reshape(dbsz, dseq, a.shape[-1])
    o_gla, s_sample = _gla_sample(d3(q), d3(k), d3(v), d3(la), state_gla[0])
    o_swa, k_sample, v_sample = _swa_sample(
        sq.reshape(dbsz, dseq * SWA_HEADS, LANES),
        cache_win_k[0].reshape(dbsz, win, LANES), cache_win_v[0].reshape(dbsz, win, LANES),
        d3(sk), d3(sv), sink_col)
    h, hn, st = _merge(o_gla.reshape(ts, GLA_WIDTH), r, o_swa.reshape(ts, SWA_HEADS * LANES),
                       x_sample.reshape(ts, d), g_n, wo_gla, wo_swa_pad, g_f, wq, keys)
    y_sample = ffn(h, hn, st).reshape(dbsz, dseq, d)
    ckv_shape = (1, dbsz, win, SWA_KV_HEADS, SWA_HEAD_DIM)

    return (y_prompt, y_sample, s_prompt[None], k_prompt, v_prompt, s_sample[None],
            k_sample.reshape(ckv_shape), v_sample.reshape(ckv_shape))
```

```python
import functools
import math

import jax
import jax.numpy as jnp
from jax import lax
from jax.experimental import pallas as pl
from jax.experimental.pallas import tpu as pltpu

F32 = jnp.float32
BF16 = jnp.bfloat16
HIGHEST = lax.Precision.HIGHEST

D_MODEL = 2048
GLA_HEADS = 4
GLA_DK = 128
GLA_DV = 256
GLA_WIDTH = GLA_HEADS * GLA_DV
GLA_GATE_RANK = 16
GLA_TAU = 16.0
GLA_CHUNK = 64
SWA_HEAD_DIM = 64
SWA_HEADS = 16
SWA_KV_HEADS = 2
SWA_GROUP = SWA_HEADS // SWA_KV_HEADS
SWA_WIDTH = SWA_HEADS * SWA_HEAD_DIM
WINDOW = 128
ROPE_THETA = 10000.0
PAST_LEN = 8192
PEER_N_KEYS = 128
PEER_EXPERTS = PEER_N_KEYS * PEER_N_KEYS
PEER_HEADS = 8
PEER_QDIM = 128
PEER_HALF = PEER_QDIM // 2
PEER_TOPK = 16
EPS = 1e-6
NEG_INF = -1e30

LANES = 128
BF16_SUBLANES = 16
PROJ_ROWS = 256
TOPK_TOKENS = 128
PEER_TOKENS = 512
PEER_CHUNK = 1024
PEER_PARTS = 2
SAMPLE_BATCH_BLOCK = 8
VMEM_LIMIT = 60 * 1024 * 1024

NT_DIMS = (((1,), (1,)), ((), ()))


def _resident(shape):
    nd = len(shape)
    return pl.BlockSpec(shape, lambda *_: (0,) * nd, pipeline_mode=pl.Buffered(1))


def _params(n_axes):
    return pltpu.CompilerParams(dimension_semantics=("arbitrary",) * n_axes,
                                vmem_limit_bytes=VMEM_LIMIT)


def _rms(x, g):
    return x * lax.rsqrt(jnp.mean(x * x, axis=-1, keepdims=True) + EPS) * g


def _proj_body(x_ref, g_ref, wg_ref, ws_ref, wgu_ref, bg_ref, cos_ref, sin_ref,
               q_ref, k_ref, v_ref, r_ref, la_ref, sq_ref, sk_ref, sv_ref):
    xn = _rms(x_ref[...], g_ref[...]).astype(BF16)
    wsq = sq_ref.shape[1]

    def proj(w_ref, start, width):
        return jnp.dot(xn, w_ref[:, start:start + width], preferred_element_type=F32)

    nk = GLA_HEADS * GLA_DK
    q_ref[...] = proj(wg_ref, 0, nk) * (GLA_DK ** -0.5)
    k_ref[...] = proj(wg_ref, nk, nk)
    v_ref[...] = proj(wg_ref, 2 * nk, GLA_WIDTH)
    r_ref[...] = proj(wg_ref, 2 * nk + GLA_WIDTH, GLA_WIDTH)

    sq_all = proj(ws_ref, 0, wsq)
    tail = proj(ws_ref, wsq, 3 * LANES)
    ga = tail[:, 2 * LANES:]
    z = jnp.dot(ga, wgu_ref[...], precision=HIGHEST, preferred_element_type=F32) + bg_ref[...]
    la_ref[...] = (jnp.minimum(z, 0.0) - jnp.log1p(jnp.exp(-jnp.abs(z)))) * (1.0 / GLA_TAU)

    cos = cos_ref[...]
    sin = sin_ref[...]
    lane = lax.broadcasted_iota(jnp.int32, cos.shape, 1)
    first_half = (lane & (SWA_HEAD_DIM // 2)) == 0

    def rope(t):
        rot = jnp.where(first_half, pltpu.roll(t, LANES - SWA_HEAD_DIM // 2, 1),
                        pltpu.roll(t, SWA_HEAD_DIM // 2, 1))
        return t * cos + rot * sin

    for c in range(wsq // LANES):
        sq_ref[:, c * LANES:(c + 1) * LANES] = rope(sq_all[:, c * LANES:(c + 1) * LANES])
    sk_ref[...] = rope(tail[:, :LANES])
    sv_ref[...] = tail[:, LANES:2 * LANES]


def _proj(x, g, w_gla, w_swa, wgu, bg, cos, sin):
    t, d = x.shape
    tm = PROJ_ROWS
    n_tab = cos.shape[0] // tm
    row = lambda i: (i, 0)
    tab = lambda i: (i % n_tab, 0)
    wsq = w_swa.shape[1] - 3 * LANES
    nk = GLA_HEADS * GLA_DK
    widths = (nk, nk, GLA_WIDTH, GLA_WIDTH, nk, wsq, LANES, LANES)
    return pl.pallas_call(
        _proj_body,
        grid=(t // tm,),
        in_specs=[pl.BlockSpec((tm, d), row), _resident(g.shape), _resident(w_gla.shape),
                  _resident(w_swa.shape), _resident(wgu.shape), _resident(bg.shape),
                  pl.BlockSpec((tm, LANES), tab), pl.BlockSpec((tm, LANES), tab)],
        out_specs=[pl.BlockSpec((tm, n), row) for n in widths],
        out_shape=[jax.ShapeDtypeStruct((t, n), F32) for n in widths],
        compiler_params=_params(1),
        name="proj",
    )(x, g, w_gla, w_swa, wgu, bg, cos, sin)


def _gla_prompt_body(q_ref, k_ref, v_ref, la_ref, o_ref, sfin_ref, s_scr):
    n = pl.program_id(0)
    bsz = q_ref.shape[0]

    @pl.when(n == 0)
    def _():
        s_scr[...] = jnp.zeros_like(s_scr)

    c = GLA_CHUNK
    row = lax.broadcasted_iota(jnp.int32, (c, c), 0)
    col = lax.broadcasted_iota(jnp.int32, (c, c), 1)
    causal = row >= col
    tri = causal.astype(F32)
    for ch in range(q_ref.shape[1] // c):
        rows = slice(ch * c, (ch + 1) * c)
        for bi in range(bsz):
            b_all = jnp.dot(tri, la_ref[bi, rows, :], precision=HIGHEST, preferred_element_type=F32)
            for h in range(GLA_HEADS):
                kl = slice(h * GLA_DK, (h + 1) * GLA_DK)
                vl = slice(h * GLA_DV, (h + 1) * GLA_DV)
                b = b_all[:, kl]
                b_last = b[c - 1:c, :]
                k = k_ref[bi, rows, kl]
                q_t = (q_ref[bi, rows, kl] * jnp.exp(b)).astype(BF16)
                k_t = (k * jnp.exp(-b)).astype(BF16)
                k_e = k * jnp.exp(b_last - b)
                vb = v_ref[bi, rows, vl].astype(BF16)
                a = lax.dot_general(q_t, k_t, NT_DIMS, preferred_element_type=F32)
                a = jnp.where(causal, a, 0.0).astype(BF16)
                s = s_scr[bi * GLA_HEADS + h]
                o_ref[bi, rows, vl] = (jnp.dot(a, vb, preferred_element_type=F32)
                                       + jnp.dot(q_t, s.astype(BF16), preferred_element_type=F32))
                decay = jnp.exp(jnp.broadcast_to(b_last, (GLA_DK, GLA_DK))).T
                decay = jnp.concatenate([decay, decay], axis=1)
                s_scr[bi * GLA_HEADS + h] = decay * s + jnp.dot(k_e.T.astype(BF16), vb,
                                                                preferred_element_type=F32)

    @pl.when(n == pl.num_programs(0) - 1)
    def _():
        sfin_ref[...] = s_scr[...].reshape(sfin_ref.shape)


GLA_CHUNKS_PER_STEP = 2


def _gla_prompt(q, k, v, la):
    bsz, s, _ = q.shape
    rows = GLA_CHUNK * GLA_CHUNKS_PER_STEP
    qk_spec = pl.BlockSpec((bsz, rows, GLA_HEADS * GLA_DK), lambda n: (0, n, 0))
    v_spec = pl.BlockSpec((bsz, rows, GLA_WIDTH), lambda n: (0, n, 0))
    return pl.pallas_call(
        _gla_prompt_body,
        grid=(s // rows,),
        in_specs=[qk_spec, qk_spec, v_spec, qk_spec],
        out_specs=[v_spec, pl.BlockSpec((bsz, GLA_HEADS, GLA_DK, GLA_DV), lambda n: (0, 0, 0, 0))],
        out_shape=[jax.ShapeDtypeStruct((bsz, s, GLA_WIDTH), F32),
                   jax.ShapeDtypeStruct((bsz, GLA_HEADS, GLA_DK, GLA_DV), F32)],
        scratch_shapes=[pltpu.VMEM((bsz * GLA_HEADS, GLA_DK, GLA_DV), F32)],
        compiler_params=_params(1),
        name="gla_prompt",
    )(q, k, v, la)


def _gla_sample_body(q_ref, k_ref, v_ref, la_ref, s0_ref, o_ref, sout_ref):
    bb, t, _ = q_ref.shape
    rows = bb * t
    q = q_ref[...].reshape(rows, GLA_DK)
    k = k_ref[...].reshape(rows, GLA_DK)
    la = la_ref[...].reshape(rows, GLA_DK)
    v = v_ref[...].reshape(rows, GLA_DV)
    row = lax.broadcasted_iota(jnp.int32, (rows, rows), 0)
    col = lax.broadcasted_iota(jnp.int32, (rows, rows), 1)
    same_seq = (row // t) == (col // t)
    causal = same_seq & (row >= col)
    b = jnp.dot(causal.astype(F32), la, precision=HIGHEST, preferred_element_type=F32)
    b_tot = jnp.dot(same_seq.astype(F32), la, precision=HIGHEST, preferred_element_type=F32)
    q_t = (q * jnp.exp(b)).astype(BF16)
    k_t = (k * jnp.exp(-b)).astype(BF16)
    k_e_t = (k * jnp.exp(b_tot - b)).T.astype(BF16)
    la_t = la.T
    a = lax.dot_general(q_t, k_t, NT_DIMS, preferred_element_type=F32)
    a = jnp.where(causal, a, 0.0).astype(BF16)
    o = jnp.dot(a, v.astype(BF16), preferred_element_type=F32)
    seq_of_row_v = lax.broadcasted_iota(jnp.int32, (rows, GLA_DV), 0) // t
    seq_of_row_k = lax.broadcasted_iota(jnp.int32, (rows, GLA_DK), 0) // t
    for i in range(bb):
        s0 = s0_ref[i]
        mine = seq_of_row_v == i
        o = o + jnp.where(mine, jnp.dot(q_t, s0.astype(BF16), preferred_element_type=F32), 0.0)
        v_i = jnp.where(mine, v, 0.0).astype(BF16)
        ones_i = jnp.where(seq_of_row_k == i, 1.0, 0.0)
        decay = jnp.exp(jnp.dot(la_t, ones_i, precision=HIGHEST, preferred_element_type=F32))
        decay = jnp.concatenate([decay, decay], axis=1)
        sout_ref[i] = decay * s0 + jnp.dot(k_e_t, v_i, preferred_element_type=F32)
    o_ref[...] = o.reshape(bb, t, GLA_DV)


def _gla_sample(q, k, v, la, s0):
    bsz, t, _ = q.shape
    bb = SAMPLE_BATCH_BLOCK
    qk_spec = pl.BlockSpec((bb, t, GLA_DK), lambda i, h: (i, 0, h))
    v_spec = pl.BlockSpec((bb, t, GLA_DV), lambda i, h: (i, 0, h))
    s_spec = pl.BlockSpec((bb, None, GLA_DK, GLA_DV), lambda i, h: (i, h, 0, 0))
    return pl.pallas_call(
        _gla_sample_body,
        grid=(bsz // bb, GLA_HEADS),
        in_specs=[qk_spec, qk_spec, v_spec, qk_spec, s_spec],
        out_specs=[v_spec, s_spec],
        out_shape=[jax.ShapeDtypeStruct((bsz, t, GLA_WIDTH), F32),
                   jax.ShapeDtypeStruct(s0.shape, F32)],
        compiler_params=_params(2),
        name="gla_sample",
    )(q, k, v, la, s0)


def _sink_softmax(s, mask, sink):
    s = jnp.where(mask, s, NEG_INF)
    m = jnp.maximum(jnp.max(s, axis=-1, keepdims=True), sink)
    p = jnp.exp(s - m)
    return p / (jnp.sum(p, axis=-1, keepdims=True) + jnp.exp(sink - m))


def _split_kv_lanes(x):
    lane = lax.broadcasted_iota(jnp.int32, x.shape, 1)
    low = lane < SWA_HEAD_DIM
    lo0 = jnp.where(low, x, 0.0)
    hi1 = jnp.where(low, 0.0, x)
    hi0 = pltpu.roll(lo0, SWA_HEAD_DIM, 1)
    lo1 = pltpu.roll(hi1, SWA_HEAD_DIM, 1)
    return (lo0.astype(BF16), lo1.astype(BF16)), (hi0.astype(BF16), hi1.astype(BF16))


def _swa_prompt_body(sink_ref, q_ref, kc_ref, kp_ref, vc_ref, vp_ref, o_ref, st_scr, p_scr):
    n = pl.program_id(1)
    blk = WINDOW
    k_parts = _split_kv_lanes(jnp.concatenate([kp_ref[...], kc_ref[...]], axis=0))
    v_parts = _split_kv_lanes(jnp.concatenate([vp_ref[...], vc_ref[...]], axis=0))
    kj = lax.broadcasted_iota(jnp.int32, (2 * blk, blk), 0)
    qi = lax.broadcasted_iota(jnp.int32, (2 * blk, blk), 1)
    diff = qi + blk - kj
    mask = (diff >= 0) & (diff < WINDOW) & ((n > 0) | (kj >= blk))
    scale = SWA_HEAD_DIM ** -0.5
    contract_rows = (((0,), (0,)), ((), ()))
    z = lax.shift_right_logical(n, 30)
    for head in range(SWA_HEADS):
        pair, half = divmod(head, 2)
        qp = q_ref[:, pair * LANES:(pair + 1) * LANES].astype(BF16)
        st_scr[z, head] = lax.dot_general(k_parts[half][head // SWA_GROUP], qp, NT_DIMS,
                                          preferred_element_type=F32)
    for head in range(SWA_HEADS):
        sink = sink_ref[head]
        st = jnp.where(mask, st_scr[z, head] * scale, NEG_INF)
        m = jnp.maximum(jnp.max(st, axis=0, keepdims=True), sink)
        p = jnp.exp(st - m)
        denom = jnp.sum(p, axis=0, keepdims=True) + jnp.exp(sink - m)
        p_scr[z, head] = (p * (1.0 / denom)).astype(BF16)
    for pair in range(SWA_HEADS // 2):
        kv = (2 * pair) // SWA_GROUP
        o_ref[:, pair * LANES:(pair + 1) * LANES] = (
            lax.dot_general(p_scr[z, 2 * pair], v_parts[0][kv], contract_rows, preferred_element_type=F32)
            + lax.dot_general(p_scr[z, 2 * pair + 1], v_parts[1][kv], contract_rows,
                              preferred_element_type=F32))


def _swa_prompt(sq, sk, sv, sinks):
    bsz, s, _ = sq.shape
    blk = WINDOW
    cur = lambda b, n: (b, n, 0)
    prev = lambda b, n: (b, jnp.maximum(n - 1, 0), 0)
    kv_cur = pl.BlockSpec((None, blk, LANES), cur)
    kv_prev = pl.BlockSpec((None, blk, LANES), prev)
    q_spec = pl.BlockSpec((None, blk, SWA_WIDTH), cur)
    return pl.pallas_call(
        _swa_prompt_body,
        grid=(bsz, s // blk),
        in_specs=[pl.BlockSpec(memory_space=pltpu.SMEM), q_spec, kv_cur, kv_prev, kv_cur, kv_prev],
        out_specs=q_spec,
        out_shape=jax.ShapeDtypeStruct((bsz, s, SWA_WIDTH), F32),
        scratch_shapes=[pltpu.VMEM((1, SWA_HEADS, 2 * blk, blk), F32),
                        pltpu.VMEM((1, SWA_HEADS, 2 * blk, blk), BF16)],
        compiler_params=_params(2),
        name="swa_prompt",
    )(sinks, sq, sk, sk, sv, sv)


def _swa_sample_body(t_new, sink_ref, q_ref, ck_ref, cv_ref, kn_ref, vn_ref, o_ref, nk_ref, nv_ref):
    bb, rows, _ = q_ref.shape
    w = ck_ref.shape[1]
    pad = jnp.zeros((16 - t_new, LANES), F32)
    n_keys = w + 16
    tq = lax.broadcasted_iota(jnp.int32, (rows, n_keys), 0) // SWA_HEADS
    c = lax.broadcasted_iota(jnp.int32, (rows, n_keys), 1)
    mask = ((c < w) & (c > tq)) | ((c >= w) & (c - w <= tq) & (c < w + t_new))
    sink = sink_ref[...]
    scale = SWA_HEAD_DIM ** -0.5
    for i in range(bb):
        ck, cv, kn, vn = ck_ref[i], cv_ref[i], kn_ref[i], vn_ref[i]
        kcat = jnp.concatenate([ck, kn, pad], axis=0).astype(BF16)
        vcat = jnp.concatenate([cv, vn, pad], axis=0).astype(BF16)
        s = lax.dot_general(q_ref[i].astype(BF16), kcat, NT_DIMS, preferred_element_type=F32) * scale
        p = _sink_softmax(s, mask, sink)
        o_ref[i] = jnp.dot(p.astype(BF16), vcat, preferred_element_type=F32)
        nk_ref[i, 0:w - t_new, :] = ck[t_new:, :]
        nk_ref[i, w - t_new:w, :] = kn
        nv_ref[i, 0:w - t_new, :] = cv[t_new:, :]
        nv_ref[i, w - t_new:w, :] = vn


def _swa_sample(q_pad, ck, cv, kn, vn, sink_col):
    bsz, rows, _ = q_pad.shape
    t_new = kn.shape[1]
    w = ck.shape[1]
    bb = SAMPLE_BATCH_BLOCK
    blk = lambda r: pl.BlockSpec((bb, r, LANES), lambda i: (i, 0, 0))
    return pl.pallas_call(
        functools.partial(_swa_sample_body, t_new),
        grid=(bsz // bb,),
        in_specs=[_resident(sink_col.shape), blk(rows), blk(w), blk(w), blk(t_new), blk(t_new)],
        out_specs=[blk(rows), blk(w), blk(w)],
        out_shape=[jax.ShapeDtypeStruct((bsz, rows, LANES), F32),
                   jax.ShapeDtypeStruct(ck.shape, F32), jax.ShapeDtypeStruct(cv.shape, F32)],
        compiler_params=_params(1),
        name="swa_sample",
    )(sink_col, q_pad, ck, cv, kn, vn)


def _merge_body(og_ref, r_ref, os_ref, x_ref, gn_ref, wog_ref, wos_ref, gf_ref, wq_ref, keys_ref,
                h_ref, hn_ref, st_ref):
    og = og_ref[...]
    gn = gn_ref[...]
    heads = [_rms(og[:, i * GLA_DV:(i + 1) * GLA_DV], gn) for i in range(GLA_HEADS)]
    r = r_ref[...]
    gated = jnp.concatenate(heads, axis=1) * (r / (1.0 + jnp.exp(-r)))
    h = (x_ref[...]
         + jnp.dot(gated.astype(BF16), wog_ref[...], preferred_element_type=F32)
         + jnp.dot(os_ref[...].astype(BF16), wos_ref[...], preferred_element_type=F32))
    h_ref[...] = h
    hn_f32 = _rms(h, gf_ref[...])
    hn = hn_f32.astype(BF16)
    hn_ref[...] = hn_f32.T.astype(BF16)
    pq = jnp.dot(hn, wq_ref[...], preferred_element_type=F32).astype(BF16)
    for i in range(PEER_HEADS):
        st_ref[i] = lax.dot_general(keys_ref[i], pq[:, i * PEER_QDIM:(i + 1) * PEER_QDIM], NT_DIMS,
                                    preferred_element_type=F32)


def _merge(o_gla, r, o_swa, x, gn, wog, wos, gf, wq, keys):
    t, d = x.shape
    tm = PROJ_ROWS
    row = lambda i: (i, 0)
    return pl.pallas_call(
        _merge_body,
        grid=(t // tm,),
        in_specs=[pl.BlockSpec((tm, GLA_WIDTH), row), pl.BlockSpec((tm, GLA_WIDTH), row),
                  pl.BlockSpec((tm, o_swa.shape[1]), row), pl.BlockSpec((tm, d), row),
                  _resident(gn.shape), _resident(wog.shape), _resident(wos.shape),
                  _resident(gf.shape), _resident(wq.shape), _resident(keys.shape)],
        out_specs=[pl.BlockSpec((tm, d), row), pl.BlockSpec((d, tm), lambda i: (0, i)),
                   pl.BlockSpec((PEER_HEADS, 2 * PEER_N_KEYS, tm), lambda i: (0, 0, i))],
        out_shape=[jax.ShapeDtypeStruct((t, d), F32), jax.ShapeDtypeStruct((d, t), BF16),
                   jax.ShapeDtypeStruct((PEER_HEADS, 2 * PEER_N_KEYS, t), F32)],
        compiler_params=_params(1),
        name="merge",
    )(o_gla, r, o_swa, x, gn, wog, wos, gf, wq, keys)


N_SORTED = PEER_TOPK + 1


def _sorting_network(n):
    pairs = []
    p = 1
    while p < n:
        k = p
        while k >= 1:
            for j in range(k % p, n - k, 2 * k):
                for i in range(min(k, n - j - k)):
                    if (i + j) // (2 * p) == (i + j + k) // (2 * p):
                        pairs.append((i + j, i + j + k))
            k //= 2
        p *= 2
    return pairs


def _topk_body(st_ref, aux_ref, top0_scr, top1_scr):
    scr = (top0_scr, top1_scr)
    sublanes = 8
    n_tiles = PEER_N_KEYS // sublanes
    network = _sorting_network(n_tiles)
    for h in range(PEER_HEADS):
        for half in range(2):
            base = half * PEER_N_KEYS
            col = [st_ref[h, base + k * sublanes:base + (k + 1) * sublanes, :] for k in range(n_tiles)]
            for i, j in network:
                col[i], col[j] = jnp.maximum(col[i], col[j]), jnp.minimum(col[i], col[j])
            for a in range(N_SORTED):
                m = col[0]
                for shift in (4, 2, 1):
                    m = jnp.maximum(m, pltpu.roll(m, shift, 0))
                scr[half][a, h:h + 1, :] = m[0:1, :]
                if a + 1 < N_SORTED:
                    hit = col[0] == m
                    for k in range(min(n_tiles, N_SORTED - 1 - a)):
                        below = col[k + 1] if k + 1 < n_tiles else NEG_INF
                        col[k] = jnp.where(hit, below, col[k])
    top0 = [top0_scr[a] for a in range(N_SORTED)]
    top1 = [top1_scr[a] for a in range(N_SORTED)]
    cands = [top0[a - 1] + top1[b - 1]
             for a in range(1, N_SORTED + 1) for b in range(1, N_SORTED // a + 1)]
    best = None
    z = None
    for rank in range(N_SORTED):
        m = functools.reduce(jnp.maximum, cands)
        if rank == 0:
            best = m
            z = jnp.ones_like(m)
        elif rank < PEER_TOPK:
            z = z + jnp.exp(m - best)
        if rank == PEER_TOPK - 1:
            kth = m
        if rank + 1 < N_SORTED:
            cands = [jnp.where(cnd == m, NEG_INF, cnd) for cnd in cands]
    aux_ref[0] = 0.5 * (kth + m)
    aux_ref[1] = top0[0]
    aux_ref[2] = top1[0]
    aux_ref[3] = 1.0 / z


def _topk(st):
    t = st.shape[2]
    tt = TOPK_TOKENS
    return pl.pallas_call(
        _topk_body,
        grid=(t // tt,),
        in_specs=[pl.BlockSpec((PEER_HEADS, 2 * PEER_N_KEYS, tt), lambda i: (0, 0, i))],
        out_specs=pl.BlockSpec((4, PEER_HEADS, tt), lambda i: (0, 0, i)),
        out_shape=jax.ShapeDtypeStruct((4, PEER_HEADS, t), F32),
        scratch_shapes=[pltpu.VMEM((N_SORTED, PEER_HEADS, tt), F32),
                        pltpu.VMEM((N_SORTED, PEER_HEADS, tt), F32)],
        compiler_params=_params(1),
        name="peer_topk",
    )(st)


def _peer_body(st_ref, aux_ref, hnt_ref, u_ref, vt_ref, out_ref,
               thr_scr, w0_scr, e1_scr, thr_cur, w0_cur, acc_scr, *piece_scr):
    ec = pl.program_id(1)
    n_rows = PEER_CHUNK // PEER_N_KEYS
    tb = hnt_ref.shape[1]
    a_scr = piece_scr[:PEER_PARTS]
    c_scr = piece_scr[PEER_PARTS:]

    @pl.when(ec == 0)
    def _():
        grid_rows = (PEER_N_KEYS // n_rows, n_rows, tb)
        for h in range(PEER_HEADS):
            s0 = st_ref[h, 0:PEER_N_KEYS, :]
            s1 = st_ref[h, PEER_N_KEYS:2 * PEER_N_KEYS, :]
            thr_scr[h] = jnp.exp(aux_ref[0, h:h + 1, :] - s0 - aux_ref[2, h:h + 1, :]).reshape(grid_rows)
            w0_scr[h] = (jnp.exp(s0 - aux_ref[1, h:h + 1, :]) * aux_ref[3, h:h + 1, :]).reshape(grid_rows)
            e1_scr[h] = jnp.exp(s1 - aux_ref[2, h:h + 1, :])
        acc_scr[...] = jnp.zeros_like(acc_scr)

    sub = 2 * BF16_SUBLANES
    inv_sqrt2 = 1.0 / math.sqrt(2.0)
    e_part = PEER_CHUNK // PEER_PARTS
    rows_per_part = n_rows // PEER_PARTS

    z = lax.shift_right_logical(ec, 30)

    for h in range(PEER_HEADS):
        thr_cur[h] = thr_scr[h, ec]
        w0_cur[h] = w0_scr[h, ec]

    def scores(p):
        a_scr[p][z] = jnp.dot(u_ref[p * e_part:(p + 1) * e_part, :], hnt_ref[...],
                              preferred_element_type=F32)

    def coefficients(p):
        for r in range(rows_per_part):
            ii = p * rows_per_part + r
            for lt in range(tb // LANES):
                ls = slice(lt * LANES, (lt + 1) * LANES)
                thr_b = [thr_cur[h, ii:ii + 1, ls] for h in range(PEER_HEADS)]
                w0_b = [w0_cur[h, ii:ii + 1, ls] for h in range(PEER_HEADS)]
                for jb in range(PEER_N_KEYS // sub):
                    rows = slice(r * PEER_N_KEYS + jb * sub, r * PEER_N_KEYS + (jb + 1) * sub)
                    a = a_scr[p][z, rows, ls]
                    act = 0.5 * a * (1.0 + lax.erf(a * inv_sqrt2))
                    g = None
                    for h in range(PEER_HEADS):
                        e1 = e1_scr[h, jb * sub:(jb + 1) * sub, ls]
                        term = jnp.where(e1 >= thr_b[h], e1, 0.0) * w0_b[h]
                        g = term if g is None else g + term
                    c_scr[p][z, rows, ls] = (g * act).astype(BF16)

    def values(p):
        acc_scr[z] += jnp.dot(vt_ref[p], c_scr[p][z], preferred_element_type=F32)

    for p in range(PEER_PARTS):
        scores(p)
    for p in range(PEER_PARTS):
        coefficients(p)
        values(p)

    @pl.when(ec == pl.num_programs(1) - 1)
    def _():
        out_ref[...] = acc_scr[0].T


def _peer(st, aux, hnt, u, vt):
    d, t = hnt.shape
    tb = PEER_TOKENS
    ec = PEER_CHUNK
    n_chunks = u.shape[0] // ec
    rows = ec // PEER_N_KEYS
    return pl.pallas_call(
        _peer_body,
        grid=(t // tb, n_chunks),
        in_specs=[pl.BlockSpec((PEER_HEADS, 2 * PEER_N_KEYS, tb), lambda i, j: (0, 0, i),
                               pipeline_mode=pl.Buffered(1)),
                  pl.BlockSpec((4, PEER_HEADS, tb), lambda i, j: (0, 0, i)),
                  pl.BlockSpec((d, tb), lambda i, j: (0, i), pipeline_mode=pl.Buffered(1)),
                  pl.BlockSpec((ec, d), lambda i, j: (j, 0)),
                  pl.BlockSpec((PEER_PARTS, d, ec // PEER_PARTS), lambda i, j: (j, 0, 0))],
        out_specs=pl.BlockSpec((tb, d), lambda i, j: (i, 0)),
        out_shape=jax.ShapeDtypeStruct((t, d), F32),
        scratch_shapes=([pltpu.VMEM((PEER_HEADS, n_chunks, rows, tb), F32),
                         pltpu.VMEM((PEER_HEADS, n_chunks, rows, tb), F32),
                         pltpu.VMEM((PEER_HEADS, PEER_N_KEYS, tb), F32),
                         pltpu.VMEM((PEER_HEADS, rows, tb), F32),
                         pltpu.VMEM((PEER_HEADS, rows, tb), F32),
                         pltpu.VMEM((1, d, tb), F32)]
                        + [pltpu.VMEM((1, ec // PEER_PARTS, tb), F32)] * PEER_PARTS
                        + [pltpu.VMEM((1, ec // PEER_PARTS, tb), BF16)] * PEER_PARTS),
        compiler_params=_params(2),
        name="peer_dense",
    )(st, aux, hnt, u, vt)


def _final_body(h_ref, p_ref, g_ref, y_ref):
    y_ref[...] = _rms(h_ref[...] + p_ref[...], g_ref[...])


def _final(h, p, g):
    t, d = h.shape
    tm = PROJ_ROWS
    row = lambda i: (i, 0)
    return pl.pallas_call(
        _final_body,
        grid=(t // tm,),
        in_specs=[pl.BlockSpec((tm, d), row), pl.BlockSpec((tm, d), row), _resident(g.shape)],
        out_specs=pl.BlockSpec((tm, d), row),
        out_shape=jax.ShapeDtypeStruct((t, d), F32),
        compiler_params=_params(1),
        name="final",
    )(h, p, g)


def _rope_tables(pos):
    half = SWA_HEAD_DIM // 2
    inv = jnp.exp(-math.log(ROPE_THETA) * jnp.arange(half, dtype=F32) * (2.0 / SWA_HEAD_DIM))
    ang = pos.astype(F32)[:, None] * inv[None, :]
    cos, sin = jnp.cos(ang), jnp.sin(ang)
    reps = LANES // SWA_HEAD_DIM
    return (jnp.tile(jnp.concatenate([cos, cos], axis=1), (1, reps)),
            jnp.tile(jnp.concatenate([-sin, sin], axis=1), (1, reps)))


def _pad_heads_to_kv_half(w, axis):
    shape = w.shape
    w = w.reshape(shape[:axis] + (SWA_KV_HEADS, SWA_GROUP, SWA_HEAD_DIM) + shape[axis + 1:])
    parts = []
    for kv in range(SWA_KV_HEADS):
        heads = lax.index_in_dim(w, kv, axis, keepdims=False)
        pads = [(0, 0)] * heads.ndim
        pads[axis + 1] = (kv * SWA_HEAD_DIM, (SWA_KV_HEADS - 1 - kv) * SWA_HEAD_DIM)
        parts.append(jnp.pad(heads, pads))
    out = jnp.concatenate(parts, axis=axis)
    return out.reshape(shape[:axis] + (SWA_HEADS * LANES,) + shape[axis + 1:])


def kernel(x_prompt, x_sample, state_gla, cache_win_k, cache_win_v, g_attn, w_in, w_gate_up, b_gate,
           g_gla_out, attn_sinks, w_out, g_ffn, w_peer_q, peer_sub_keys, peer_u, peer_v, g_final):
    bsz, seq, d = x_prompt.shape
    dbsz, dseq, _ = x_sample.shape
    win = cache_win_k.shape[2]

    n_gla = 2 * GLA_HEADS * GLA_DK + 2 * GLA_WIDTH
    o_sq = n_gla + GLA_GATE_RANK
    w_gla = w_in[0, :, :n_gla].astype(BF16)
    ga = jnp.pad(w_in[0, :, n_gla:o_sq], ((0, 0), (0, LANES - GLA_GATE_RANK)))
    wsq = w_in[0, :, o_sq:o_sq + SWA_WIDTH]
    wkv = w_in[0, :, o_sq + SWA_WIDTH:]
    w_swa_prompt = jnp.concatenate([wsq, wkv, ga], axis=1).astype(BF16)
    w_swa_sample = jnp.concatenate([_pad_heads_to_kv_half(wsq, 1), wkv, ga], axis=1).astype(BF16)
    wgu = jnp.pad(w_gate_up[0], ((0, LANES - GLA_GATE_RANK), (0, 0)))
    bg = b_gate[0][None, :]
    g_a = g_attn[0][None, :]
    g_f = g_ffn[0][None, :]
    g_n = g_gla_out[0][None, :]
    wo_gla = w_out[0, :GLA_WIDTH].astype(BF16)
    wo_swa = w_out[0, GLA_WIDTH:]
    wo_swa_pad = _pad_heads_to_kv_half(wo_swa, 0).astype(BF16)
    wo_swa = wo_swa.astype(BF16)
    wq = w_peer_q[0].astype(BF16)
    sk0 = jnp.pad(peer_sub_keys[0, :, 0], ((0, 0), (0, 0), (0, PEER_HALF)))
    sk1 = jnp.pad(peer_sub_keys[0, :, 1], ((0, 0), (0, 0), (PEER_HALF, 0)))
    keys = jnp.concatenate([sk0, sk1], axis=1).astype(BF16)
    u = peer_u[0].astype(BF16)
    e_piece = PEER_CHUNK // PEER_PARTS
    vt = peer_v[0].reshape(PEER_EXPERTS // e_piece, e_piece, d).transpose(0, 2, 1).astype(BF16)
    sinks = attn_sinks[0]
    sink_col = jnp.tile(sinks, dseq)[:, None]

    cos_p, sin_p = _rope_tables(jnp.arange(seq))
    cos_s, sin_s = _rope_tables(PAST_LEN + jnp.arange(dseq))
    cos_s = jnp.tile(cos_s, (PROJ_ROWS // dseq, 1))
    sin_s = jnp.tile(sin_s, (PROJ_ROWS // dseq, 1))

    def ffn(h, hn, st):
        aux = _topk(st)
        return _final(h, _peer(st, aux, hn, u, vt), g_final[None, :])

    tp = bsz * seq
    q, k, v, r, la, sq, sk, sv = _proj(x_prompt.reshape(tp, d), g_a, w_gla, w_swa_prompt, wgu, bg,
                                       cos_p, sin_p)
    b3 = lambda a: a.reshape(bsz, seq, a.shape[-1])
    o_gla, s_prompt = _gla_prompt(b3(q), b3(k), b3(v), b3(la))
    o_swa = _swa_prompt(b3(sq), b3(sk), b3(sv), sinks)
    h, hn, st = _merge(o_gla.reshape(tp, GLA_WIDTH), r, o_swa.reshape(tp, SWA_WIDTH),
                       x_prompt.reshape(tp, d), g_n, wo_gla, wo_swa, g_f, wq, keys)
    y_prompt = ffn(h, hn, st).reshape(bsz, seq, d)
    kv_shape = (1, bsz, WINDOW, SWA_KV_HEADS, SWA_HEAD_DIM)
    k_prompt = b3(sk)[:, seq - WINDOW:].reshape(kv_shape)
    v_prompt = b3(sv)[:, seq - WINDOW:].reshape(kv_shape)

    ts = dbsz * dseq
    q, k, v, r, la, sq, sk, sv = _proj(x_sample.reshape(ts, d), g_a, w_gla, w_swa_sample, wgu, bg,
                                       cos_s, sin_s)
    d3 = lambda a: a.reshape(dbsz, dseq, a.shape[-1])
    o_gla, s_sample = _gla_sample(d3(q), d3(k), d3(v), d3(la), state_gla[0])
    o_swa, k_sample, v_sample = _swa_sample(
        sq.reshape(dbsz, dseq * SWA_HEADS, LANES),
        cache_win_k[0].reshape(dbsz, win, LANES), cache_win_v[0].reshape(dbsz, win, LANES),
        d3(sk), d3(sv), sink_col)
    h, hn, st = _merge(o_gla.reshape(ts, GLA_WIDTH), r, o_swa.reshape(ts, SWA_HEADS * LANES),
                       x_sample.reshape(ts, d), g_n, wo_gla, wo_swa_pad, g_f, wq, keys)
    y_sample = ffn(h, hn, st).reshape(dbsz, dseq, d)
    ckv_shape = (1, dbsz, win, SWA_KV_HEADS, SWA_HEAD_DIM)

    return (y_prompt, y_sample, s_prompt[None], k_prompt, v_prompt, s_sample[None],
            k_sample.reshape(ckv_shape), v_sample.reshape(ckv_shape))
```

```python
import functools
import math

import jax
import jax.numpy as jnp
from jax import lax
from jax.experimental import pallas as pl
from jax.experimental.pallas import tpu as pltpu

F32 = jnp.float32
BF16 = jnp.bfloat16
HIGHEST = lax.Precision.HIGHEST

D_MODEL = 2048
GLA_HEADS = 4
GLA_DK = 128
GLA_DV = 256
GLA_WIDTH = GLA_HEADS * GLA_DV
GLA_GATE_RANK = 16
GLA_TAU = 16.0
GLA_CHUNK = 64
SWA_HEAD_DIM = 64
SWA_HEADS = 16
SWA_KV_HEADS = 2
SWA_GROUP = SWA_HEADS // SWA_KV_HEADS
SWA_WIDTH = SWA_HEADS * SWA_HEAD_DIM
WINDOW = 128
ROPE_THETA = 10000.0
PAST_LEN = 8192
PEER_N_KEYS = 128
PEER_EXPERTS = PEER_N_KEYS * PEER_N_KEYS
PEER_HEADS = 8
PEER_QDIM = 128
PEER_HALF = PEER_QDIM // 2
PEER_TOPK = 16
EPS = 1e-6
NEG_INF = -1e30

LANES = 128
BF16_SUBLANES = 16
PROJ_ROWS = 256
TOPK_TOKENS = 128
PEER_TOKENS = 512
PEER_CHUNK = 1024
PEER_PARTS = 2
SAMPLE_BATCH_BLOCK = 8
VMEM_LIMIT = 60 * 1024 * 1024

NT_DIMS = (((1,), (1,)), ((), ()))


def _resident(shape):
    nd = len(shape)
    return pl.BlockSpec(shape, lambda *_: (0,) * nd, pipeline_mode=pl.Buffered(1))


def _params(n_axes):
    return pltpu.CompilerParams(dimension_semantics=("arbitrary",) * n_axes,
                                vmem_limit_bytes=VMEM_LIMIT)


def _rms(x, g):
    return x * lax.rsqrt(jnp.mean(x * x, axis=-1, keepdims=True) + EPS) * g


def _proj_body(x_ref, g_ref, wg_ref, ws_ref, wgu_ref, bg_ref, cos_ref, sin_ref,
               q_ref, k_ref, v_ref, r_ref, la_ref, sq_ref, sk_ref, sv_ref):
    xn = _rms(x_ref[...], g_ref[...]).astype(BF16)
    wsq = sq_ref.shape[1]

    def proj(w_ref, start, width):
        return jnp.dot(xn, w_ref[:, start:start + width], preferred_element_type=F32)

    nk = GLA_HEADS * GLA_DK
    q_ref[...] = proj(wg_ref, 0, nk) * (GLA_DK ** -0.5)
    k_ref[...] = proj(wg_ref, nk, nk)
    v_ref[...] = proj(wg_ref, 2 * nk, GLA_WIDTH)
    r_ref[...] = proj(wg_ref, 2 * nk + GLA_WIDTH, GLA_WIDTH)

    sq_all = proj(ws_ref, 0, wsq)
    tail = proj(ws_ref, wsq, 3 * LANES)
    ga = tail[:, 2 * LANES:]
    z = jnp.dot(ga, wgu_ref[...], precision=HIGHEST, preferred_element_type=F32) + bg_ref[...]
    la_ref[...] = (jnp.minimum(z, 0.0) - jnp.log1p(jnp.exp(-jnp.abs(z)))) * (1.0 / GLA_TAU)

    cos = cos_ref[...]
    sin = sin_ref[...]
    lane = lax.broadcasted_iota(jnp.int32, cos.shape, 1)
    first_half = (lane & (SWA_HEAD_DIM // 2)) == 0

    def rope(t):
        rot = jnp.where(first_half, pltpu.roll(t, LANES - SWA_HEAD_DIM // 2, 1),
                        pltpu.roll(t, SWA_HEAD_DIM // 2, 1))
        return t * cos + rot * sin

    for c in range(wsq // LANES):
        sq_ref[:, c * LANES:(c + 1) * LANES] = rope(sq_all[:, c * LANES:(c + 1) * LANES])
    sk_ref[...] = rope(tail[:, :LANES])
    sv_ref[...] = tail[:, LANES:2 * LANES]


def _proj(x, g, w_gla, w_swa, wgu, bg, cos, sin):
    t, d = x.shape
    tm = PROJ_ROWS
    n_tab = cos.shape[0] // tm
    row = lambda i: (i, 0)
    tab = lambda i: (i % n_tab, 0)
    wsq = w_swa.shape[1] - 3 * LANES
    nk = GLA_HEADS * GLA_DK
    widths = (nk, nk, GLA_WIDTH, GLA_WIDTH, nk, wsq, LANES, LANES)
    return pl.pallas_call(
        _proj_body,
        grid=(t // tm,),
        in_specs=[pl.BlockSpec((tm, d), row), _resident(g.shape), _resident(w_gla.shape),
                  _resident(w_swa.shape), _resident(wgu.shape), _resident(bg.shape),
                  pl.BlockSpec((tm, LANES), tab), pl.BlockSpec((tm, LANES), tab)],
        out_specs=[pl.BlockSpec((tm, n), row) for n in widths],
        out_shape=[jax.ShapeDtypeStruct((t, n), F32) for n in widths],
        compiler_params=_params(1),
        name="proj",
    )(x, g, w_gla, w_swa, wgu, bg, cos, sin)


def _gla_prompt_body(q_ref, k_ref, v_ref, la_ref, o_ref, sfin_ref,
                     s_scr, qt_scr, ds_scr, dec_scr, b_scr, kt_scr, ket_scr, a_scr):
    n = pl.program_id(0)
    bsz = q_ref.shape[0]
    n_ch = q_ref.shape[1] // GLA_CHUNK

    @pl.when(n == 0)
    def _():
        s_scr[...] = jnp.zeros_like(s_scr)

    c = GLA_CHUNK
    row = lax.broadcasted_iota(jnp.int32, (c, c), 0)
    col = lax.broadcasted_iota(jnp.int32, (c, c), 1)
    causal = row >= col
    tri = causal.astype(F32)
    z = lax.shift_right_logical(n, 30)

    def units():
        for ch in range(n_ch):
            for bi in range(bsz):
                for h in range(GLA_HEADS):
                    yield ((ch * bsz + bi) * GLA_HEADS + h, ch * bsz + bi, bi,
                           slice(ch * c, (ch + 1) * c),
                           slice(h * GLA_DK, (h + 1) * GLA_DK), slice(h * GLA_DV, (h + 1) * GLA_DV))

    for ch in range(n_ch):
        for bi in range(bsz):
            b_scr[z, ch * bsz + bi] = jnp.dot(tri, la_ref[bi, ch * c:(ch + 1) * c, :], precision=HIGHEST,
                                              preferred_element_type=F32)
    for unit, cb, bi, rows, kl, vl in units():
        b = b_scr[z, cb, :, kl]
        b_last = b[c - 1:c, :]
        k = k_ref[bi, rows, kl]
        qt_scr[z, unit] = (q_ref[bi, rows, kl] * jnp.exp(b)).astype(BF16)
        kt_scr[z, unit] = (k * jnp.exp(-b)).astype(BF16)
        ket_scr[z, unit] = (k * jnp.exp(b_last - b)).T.astype(BF16)
        dec_scr[z, unit] = jnp.exp(jnp.broadcast_to(b_last, (GLA_DK, GLA_DK))).T
    for unit, cb, bi, rows, kl, vl in units():
        a = lax.dot_general(qt_scr[z, unit], kt_scr[z, unit], NT_DIMS, preferred_element_type=F32)
        a_scr[z, unit] = jnp.where(causal, a, 0.0).astype(BF16)
    for unit, cb, bi, rows, kl, vl in units():
        vb = v_ref[bi, rows, vl].astype(BF16)
        o_ref[bi, rows, vl] = jnp.dot(a_scr[z, unit], vb, preferred_element_type=F32)
        ds_scr[z, unit] = jnp.dot(ket_scr[z, unit], vb, preferred_element_type=F32)

    for ch in range(n_ch):
        rows = slice(ch * c, (ch + 1) * c)
        for bi in range(bsz):
            for h in range(GLA_HEADS):
                unit = (ch * bsz + bi) * GLA_HEADS + h
                chain = bi * GLA_HEADS + h
                vl = slice(h * GLA_DV, (h + 1) * GLA_DV)
                s = s_scr[chain]
                o_ref[bi, rows, vl] += jnp.dot(qt_scr[z, unit], s.astype(BF16), preferred_element_type=F32)
                decay = dec_scr[z, unit]
                s_scr[chain] = jnp.concatenate([decay, decay], axis=1) * s + ds_scr[z, unit]

    @pl.when(n == pl.num_programs(0) - 1)
    def _():
        sfin_ref[...] = s_scr[...].reshape(sfin_ref.shape)


GLA_CHUNKS_PER_STEP = 4


def _gla_prompt(q, k, v, la):
    bsz, s, _ = q.shape
    rows = GLA_CHUNK * GLA_CHUNKS_PER_STEP
    units = GLA_CHUNKS_PER_STEP * bsz * GLA_HEADS
    qk_spec = pl.BlockSpec((bsz, rows, GLA_HEADS * GLA_DK), lambda n: (0, n, 0))
    v_spec = pl.BlockSpec((bsz, rows, GLA_WIDTH), lambda n: (0, n, 0))
    return pl.pallas_call(
        _gla_prompt_body,
        grid=(s // rows,),
        in_specs=[qk_spec, qk_spec, v_spec, qk_spec],
        out_specs=[v_spec, pl.BlockSpec((bsz, GLA_HEADS, GLA_DK, GLA_DV), lambda n: (0, 0, 0, 0))],
        out_shape=[jax.ShapeDtypeStruct((bsz, s, GLA_WIDTH), F32),
                   jax.ShapeDtypeStruct((bsz, GLA_HEADS, GLA_DK, GLA_DV), F32)],
        scratch_shapes=[pltpu.VMEM((bsz * GLA_HEADS, GLA_DK, GLA_DV), F32),
                        pltpu.VMEM((1, units, GLA_CHUNK, GLA_DK), BF16),
                        pltpu.VMEM((1, units, GLA_DK, GLA_DV), F32),
                        pltpu.VMEM((1, units, GLA_DK, GLA_DK), F32),
                        pltpu.VMEM((1, units // GLA_HEADS, GLA_CHUNK, GLA_HEADS * GLA_DK), F32),
                        pltpu.VMEM((1, units, GLA_CHUNK, GLA_DK), BF16),
                        pltpu.VMEM((1, units, GLA_DK, GLA_CHUNK), BF16),
                        pltpu.VMEM((1, units, GLA_CHUNK, GLA_CHUNK), BF16)],
        compiler_params=_params(1),
        name="gla_prompt",
    )(q, k, v, la)


def _gla_sample_body(q_ref, k_ref, v_ref, la_ref, s0_ref, o_ref, sout_ref, oi_scr, dsum_scr, upd_scr):
    bb, t, _ = q_ref.shape
    rows = bb * t
    q = q_ref[...].reshape(rows, GLA_DK)
    k = k_ref[...].reshape(rows, GLA_DK)
    la = la_ref[...].reshape(rows, GLA_DK)
    v = v_ref[...].reshape(rows, GLA_DV)
    row = lax.broadcasted_iota(jnp.int32, (rows, rows), 0)
    col = lax.broadcasted_iota(jnp.int32, (rows, rows), 1)
    same_seq = (row // t) == (col // t)
    causal = same_seq & (row >= col)
    b = jnp.dot(causal.astype(F32), la, precision=HIGHEST, preferred_element_type=F32)
    b_tot = jnp.dot(same_seq.astype(F32), la, precision=HIGHEST, preferred_element_type=F32)
    q_t = (q * jnp.exp(b)).astype(BF16)
    k_t = (k * jnp.exp(-b)).astype(BF16)
    k_e_t = (k * jnp.exp(b_tot - b)).T.astype(BF16)
    la_t = la.T
    a = lax.dot_general(q_t, k_t, NT_DIMS, preferred_element_type=F32)
    a = jnp.where(causal, a, 0.0).astype(BF16)
    o = jnp.dot(a, v.astype(BF16), preferred_element_type=F32)
    seq_of_row_v = lax.broadcasted_iota(jnp.int32, (rows, GLA_DV), 0) // t
    seq_of_row_k = lax.broadcasted_iota(jnp.int32, (rows, GLA_DK), 0) // t
    z = lax.shift_right_logical(pl.program_id(0), 30)
    for i in range(bb):
        oi_scr[z, i] = jnp.dot(q_t, s0_ref[i].astype(BF16), preferred_element_type=F32)
    for i in range(bb):
        ones_i = jnp.where(seq_of_row_k == i, 1.0, 0.0)
        dsum_scr[z, i] = jnp.dot(la_t, ones_i, precision=HIGHEST, preferred_element_type=F32)
    for i in range(bb):
        v_i = jnp.where(seq_of_row_v == i, v, 0.0).astype(BF16)
        upd_scr[z, i] = jnp.dot(k_e_t, v_i, preferred_element_type=F32)
    for i in range(bb):
        o = o + jnp.where(seq_of_row_v == i, oi_scr[z, i], 0.0)
        decay = jnp.exp(dsum_scr[z, i])
        sout_ref[i] = jnp.concatenate([decay, decay], axis=1) * s0_ref[i] + upd_scr[z, i]
    o_ref[...] = o.reshape(bb, t, GLA_DV)


def _gla_sample(q, k, v, la, s0):
    bsz, t, _ = q.shape
    bb = SAMPLE_BATCH_BLOCK
    qk_spec = pl.BlockSpec((bb, t, GLA_DK), lambda i, h: (i, 0, h))
    v_spec = pl.BlockSpec((bb, t, GLA_DV), lambda i, h: (i, 0, h))
    s_spec = pl.BlockSpec((bb, None, GLA_DK, GLA_DV), lambda i, h: (i, h, 0, 0))
    return pl.pallas_call(
        _gla_sample_body,
        grid=(bsz // bb, GLA_HEADS),
        in_specs=[qk_spec, qk_spec, v_spec, qk_spec, s_spec],
        out_specs=[v_spec, s_spec],
        out_shape=[jax.ShapeDtypeStruct((bsz, t, GLA_WIDTH), F32),
                   jax.ShapeDtypeStruct(s0.shape, F32)],
        scratch_shapes=[pltpu.VMEM((1, bb, bb * t, GLA_DV), F32),
                        pltpu.VMEM((1, bb, GLA_DK, GLA_DK), F32),
                        pltpu.VMEM((1, bb, GLA_DK, GLA_DV), F32)],
        compiler_params=_params(2),
        name="gla_sample",
    )(q, k, v, la, s0)


def _sink_softmax(s, mask, sink):
    s = jnp.where(mask, s, NEG_INF)
    m = jnp.maximum(jnp.max(s, axis=-1, keepdims=True), sink)
    p = jnp.exp(s - m)
    return p / (jnp.sum(p, axis=-1, keepdims=True) + jnp.exp(sink - m))


def _split_kv_lanes(x):
    lane = lax.broadcasted_iota(jnp.int32, x.shape, 1)
    low = lane < SWA_HEAD_DIM
    lo0 = jnp.where(low, x, 0.0)
    hi1 = jnp.where(low, 0.0, x)
    hi0 = pltpu.roll(lo0, SWA_HEAD_DIM, 1)
    lo1 = pltpu.roll(hi1, SWA_HEAD_DIM, 1)
    return (lo0.astype(BF16), lo1.astype(BF16)), (hi0.astype(BF16), hi1.astype(BF16))


def _swa_prompt_body(sink_ref, q_ref, kc_ref, kp_ref, vc_ref, vp_ref, o_ref, st_scr, p_scr):
    n = pl.program_id(1)
    blk = WINDOW
    k_parts = _split_kv_lanes(jnp.concatenate([kp_ref[...], kc_ref[...]], axis=0))
    v_parts = _split_kv_lanes(jnp.concatenate([vp_ref[...], vc_ref[...]], axis=0))
    kj = lax.broadcasted_iota(jnp.int32, (2 * blk, blk), 0)
    qi = lax.broadcasted_iota(jnp.int32, (2 * blk, blk), 1)
    diff = qi + blk - kj
    mask = (diff >= 0) & (diff < WINDOW) & ((n > 0) | (kj >= blk))
    scale = SWA_HEAD_DIM ** -0.5
    contract_rows = (((0,), (0,)), ((), ()))
    z = lax.shift_right_logical(n, 30)
    for head in range(SWA_HEADS):
        pair, half = divmod(head, 2)
        qp = q_ref[:, pair * LANES:(pair + 1) * LANES].astype(BF16)
        st_scr[z, head] = lax.dot_general(k_parts[half][head // SWA_GROUP], qp, NT_DIMS,
                                          preferred_element_type=F32)
    for head in range(SWA_HEADS):
        sink = sink_ref[head]
        st = jnp.where(mask, st_scr[z, head] * scale, NEG_INF)
        m = jnp.maximum(jnp.max(st, axis=0, keepdims=True), sink)
        p = jnp.exp(st - m)
        denom = jnp.sum(p, axis=0, keepdims=True) + jnp.exp(sink - m)
        p_scr[z, head] = (p * (1.0 / denom)).astype(BF16)
    for pair in range(SWA_HEADS // 2):
        kv = (2 * pair) // SWA_GROUP
        o_ref[:, pair * LANES:(pair + 1) * LANES] = (
            lax.dot_general(p_scr[z, 2 * pair], v_parts[0][kv], contract_rows, preferred_element_type=F32)
            + lax.dot_general(p_scr[z, 2 * pair + 1], v_parts[1][kv], contract_rows,
                              preferred_element_type=F32))


def _swa_prompt(sq, sk, sv, sinks):
    bsz, s, _ = sq.shape
    blk = WINDOW
    cur = lambda b, n: (b, n, 0)
    prev = lambda b, n: (b, jnp.maximum(n - 1, 0), 0)
    kv_cur = pl.BlockSpec((None, blk, LANES), cur)
    kv_prev = pl.BlockSpec((None, blk, LANES), prev)
    q_spec = pl.BlockSpec((None, blk, SWA_WIDTH), cur)
    return pl.pallas_call(
        _swa_prompt_body,
        grid=(bsz, s // blk),
        in_specs=[pl.BlockSpec(memory_space=pltpu.SMEM), q_spec, kv_cur, kv_prev, kv_cur, kv_prev],
        out_specs=q_spec,
        out_shape=jax.ShapeDtypeStruct((bsz, s, SWA_WIDTH), F32),
        scratch_shapes=[pltpu.VMEM((1, SWA_HEADS, 2 * blk, blk), F32),
                        pltpu.VMEM((1, SWA_HEADS, 2 * blk, blk), BF16)],
        compiler_params=_params(2),
        name="swa_prompt",
    )(sinks, sq, sk, sk, sv, sv)


def _swa_sample_body(t_new, sink_ref, q_ref, ck_ref, cv_ref, kn_ref, vn_ref, o_ref, nk_ref, nv_ref):
    bb, rows, _ = q_ref.shape
    w = ck_ref.shape[1]
    pad = jnp.zeros((16 - t_new, LANES), F32)
    n_keys = w + 16
    tq = lax.broadcasted_iota(jnp.int32, (rows, n_keys), 0) // SWA_HEADS
    c = lax.broadcasted_iota(jnp.int32, (rows, n_keys), 1)
    mask = ((c < w) & (c > tq)) | ((c >= w) & (c - w <= tq) & (c < w + t_new))
    sink = sink_ref[...]
    scale = SWA_HEAD_DIM ** -0.5
    for i in range(bb):
        ck, cv, kn, vn = ck_ref[i], cv_ref[i], kn_ref[i], vn_ref[i]
        kcat = jnp.concatenate([ck, kn, pad], axis=0).astype(BF16)
        vcat = jnp.concatenate([cv, vn, pad], axis=0).astype(BF16)
        s = lax.dot_general(q_ref[i].astype(BF16), kcat, NT_DIMS, preferred_element_type=F32) * scale
        p = _sink_softmax(s, mask, sink)
        o_ref[i] = jnp.dot(p.astype(BF16), vcat, preferred_element_type=F32)
        nk_ref[i, 0:w - t_new, :] = ck[t_new:, :]
        nk_ref[i, w - t_new:w, :] = kn
        nv_ref[i, 0:w - t_new, :] = cv[t_new:, :]
        nv_ref[i, w - t_new:w, :] = vn


def _swa_sample(q_pad, ck, cv, kn, vn, sink_col):
    bsz, rows, _ = q_pad.shape
    t_new = kn.shape[1]
    w = ck.shape[1]
    bb = SAMPLE_BATCH_BLOCK
    blk = lambda r: pl.BlockSpec((bb, r, LANES), lambda i: (i, 0, 0))
    return pl.pallas_call(
        functools.partial(_swa_sample_body, t_new),
        grid=(bsz // bb,),
        in_specs=[_resident(sink_col.shape), blk(rows), blk(w), blk(w), blk(t_new), blk(t_new)],
        out_specs=[blk(rows), blk(w), blk(w)],
        out_shape=[jax.ShapeDtypeStruct((bsz, rows, LANES), F32),
                   jax.ShapeDtypeStruct(ck.shape, F32), jax.ShapeDtypeStruct(cv.shape, F32)],
        compiler_params=_params(1),
        name="swa_sample",
    )(sink_col, q_pad, ck, cv, kn, vn)


def _merge_body(og_ref, r_ref, os_ref, x_ref, gn_ref, wog_ref, wos_ref, gf_ref, wq_ref, keys_ref,
                h_ref, hn_ref, st_ref):
    og = og_ref[...]
    gn = gn_ref[...]
    heads = [_rms(og[:, i * GLA_DV:(i + 1) * GLA_DV], gn) for i in range(GLA_HEADS)]
    r = r_ref[...]
    gated = jnp.concatenate(heads, axis=1) * (r / (1.0 + jnp.exp(-r)))
    h = (x_ref[...]
         + jnp.dot(gated.astype(BF16), wog_ref[...], preferred_element_type=F32)
         + jnp.dot(os_ref[...].astype(BF16), wos_ref[...], preferred_element_type=F32))
    h_ref[...] = h
    hn_f32 = _rms(h, gf_ref[...])
    hn = hn_f32.astype(BF16)
    hn_ref[...] = hn_f32.T.astype(BF16)
    pq = jnp.dot(hn, wq_ref[...], preferred_element_type=F32).astype(BF16)
    for i in range(PEER_HEADS):
        st_ref[i] = lax.dot_general(keys_ref[i], pq[:, i * PEER_QDIM:(i + 1) * PEER_QDIM], NT_DIMS,
                                    preferred_element_type=F32)


def _merge(o_gla, r, o_swa, x, gn, wog, wos, gf, wq, keys):
    t, d = x.shape
    tm = PROJ_ROWS
    row = lambda i: (i, 0)
    return pl.pallas_call(
        _merge_body,
        grid=(t // tm,),
        in_specs=[pl.BlockSpec((tm, GLA_WIDTH), row), pl.BlockSpec((tm, GLA_WIDTH), row),
                  pl.BlockSpec((tm, o_swa.shape[1]), row), pl.BlockSpec((tm, d), row),
                  _resident(gn.shape), _resident(wog.shape), _resident(wos.shape),
                  _resident(gf.shape), _resident(wq.shape), _resident(keys.shape)],
        out_specs=[pl.BlockSpec((tm, d), row), pl.BlockSpec((d, tm), lambda i: (0, i)),
                   pl.BlockSpec((PEER_HEADS, 2 * PEER_N_KEYS, tm), lambda i: (0, 0, i))],
        out_shape=[jax.ShapeDtypeStruct((t, d), F32), jax.ShapeDtypeStruct((d, t), BF16),
                   jax.ShapeDtypeStruct((PEER_HEADS, 2 * PEER_N_KEYS, t), F32)],
        compiler_params=_params(1),
        name="merge",
    )(o_gla, r, o_swa, x, gn, wog, wos, gf, wq, keys)


N_SORTED = PEER_TOPK + 1


def _sorting_network(n):
    pairs = []
    p = 1
    while p < n:
        k = p
        while k >= 1:
            for j in range(k % p, n - k, 2 * k):
                for i in range(min(k, n - j - k)):
                    if (i + j) // (2 * p) == (i + j + k) // (2 * p):
                        pairs.append((i + j, i + j + k))
            k //= 2
        p *= 2
    return pairs


def _topk_body(st_ref, aux_ref, top0_scr, top1_scr):
    scr = (top0_scr, top1_scr)
    sublanes = 8
    n_tiles = PEER_N_KEYS // sublanes
    network = _sorting_network(n_tiles)
    for h in range(PEER_HEADS):
        for half in range(2):
            base = half * PEER_N_KEYS
            col = [st_ref[h, base + k * sublanes:base + (k + 1) * sublanes, :] for k in range(n_tiles)]
            for i, j in network:
                col[i], col[j] = jnp.maximum(col[i], col[j]), jnp.minimum(col[i], col[j])
            for a in range(N_SORTED):
                m = col[0]
                for shift in (4, 2, 1):
                    m = jnp.maximum(m, pltpu.roll(m, shift, 0))
                scr[half][a, h:h + 1, :] = m[0:1, :]
                if a + 1 < N_SORTED:
                    hit = col[0] == m
                    for k in range(min(n_tiles, N_SORTED - 1 - a)):
                        below = col[k + 1] if k + 1 < n_tiles else NEG_INF
                        col[k] = jnp.where(hit, below, col[k])
    top0 = [top0_scr[a] for a in range(N_SORTED)]
    top1 = [top1_scr[a] for a in range(N_SORTED)]
    cands = [top0[a - 1] + top1[b - 1]
             for a in range(1, N_SORTED + 1) for b in range(1, N_SORTED // a + 1)]
    best = None
    z = None
    for rank in range(N_SORTED):
        m = functools.reduce(jnp.maximum, cands)
        if rank == 0:
            best = m
            z = jnp.ones_like(m)
        elif rank < PEER_TOPK:
            z = z + jnp.exp(m - best)
        if rank == PEER_TOPK - 1:
            kth = m
        if rank + 1 < N_SORTED:
            cands = [jnp.where(cnd == m, NEG_INF, cnd) for cnd in cands]
    aux_ref[0] = 0.5 * (kth + m)
    aux_ref[1] = top0[0]
    aux_ref[2] = top1[0]
    aux_ref[3] = 1.0 / z


def _topk(st):
    t = st.shape[2]
    tt = TOPK_TOKENS
    return pl.pallas_call(
        _topk_body,
        grid=(t // tt,),
        in_specs=[pl.BlockSpec((PEER_HEADS, 2 * PEER_N_KEYS, tt), lambda i: (0, 0, i))],
        out_specs=pl.BlockSpec((4, PEER_HEADS, tt), lambda i: (0, 0, i)),
        out_shape=jax.ShapeDtypeStruct((4, PEER_HEADS, t), F32),
        scratch_shapes=[pltpu.VMEM((N_SORTED, PEER_HEADS, tt), F32),
                        pltpu.VMEM((N_SORTED, PEER_HEADS, tt), F32)],
        compiler_params=_params(1),
        name="peer_topk",
    )(st)


def _peer_body(st_ref, aux_ref, hnt_ref, u_ref, vt_ref, h_ref, gfin_ref, out_ref,
               thr_scr, w0_scr, e1_scr, thr_cur, w0_cur, acc_scr, *piece_scr):
    ec = pl.program_id(1)
    n_rows = PEER_CHUNK // PEER_N_KEYS
    tb = hnt_ref.shape[1]
    a_scr = piece_scr[:PEER_PARTS]
    c_scr = piece_scr[PEER_PARTS:]

    @pl.when(ec == 0)
    def _():
        grid_rows = (PEER_N_KEYS // n_rows, n_rows, tb)
        for h in range(PEER_HEADS):
            s0 = st_ref[h, 0:PEER_N_KEYS, :]
            s1 = st_ref[h, PEER_N_KEYS:2 * PEER_N_KEYS, :]
            thr_scr[h] = jnp.exp(aux_ref[0, h:h + 1, :] - s0 - aux_ref[2, h:h + 1, :]).reshape(grid_rows)
            w0_scr[h] = (jnp.exp(s0 - aux_ref[1, h:h + 1, :]) * aux_ref[3, h:h + 1, :]).reshape(grid_rows)
            e1_scr[h] = jnp.exp(s1 - aux_ref[2, h:h + 1, :])
        acc_scr[...] = jnp.zeros_like(acc_scr)

    sub = 2 * BF16_SUBLANES
    inv_sqrt2 = 1.0 / math.sqrt(2.0)
    e_part = PEER_CHUNK // PEER_PARTS
    rows_per_part = n_rows // PEER_PARTS

    z = lax.shift_right_logical(ec, 30)

    for h in range(PEER_HEADS):
        thr_cur[h] = thr_scr[h, ec]
        w0_cur[h] = w0_scr[h, ec]

    def scores(p):
        a_scr[p][z] = jnp.dot(u_ref[p * e_part:(p + 1) * e_part, :], hnt_ref[...],
                              preferred_element_type=F32)

    def coefficients(p):
        for r in range(rows_per_part):
            ii = p * rows_per_part + r
            for lt in range(tb // LANES):
                ls = slice(lt * LANES, (lt + 1) * LANES)
                thr_b = [thr_cur[h, ii:ii + 1, ls] for h in range(PEER_HEADS)]
                w0_b = [w0_cur[h, ii:ii + 1, ls] for h in range(PEER_HEADS)]
                for jb in range(PEER_N_KEYS // sub):
                    rows = slice(r * PEER_N_KEYS + jb * sub, r * PEER_N_KEYS + (jb + 1) * sub)
                    a = a_scr[p][z, rows, ls]
                    act = 0.5 * a * (1.0 + lax.erf(a * inv_sqrt2))
                    g = None
                    for h in range(PEER_HEADS):
                        e1 = e1_scr[h, jb * sub:(jb + 1) * sub, ls]
                        term = jnp.where(e1 >= thr_b[h], e1, 0.0) * w0_b[h]
                        g = term if g is None else g + term
                    c_scr[p][z, rows, ls] = (g * act).astype(BF16)

    def values(p):
        acc_scr[z] += jnp.dot(vt_ref[p], c_scr[p][z], preferred_element_type=F32)

    for p in range(PEER_PARTS):
        scores(p)
    for p in range(PEER_PARTS):
        coefficients(p)
        values(p)

    @pl.when(ec == pl.num_programs(1) - 1)
    def _():
        out_ref[...] = _rms(h_ref[...] + acc_scr[0].T, gfin_ref[...])


def _peer(st, aux, hnt, u, vt, h, g_final):
    d, t = hnt.shape
    tb = PEER_TOKENS
    ec = PEER_CHUNK
    n_chunks = u.shape[0] // ec
    rows = ec // PEER_N_KEYS
    return pl.pallas_call(
        _peer_body,
        grid=(t // tb, n_chunks),
        in_specs=[pl.BlockSpec((PEER_HEADS, 2 * PEER_N_KEYS, tb), lambda i, j: (0, 0, i),
                               pipeline_mode=pl.Buffered(1)),
                  pl.BlockSpec((4, PEER_HEADS, tb), lambda i, j: (0, 0, i)),
                  pl.BlockSpec((d, tb), lambda i, j: (0, i), pipeline_mode=pl.Buffered(1)),
                  pl.BlockSpec((ec, d), lambda i, j: (j, 0)),
                  pl.BlockSpec((PEER_PARTS, d, ec // PEER_PARTS), lambda i, j: (j, 0, 0)),
                  pl.BlockSpec((tb, d), lambda i, j: (i, 0), pipeline_mode=pl.Buffered(1)),
                  _resident(g_final.shape)],
        out_specs=pl.BlockSpec((tb, d), lambda i, j: (i, 0)),
        out_shape=jax.ShapeDtypeStruct((t, d), F32),
        scratch_shapes=([pltpu.VMEM((PEER_HEADS, n_chunks, rows, tb), F32),
                         pltpu.VMEM((PEER_HEADS, n_chunks, rows, tb), F32),
                         pltpu.VMEM((PEER_HEADS, PEER_N_KEYS, tb), F32),
                         pltpu.VMEM((PEER_HEADS, rows, tb), F32),
                         pltpu.VMEM((PEER_HEADS, rows, tb), F32),
                         pltpu.VMEM((1, d, tb), F32)]
                        + [pltpu.VMEM((1, ec // PEER_PARTS, tb), F32)] * PEER_PARTS
                        + [pltpu.VMEM((1, ec // PEER_PARTS, tb), BF16)] * PEER_PARTS),
        compiler_params=_params(2),
        name="peer_dense",
    )(st, aux, hnt, u, vt, h, g_final)


def _rope_tables(pos):
    half = SWA_HEAD_DIM // 2
    inv = jnp.exp(-math.log(ROPE_THETA) * jnp.arange(half, dtype=F32) * (2.0 / SWA_HEAD_DIM))
    ang = pos.astype(F32)[:, None] * inv[None, :]
    cos, sin = jnp.cos(ang), jnp.sin(ang)
    reps = LANES // SWA_HEAD_DIM
    return (jnp.tile(jnp.concatenate([cos, cos], axis=1), (1, reps)),
            jnp.tile(jnp.concatenate([-sin, sin], axis=1), (1, reps)))


def _pad_heads_to_kv_half(w, axis):
    shape = w.shape
    w = w.reshape(shape[:axis] + (SWA_KV_HEADS, SWA_GROUP, SWA_HEAD_DIM) + shape[axis + 1:])
    parts = []
    for kv in range(SWA_KV_HEADS):
        heads = lax.index_in_dim(w, kv, axis, keepdims=False)
        pads = [(0, 0)] * heads.ndim
        pads[axis + 1] = (kv * SWA_HEAD_DIM, (SWA_KV_HEADS - 1 - kv) * SWA_HEAD_DIM)
        parts.append(jnp.pad(heads, pads))
    out = jnp.concatenate(parts, axis=axis)
    return out.reshape(shape[:axis] + (SWA_HEADS * LANES,) + shape[axis + 1:])


def kernel(x_prompt, x_sample, state_gla, cache_win_k, cache_win_v, g_attn, w_in, w_gate_up, b_gate,
           g_gla_out, attn_sinks, w_out, g_ffn, w_peer_q, peer_sub_keys, peer_u, peer_v, g_final):
    bsz, seq, d = x_prompt.shape
    dbsz, dseq, _ = x_sample.shape
    win = cache_win_k.shape[2]

    n_gla = 2 * GLA_HEADS * GLA_DK + 2 * GLA_WIDTH
    o_sq = n_gla + GLA_GATE_RANK
    w_gla = w_in[0, :, :n_gla].astype(BF16)
    ga = jnp.pad(w_in[0, :, n_gla:o_sq], ((0, 0), (0, LANES - GLA_GATE_RANK)))
    wsq = w_in[0, :, o_sq:o_sq + SWA_WIDTH]
    wkv = w_in[0, :, o_sq + SWA_WIDTH:]
    w_swa_prompt = jnp.concatenate([wsq, wkv, ga], axis=1).astype(BF16)
    w_swa_sample = jnp.concatenate([_pad_heads_to_kv_half(wsq, 1), wkv, ga], axis=1).astype(BF16)
    wgu = jnp.pad(w_gate_up[0], ((0, LANES - GLA_GATE_RANK), (0, 0)))
    bg = b_gate[0][None, :]
    g_a = g_attn[0][None, :]
    g_f = g_ffn[0][None, :]
    g_n = g_gla_out[0][None, :]
    wo_gla = w_out[0, :GLA_WIDTH].astype(BF16)
    wo_swa = w_out[0, GLA_WIDTH:]
    wo_swa_pad = _pad_heads_to_kv_half(wo_swa, 0).astype(BF16)
    wo_swa = wo_swa.astype(BF16)
    wq = w_peer_q[0].astype(BF16)
    sk0 = jnp.pad(peer_sub_keys[0, :, 0], ((0, 0), (0, 0), (0, PEER_HALF)))
    sk1 = jnp.pad(peer_sub_keys[0, :, 1], ((0, 0), (0, 0), (PEER_HALF, 0)))
    keys = jnp.concatenate([sk0, sk1], axis=1).astype(BF16)
    u = peer_u[0].astype(BF16)
    e_piece = PEER_CHUNK // PEER_PARTS
    vt = peer_v[0].reshape(PEER_EXPERTS // e_piece, e_piece, d).transpose(0, 2, 1).astype(BF16)
    sinks = attn_sinks[0]
    sink_col = jnp.tile(sinks, dseq)[:, None]

    cos_p, sin_p = _rope_tables(jnp.arange(seq))
    cos_s, sin_s = _rope_tables(PAST_LEN + jnp.arange(dseq))
    cos_s = jnp.tile(cos_s, (PROJ_ROWS // dseq, 1))
    sin_s = jnp.tile(sin_s, (PROJ_ROWS // dseq, 1))

    def ffn(h, hn, st):
        aux = _topk(st)
        return _peer(st, aux, hn, u, vt, h, g_final[None, :])

    tp = bsz * seq
    q, k, v, r, la, sq, sk, sv = _proj(x_prompt.reshape(tp, d), g_a, w_gla, w_swa_prompt, wgu, bg,
                                       cos_p, sin_p)
    b3 = lambda a: a.reshape(bsz, seq, a.shape[-1])
    o_gla, s_prompt = _gla_prompt(b3(q), b3(k), b3(v), b3(la))
    o_swa = _swa_prompt(b3(sq), b3(sk), b3(sv), sinks)
    h, hn, st = _merge(o_gla.reshape(tp, GLA_WIDTH), r, o_swa.reshape(tp, SWA_WIDTH),
                       x_prompt.reshape(tp, d), g_n, wo_gla, wo_swa, g_f, wq, keys)
    y_prompt = ffn(h, hn, st).reshape(bsz, seq, d)
    kv_shape = (1, bsz, WINDOW, SWA_KV_HEADS, SWA_HEAD_DIM)
    k_prompt = b3(sk)[:, seq - WINDOW:].reshape(kv_shape)
    v_prompt = b3(sv)[:, seq - WINDOW:].reshape(kv_shape)

    ts = dbsz * dseq
    q, k, v, r, la, sq, sk, sv = _proj(x_sample.reshape(ts, d), g_a, w_gla, w_swa_sample, wgu, bg,
                                       cos_s, sin_s)
    d3 = lambda a: a.reshape(dbsz, dseq, a.shape[-1])
    o_gla, s_sample = _gla_sample(d3(q), d3(k), d3(v), d3(la), state_gla[0])
    o_swa, k_sample, v_sample = _swa_sample(
        sq.reshape(dbsz, dseq * SWA_HEADS, LANES),
        cache_win_k[0].reshape(dbsz, win, LANES), cache_win_v[0].reshape(dbsz, win, LANES),
        d3(sk), d3(sv), sink_col)
    h, hn, st = _merge(o_gla.reshape(ts, GLA_WIDTH), r, o_swa.reshape(ts, SWA_HEADS * LANES),
                       x_sample.reshape(ts, d), g_n, wo_gla, wo_swa_pad, g_f, wq, keys)
    y_sample = ffn(h, hn, st).reshape(dbsz, dseq, d)
    ckv_shape = (1, dbsz, win, SWA_KV_HEADS, SWA_HEAD_DIM)

    return (y_prompt, y_sample, s_prompt[None], k_prompt, v_prompt, s_sample[None],
            k_sample.reshape(ckv_shape), v_sample.reshape(ckv_shape))
```

```python
import functools
import math

import jax
import jax.numpy as jnp
from jax import lax
from jax.experimental import pallas as pl
from jax.experimental.pallas import tpu as pltpu

F32 = jnp.float32
BF16 = jnp.bfloat16
HIGHEST = lax.Precision.HIGHEST

D_MODEL = 2048
GLA_HEADS = 4
GLA_DK = 128
GLA_DV = 256
GLA_WIDTH = GLA_HEADS * GLA_DV
GLA_GATE_RANK = 16
GLA_TAU = 16.0
GLA_CHUNK = 64
SWA_HEAD_DIM = 64
SWA_HEADS = 16
SWA_KV_HEADS = 2
SWA_GROUP = SWA_HEADS // SWA_KV_HEADS
SWA_WIDTH = SWA_HEADS * SWA_HEAD_DIM
WINDOW = 128
ROPE_THETA = 10000.0
PAST_LEN = 8192
PEER_N_KEYS = 128
PEER_EXPERTS = PEER_N_KEYS * PEER_N_KEYS
PEER_HEADS = 8
PEER_QDIM = 128
PEER_HALF = PEER_QDIM // 2
PEER_TOPK = 16
EPS = 1e-6
NEG_INF = -1e30

LANES = 128
BF16_SUBLANES = 16
PROJ_ROWS = 256
TOPK_TOKENS = 128
PEER_TOKENS = 512
PEER_CHUNK = 1024
PEER_PARTS = 2
SAMPLE_BATCH_BLOCK = 8
VMEM_LIMIT = 60 * 1024 * 1024

NT_DIMS = (((1,), (1,)), ((), ()))


def _resident(shape):
    nd = len(shape)
    return pl.BlockSpec(shape, lambda *_: (0,) * nd, pipeline_mode=pl.Buffered(1))


def _params(n_axes):
    return pltpu.CompilerParams(dimension_semantics=("arbitrary",) * n_axes,
                                vmem_limit_bytes=VMEM_LIMIT)


def _rms(x, g):
    return x * lax.rsqrt(jnp.mean(x * x, axis=-1, keepdims=True) + EPS) * g


def _proj_body(x_ref, g_ref, wg_ref, ws_ref, wgu_ref, bg_ref, cos_ref, sin_ref,
               q_ref, k_ref, v_ref, r_ref, la_ref, sq_ref, sk_ref, sv_ref):
    xn = _rms(x_ref[...], g_ref[...]).astype(BF16)
    wsq = sq_ref.shape[1]

    def proj(w_ref, start, width):
        return jnp.dot(xn, w_ref[:, start:start + width], preferred_element_type=F32)

    nk = GLA_HEADS * GLA_DK
    q_ref[...] = proj(wg_ref, 0, nk) * (GLA_DK ** -0.5)
    k_ref[...] = proj(wg_ref, nk, nk)
    v_ref[...] = proj(wg_ref, 2 * nk, GLA_WIDTH)
    r_ref[...] = proj(wg_ref, 2 * nk + GLA_WIDTH, GLA_WIDTH)

    sq_all = proj(ws_ref, 0, wsq)
    tail = proj(ws_ref, wsq, 3 * LANES)
    ga = tail[:, 2 * LANES:]
    z = jnp.dot(ga, wgu_ref[...], precision=HIGHEST, preferred_element_type=F32) + bg_ref[...]
    la_ref[...] = (jnp.minimum(z, 0.0) - jnp.log1p(jnp.exp(-jnp.abs(z)))) * (1.0 / GLA_TAU)

    cos = cos_ref[...]
    sin = sin_ref[...]
    lane = lax.broadcasted_iota(jnp.int32, cos.shape, 1)
    first_half = (lane & (SWA_HEAD_DIM // 2)) == 0

    def rope(t):
        rot = jnp.where(first_half, pltpu.roll(t, LANES - SWA_HEAD_DIM // 2, 1),
                        pltpu.roll(t, SWA_HEAD_DIM // 2, 1))
        return t * cos + rot * sin

    for c in range(wsq // LANES):
        sq_ref[:, c * LANES:(c + 1) * LANES] = rope(sq_all[:, c * LANES:(c + 1) * LANES])
    sk_ref[...] = rope(tail[:, :LANES])
    sv_ref[...] = tail[:, LANES:2 * LANES]


def _proj(x, g, w_gla, w_swa, wgu, bg, cos, sin):
    t, d = x.shape
    tm = PROJ_ROWS
    n_tab = cos.shape[0] // tm
    row = lambda i: (i, 0)
    tab = lambda i: (i % n_tab, 0)
    wsq = w_swa.shape[1] - 3 * LANES
    nk = GLA_HEADS * GLA_DK
    widths = (nk, nk, GLA_WIDTH, GLA_WIDTH, nk, wsq, LANES, LANES)
    return pl.pallas_call(
        _proj_body,
        grid=(t // tm,),
        in_specs=[pl.BlockSpec((tm, d), row), _resident(g.shape), _resident(w_gla.shape),
                  _resident(w_swa.shape), _resident(wgu.shape), _resident(bg.shape),
                  pl.BlockSpec((tm, LANES), tab), pl.BlockSpec((tm, LANES), tab)],
        out_specs=[pl.BlockSpec((tm, n), row) for n in widths],
        out_shape=[jax.ShapeDtypeStruct((t, n), F32) for n in widths],
        compiler_params=_params(1),
        name="proj",
    )(x, g, w_gla, w_swa, wgu, bg, cos, sin)


def _gla_prompt_body(q_ref, k_ref, v_ref, la_ref, o_ref, sfin_ref,
                     s_scr, qt_scr, ds_scr, dec_scr, b_scr, kt_scr, ket_scr, a_scr):
    n = pl.program_id(0)
    bsz = q_ref.shape[0]
    n_ch = q_ref.shape[1] // GLA_CHUNK

    @pl.when(n == 0)
    def _():
        s_scr[...] = jnp.zeros_like(s_scr)

    c = GLA_CHUNK
    row = lax.broadcasted_iota(jnp.int32, (c, c), 0)
    col = lax.broadcasted_iota(jnp.int32, (c, c), 1)
    causal = row >= col
    tri = causal.astype(F32)
    z = lax.shift_right_logical(n, 30)

    def units():
        for ch in range(n_ch):
            for bi in range(bsz):
                for h in range(GLA_HEADS):
                    yield ((ch * bsz + bi) * GLA_HEADS + h, ch * bsz + bi, bi,
                           slice(ch * c, (ch + 1) * c),
                           slice(h * GLA_DK, (h + 1) * GLA_DK), slice(h * GLA_DV, (h + 1) * GLA_DV))

    for ch in range(n_ch):
        for bi in range(bsz):
            b_scr[z, ch * bsz + bi] = jnp.dot(tri, la_ref[bi, ch * c:(ch + 1) * c, :], precision=HIGHEST,
                                              preferred_element_type=F32)
    for unit, cb, bi, rows, kl, vl in units():
        b = b_scr[z, cb, :, kl]
        b_last = b[c - 1:c, :]
        k = k_ref[bi, rows, kl]
        qt_scr[z, unit] = (q_ref[bi, rows, kl] * jnp.exp(b)).astype(BF16)
        kt_scr[z, unit] = (k * jnp.exp(-b)).astype(BF16)
        ket_scr[z, unit] = (k * jnp.exp(b_last - b)).T.astype(BF16)
        dec_scr[z, unit] = jnp.exp(jnp.broadcast_to(b_last, (GLA_DK, GLA_DK))).T
    for unit, cb, bi, rows, kl, vl in units():
        a = lax.dot_general(qt_scr[z, unit], kt_scr[z, unit], NT_DIMS, preferred_element_type=F32)
        a_scr[z, unit] = jnp.where(causal, a, 0.0).astype(BF16)
    for unit, cb, bi, rows, kl, vl in units():
        vb = v_ref[bi, rows, vl].astype(BF16)
        o_ref[bi, rows, vl] = jnp.dot(a_scr[z, unit], vb, preferred_element_type=F32)
        ds_scr[z, unit] = jnp.dot(ket_scr[z, unit], vb, preferred_element_type=F32)

    for ch in range(n_ch):
        rows = slice(ch * c, (ch + 1) * c)
        for bi in range(bsz):
            for h in range(GLA_HEADS):
                unit = (ch * bsz + bi) * GLA_HEADS + h
                chain = bi * GLA_HEADS + h
                vl = slice(h * GLA_DV, (h + 1) * GLA_DV)
                s = s_scr[chain]
                o_ref[bi, rows, vl] += jnp.dot(qt_scr[z, unit], s.astype(BF16), preferred_element_type=F32)
                decay = dec_scr[z, unit]
                s_scr[chain] = jnp.concatenate([decay, decay], axis=1) * s + ds_scr[z, unit]

    @pl.when(n == pl.num_programs(0) - 1)
    def _():
        sfin_ref[...] = s_scr[...].reshape(sfin_ref.shape)


GLA_CHUNKS_PER_STEP = 4


def _gla_prompt(q, k, v, la):
    bsz, s, _ = q.shape
    rows = GLA_CHUNK * GLA_CHUNKS_PER_STEP
    units = GLA_CHUNKS_PER_STEP * bsz * GLA_HEADS
    qk_spec = pl.BlockSpec((bsz, rows, GLA_HEADS * GLA_DK), lambda n: (0, n, 0))
    v_spec = pl.BlockSpec((bsz, rows, GLA_WIDTH), lambda n: (0, n, 0))
    return pl.pallas_call(
        _gla_prompt_body,
        grid=(s // rows,),
        in_specs=[qk_spec, qk_spec, v_spec, qk_spec],
        out_specs=[v_spec, pl.BlockSpec((bsz, GLA_HEADS, GLA_DK, GLA_DV), lambda n: (0, 0, 0, 0))],
        out_shape=[jax.ShapeDtypeStruct((bsz, s, GLA_WIDTH), F32),
                   jax.ShapeDtypeStruct((bsz, GLA_HEADS, GLA_DK, GLA_DV), F32)],
        scratch_shapes=[pltpu.VMEM((bsz * GLA_HEADS, GLA_DK, GLA_DV), F32),
                        pltpu.VMEM((1, units, GLA_CHUNK, GLA_DK), BF16),
                        pltpu.VMEM((1, units, GLA_DK, GLA_DV), F32),
                        pltpu.VMEM((1, units, GLA_DK, GLA_DK), F32),
                        pltpu.VMEM((1, units // GLA_HEADS, GLA_CHUNK, GLA_HEADS * GLA_DK), F32),
                        pltpu.VMEM((1, units, GLA_CHUNK, GLA_DK), BF16),
                        pltpu.VMEM((1, units, GLA_DK, GLA_CHUNK), BF16),
                        pltpu.VMEM((1, units, GLA_CHUNK, GLA_CHUNK), BF16)],
        compiler_params=_params(1),
        name="gla_prompt",
    )(q, k, v, la)


def _gla_sample_body(q_ref, k_ref, v_ref, la_ref, s0_ref, o_ref, sout_ref, oi_scr, dsum_scr, upd_scr):
    bb, t, _ = q_ref.shape
    rows = bb * t
    q = q_ref[...].reshape(rows, GLA_DK)
    k = k_ref[...].reshape(rows, GLA_DK)
    la = la_ref[...].reshape(rows, GLA_DK)
    v = v_ref[...].reshape(rows, GLA_DV)
    row = lax.broadcasted_iota(jnp.int32, (rows, rows), 0)
    col = lax.broadcasted_iota(jnp.int32, (rows, rows), 1)
    same_seq = (row // t) == (col // t)
    causal = same_seq & (row >= col)
    b = jnp.dot(causal.astype(F32), la, precision=HIGHEST, preferred_element_type=F32)
    b_tot = jnp.dot(same_seq.astype(F32), la, precision=HIGHEST, preferred_element_type=F32)
    q_t = (q * jnp.exp(b)).astype(BF16)
    k_t = (k * jnp.exp(-b)).astype(BF16)
    k_e_t = (k * jnp.exp(b_tot - b)).T.astype(BF16)
    la_t = la.T
    a = lax.dot_general(q_t, k_t, NT_DIMS, preferred_element_type=F32)
    a = jnp.where(causal, a, 0.0).astype(BF16)
    o = jnp.dot(a, v.astype(BF16), preferred_element_type=F32)
    seq_of_row_v = lax.broadcasted_iota(jnp.int32, (rows, GLA_DV), 0) // t
    seq_of_row_k = lax.broadcasted_iota(jnp.int32, (rows, GLA_DK), 0) // t
    z = lax.shift_right_logical(pl.program_id(0), 30)
    for i in range(bb):
        oi_scr[z, i] = jnp.dot(q_t, s0_ref[i].astype(BF16), preferred_element_type=F32)
    for i in range(bb):
        ones_i = jnp.where(seq_of_row_k == i, 1.0, 0.0)
        dsum_scr[z, i] = jnp.dot(la_t, ones_i, precision=HIGHEST, preferred_element_type=F32)
    for i in range(bb):
        v_i = jnp.where(seq_of_row_v == i, v, 0.0).astype(BF16)
        upd_scr[z, i] = jnp.dot(k_e_t, v_i, preferred_element_type=F32)
    for i in range(bb):
        o = o + jnp.where(seq_of_row_v == i, oi_scr[z, i], 0.0)
        decay = jnp.exp(dsum_scr[z, i])
        sout_ref[i] = jnp.concatenate([decay, decay], axis=1) * s0_ref[i] + upd_scr[z, i]
    o_ref[...] = o.reshape(bb, t, GLA_DV)


def _gla_sample(q, k, v, la, s0):
    bsz, t, _ = q.shape
    bb = SAMPLE_BATCH_BLOCK
    qk_spec = pl.BlockSpec((bb, t, GLA_DK), lambda i, h: (i, 0, h))
    v_spec = pl.BlockSpec((bb, t, GLA_DV), lambda i, h: (i, 0, h))
    s_spec = pl.BlockSpec((bb, None, GLA_DK, GLA_DV), lambda i, h: (i, h, 0, 0))
    return pl.pallas_call(
        _gla_sample_body,
        grid=(bsz // bb, GLA_HEADS),
        in_specs=[qk_spec, qk_spec, v_spec, qk_spec, s_spec],
        out_specs=[v_spec, s_spec],
        out_shape=[jax.ShapeDtypeStruct((bsz, t, GLA_WIDTH), F32),
                   jax.ShapeDtypeStruct(s0.shape, F32)],
        scratch_shapes=[pltpu.VMEM((1, bb, bb * t, GLA_DV), F32),
                        pltpu.VMEM((1, bb, GLA_DK, GLA_DK), F32),
                        pltpu.VMEM((1, bb, GLA_DK, GLA_DV), F32)],
        compiler_params=_params(2),
        name="gla_sample",
    )(q, k, v, la, s0)


def _sink_softmax(s, mask, sink):
    s = jnp.where(mask, s, NEG_INF)
    m = jnp.maximum(jnp.max(s, axis=-1, keepdims=True), sink)
    p = jnp.exp(s - m)
    return p / (jnp.sum(p, axis=-1, keepdims=True) + jnp.exp(sink - m))


def _split_kv_lanes(x):
    lane = lax.broadcasted_iota(jnp.int32, x.shape, 1)
    low = lane < SWA_HEAD_DIM
    lo0 = jnp.where(low, x, 0.0)
    hi1 = jnp.where(low, 0.0, x)
    hi0 = pltpu.roll(lo0, SWA_HEAD_DIM, 1)
    lo1 = pltpu.roll(hi1, SWA_HEAD_DIM, 1)
    return (lo0.astype(BF16), lo1.astype(BF16)), (hi0.astype(BF16), hi1.astype(BF16))


def _swa_prompt_body(sink_ref, q_ref, kc_ref, kp_ref, vc_ref, vp_ref, o_ref, st_scr, p_scr):
    n = pl.program_id(1)
    blk = WINDOW
    k_parts = _split_kv_lanes(jnp.concatenate([kp_ref[...], kc_ref[...]], axis=0))
    v_parts = _split_kv_lanes(jnp.concatenate([vp_ref[...], vc_ref[...]], axis=0))
    kj = lax.broadcasted_iota(jnp.int32, (2 * blk, blk), 0)
    qi = lax.broadcasted_iota(jnp.int32, (2 * blk, blk), 1)
    diff = qi + blk - kj
    mask = (diff >= 0) & (diff < WINDOW) & ((n > 0) | (kj >= blk))
    scale = SWA_HEAD_DIM ** -0.5
    contract_rows = (((0,), (0,)), ((), ()))
    z = lax.shift_right_logical(n, 30)
    for head in range(SWA_HEADS):
        pair, half = divmod(head, 2)
        qp = q_ref[:, pair * LANES:(pair + 1) * LANES].astype(BF16)
        st_scr[z, head] = lax.dot_general(k_parts[half][head // SWA_GROUP], qp, NT_DIMS,
                                          preferred_element_type=F32)
    for head in range(SWA_HEADS):
        sink = sink_ref[head]
        st = jnp.where(mask, st_scr[z, head] * scale, NEG_INF)
        m = jnp.maximum(jnp.max(st, axis=0, keepdims=True), sink)
        p = jnp.exp(st - m)
        denom = jnp.sum(p, axis=0, keepdims=True) + jnp.exp(sink - m)
        p_scr[z, head] = (p * (1.0 / denom)).astype(BF16)
    for pair in range(SWA_HEADS // 2):
        kv = (2 * pair) // SWA_GROUP
        o_ref[:, pair * LANES:(pair + 1) * LANES] = (
            lax.dot_general(p_scr[z, 2 * pair], v_parts[0][kv], contract_rows, preferred_element_type=F32)
            + lax.dot_general(p_scr[z, 2 * pair + 1], v_parts[1][kv], contract_rows,
                              preferred_element_type=F32))


def _swa_prompt(sq, sk, sv, sinks):
    bsz, s, _ = sq.shape
    blk = WINDOW
    cur = lambda b, n: (b, n, 0)
    prev = lambda b, n: (b, jnp.maximum(n - 1, 0), 0)
    kv_cur = pl.BlockSpec((None, blk, LANES), cur)
    kv_prev = pl.BlockSpec((None, blk, LANES), prev)
    q_spec = pl.BlockSpec((None, blk, SWA_WIDTH), cur)
    return pl.pallas_call(
        _swa_prompt_body,
        grid=(bsz, s // blk),
        in_specs=[pl.BlockSpec(memory_space=pltpu.SMEM), q_spec, kv_cur, kv_prev, kv_cur, kv_prev],
        out_specs=q_spec,
        out_shape=jax.ShapeDtypeStruct((bsz, s, SWA_WIDTH), F32),
        scratch_shapes=[pltpu.VMEM((1, SWA_HEADS, 2 * blk, blk), F32),
                        pltpu.VMEM((1, SWA_HEADS, 2 * blk, blk), BF16)],
        compiler_params=_params(2),
        name="swa_prompt",
    )(sinks, sq, sk, sk, sv, sv)


def _swa_sample_body(t_new, sink_ref, q_ref, ck_ref, cv_ref, kn_ref, vn_ref, o_ref, nk_ref, nv_ref,
                     st_scr, p_scr):
    bb, rows, _ = q_ref.shape
    w = ck_ref.shape[1]
    pad = jnp.zeros((BF16_SUBLANES - t_new, LANES), F32)
    n_keys = w + BF16_SUBLANES
    c = lax.broadcasted_iota(jnp.int32, (n_keys, rows), 0)
    tq = lax.broadcasted_iota(jnp.int32, (n_keys, rows), 1) // SWA_HEADS
    mask = ((c < w) & (c > tq)) | ((c >= w) & (c - w <= tq) & (c < w + t_new))
    sink = sink_ref[...]
    scale = SWA_HEAD_DIM ** -0.5
    contract_rows = (((0,), (0,)), ((), ()))
    z = lax.shift_right_logical(pl.program_id(0), 30)
    for i in range(bb):
        kcat = jnp.concatenate([ck_ref[i], kn_ref[i], pad], axis=0).astype(BF16)
        st_scr[z, i] = lax.dot_general(kcat, q_ref[i].astype(BF16), NT_DIMS, preferred_element_type=F32)
    for i in range(bb):
        st = jnp.where(mask, st_scr[z, i] * scale, NEG_INF)
        m = jnp.maximum(jnp.max(st, axis=0, keepdims=True), sink)
        p = jnp.exp(st - m)
        denom = jnp.sum(p, axis=0, keepdims=True) + jnp.exp(sink - m)
        p_scr[z, i] = (p * (1.0 / denom)).astype(BF16)
    for i in range(bb):
        vcat = jnp.concatenate([cv_ref[i], vn_ref[i], pad], axis=0).astype(BF16)
        o_ref[i] = lax.dot_general(p_scr[z, i], vcat, contract_rows, preferred_element_type=F32)
        nk_ref[i, 0:w - t_new, :] = ck_ref[i, t_new:, :]
        nk_ref[i, w - t_new:w, :] = kn_ref[i]
        nv_ref[i, 0:w - t_new, :] = cv_ref[i, t_new:, :]
        nv_ref[i, w - t_new:w, :] = vn_ref[i]


def _swa_sample(q_pad, ck, cv, kn, vn, sink_row):
    bsz, rows, _ = q_pad.shape
    t_new = kn.shape[1]
    w = ck.shape[1]
    bb = SAMPLE_BATCH_BLOCK
    n_keys = w + BF16_SUBLANES
    blk = lambda r: pl.BlockSpec((bb, r, LANES), lambda i: (i, 0, 0))
    return pl.pallas_call(
        functools.partial(_swa_sample_body, t_new),
        grid=(bsz // bb,),
        in_specs=[_resident(sink_row.shape), blk(rows), blk(w), blk(w), blk(t_new), blk(t_new)],
        out_specs=[blk(rows), blk(w), blk(w)],
        out_shape=[jax.ShapeDtypeStruct((bsz, rows, LANES), F32),
                   jax.ShapeDtypeStruct(ck.shape, F32), jax.ShapeDtypeStruct(cv.shape, F32)],
        scratch_shapes=[pltpu.VMEM((1, bb, n_keys, rows), F32),
                        pltpu.VMEM((1, bb, n_keys, rows), BF16)],
        compiler_params=_params(1),
        name="swa_sample",
    )(sink_row, q_pad, ck, cv, kn, vn)


def _merge_body(og_ref, r_ref, os_ref, x_ref, gn_ref, wog_ref, wos_ref, gf_ref, wq_ref, keys_ref,
                h_ref, hn_ref, st_ref):
    og = og_ref[...]
    gn = gn_ref[...]
    heads = [_rms(og[:, i * GLA_DV:(i + 1) * GLA_DV], gn) for i in range(GLA_HEADS)]
    r = r_ref[...]
    gated = jnp.concatenate(heads, axis=1) * (r / (1.0 + jnp.exp(-r)))
    h = (x_ref[...]
         + jnp.dot(gated.astype(BF16), wog_ref[...], preferred_element_type=F32)
         + jnp.dot(os_ref[...].astype(BF16), wos_ref[...], preferred_element_type=F32))
    h_ref[...] = h
    hn_f32 = _rms(h, gf_ref[...])
    hn = hn_f32.astype(BF16)
    hn_ref[...] = hn_f32.T.astype(BF16)
    pq = jnp.dot(hn, wq_ref[...], preferred_element_type=F32).astype(BF16)
    for i in range(PEER_HEADS):
        st_ref[i] = lax.dot_general(keys_ref[i], pq[:, i * PEER_QDIM:(i + 1) * PEER_QDIM], NT_DIMS,
                                    preferred_element_type=F32)


def _merge(o_gla, r, o_swa, x, gn, wog, wos, gf, wq, keys):
    t, d = x.shape
    tm = PROJ_ROWS
    row = lambda i: (i, 0)
    return pl.pallas_call(
        _merge_body,
        grid=(t // tm,),
        in_specs=[pl.BlockSpec((tm, GLA_WIDTH), row), pl.BlockSpec((tm, GLA_WIDTH), row),
                  pl.BlockSpec((tm, o_swa.shape[1]), row), pl.BlockSpec((tm, d), row),
                  _resident(gn.shape), _resident(wog.shape), _resident(wos.shape),
                  _resident(gf.shape), _resident(wq.shape), _resident(keys.shape)],
        out_specs=[pl.BlockSpec((tm, d), row), pl.BlockSpec((d, tm), lambda i: (0, i)),
                   pl.BlockSpec((PEER_HEADS, 2 * PEER_N_KEYS, tm), lambda i: (0, 0, i))],
        out_shape=[jax.ShapeDtypeStruct((t, d), F32), jax.ShapeDtypeStruct((d, t), BF16),
                   jax.ShapeDtypeStruct((PEER_HEADS, 2 * PEER_N_KEYS, t), F32)],
        compiler_params=_params(1),
        name="merge",
    )(o_gla, r, o_swa, x, gn, wog, wos, gf, wq, keys)


N_SORTED = PEER_TOPK + 1


def _sorting_network(n):
    pairs = []
    p = 1
    while p < n:
        k = p
        while k >= 1:
            for j in range(k % p, n - k, 2 * k):
                for i in range(min(k, n - j - k)):
                    if (i + j) // (2 * p) == (i + j + k) // (2 * p):
                        pairs.append((i + j, i + j + k))
            k //= 2
        p *= 2
    return pairs


def _largest(vals, n_out):
    n = 1
    while n < len(vals):
        n *= 2
    wires = list(vals) + [None] * (n - len(vals))
    needed = set(range(n_out))
    kept = []
    for i, j in reversed(_sorting_network(n)):
        if i in needed or j in needed:
            needed.update((i, j))
            kept.append((i, j))
    for i, j in reversed(kept):
        hi, lo = wires[i], wires[j]
        if lo is None:
            continue
        if hi is None:
            wires[i], wires[j] = lo, None
        else:
            wires[i], wires[j] = jnp.maximum(hi, lo), jnp.minimum(hi, lo)
    return wires[:n_out]


def _topk_body(st_ref, aux_ref, top0_scr, top1_scr):
    scr = (top0_scr, top1_scr)
    sublanes = 8
    n_tiles = PEER_N_KEYS // sublanes
    network = _sorting_network(n_tiles)
    for h in range(PEER_HEADS):
        for half in range(2):
            base = half * PEER_N_KEYS
            col = [st_ref[h, base + k * sublanes:base + (k + 1) * sublanes, :] for k in range(n_tiles)]
            for i, j in network:
                col[i], col[j] = jnp.maximum(col[i], col[j]), jnp.minimum(col[i], col[j])
            for a in range(N_SORTED):
                m = col[0]
                for shift in (4, 2, 1):
                    m = jnp.maximum(m, pltpu.roll(m, shift, 0))
                scr[half][a, h:h + 1, :] = m[0:1, :]
                if a + 1 < N_SORTED:
                    hit = col[0] == m
                    for k in range(min(n_tiles, N_SORTED - 1 - a)):
                        below = col[k + 1] if k + 1 < n_tiles else NEG_INF
                        col[k] = jnp.where(hit, below, col[k])
    top0 = [top0_scr[a] for a in range(N_SORTED)]
    top1 = [top1_scr[a] for a in range(N_SORTED)]
    cands = [top0[a - 1] + top1[b - 1]
             for a in range(1, N_SORTED + 1) for b in range(1, N_SORTED // a + 1)]
    best = _largest(cands, N_SORTED)
    z = jnp.ones_like(best[0])
    for rank in range(1, PEER_TOPK):
        z = z + jnp.exp(best[rank] - best[0])
    aux_ref[0] = 0.5 * (best[PEER_TOPK - 1] + best[PEER_TOPK])
    aux_ref[1] = top0[0]
    aux_ref[2] = top1[0]
    aux_ref[3] = 1.0 / z


def _topk(st):
    t = st.shape[2]
    tt = TOPK_TOKENS
    return pl.pallas_call(
        _topk_body,
        grid=(t // tt,),
        in_specs=[pl.BlockSpec((PEER_HEADS, 2 * PEER_N_KEYS, tt), lambda i: (0, 0, i))],
        out_specs=pl.BlockSpec((4, PEER_HEADS, tt), lambda i: (0, 0, i)),
        out_shape=jax.ShapeDtypeStruct((4, PEER_HEADS, t), F32),
        scratch_shapes=[pltpu.VMEM((N_SORTED, PEER_HEADS, tt), F32),
                        pltpu.VMEM((N_SORTED, PEER_HEADS, tt), F32)],
        compiler_params=_params(1),
        name="peer_topk",
    )(st)


def _peer_body(st_ref, aux_ref, hnt_ref, u_ref, vt_ref, h_ref, gfin_ref, out_ref,
               thr_scr, w0_scr, e1_scr, thr_cur, w0_cur, acc_scr, *piece_scr):
    ec = pl.program_id(1)
    n_rows = PEER_CHUNK // PEER_N_KEYS
    tb = hnt_ref.shape[1]
    a_scr = piece_scr[:PEER_PARTS]
    c_scr = piece_scr[PEER_PARTS:]

    @pl.when(ec == 0)
    def _():
        grid_rows = (PEER_N_KEYS // n_rows, n_rows, tb)
        for h in range(PEER_HEADS):
            s0 = st_ref[h, 0:PEER_N_KEYS, :]
            s1 = st_ref[h, PEER_N_KEYS:2 * PEER_N_KEYS, :]
            thr_scr[h] = jnp.exp(aux_ref[0, h:h + 1, :] - s0 - aux_ref[2, h:h + 1, :]).reshape(grid_rows)
            w0_scr[h] = (jnp.exp(s0 - aux_ref[1, h:h + 1, :]) * aux_ref[3, h:h + 1, :]).reshape(grid_rows)
            e1_scr[h] = jnp.exp(s1 - aux_ref[2, h:h + 1, :])
        acc_scr[...] = jnp.zeros_like(acc_scr)

    sub = 2 * BF16_SUBLANES
    inv_sqrt2 = 1.0 / math.sqrt(2.0)
    e_part = PEER_CHUNK // PEER_PARTS
    rows_per_part = n_rows // PEER_PARTS

    z = lax.shift_right_logical(ec, 30)

    for h in range(PEER_HEADS):
        thr_cur[h] = thr_scr[h, ec]
        w0_cur[h] = w0_scr[h, ec]

    def scores(p):
        a_scr[p][z] = jnp.dot(u_ref[p * e_part:(p + 1) * e_part, :], hnt_ref[...],
                              preferred_element_type=F32)

    def coefficients(p):
        for r in range(rows_per_part):
            ii = p * rows_per_part + r
            for lt in range(tb // LANES):
                ls = slice(lt * LANES, (lt + 1) * LANES)
                thr_b = [thr_cur[h, ii:ii + 1, ls] for h in range(PEER_HEADS)]
                w0_b = [w0_cur[h, ii:ii + 1, ls] for h in range(PEER_HEADS)]
                for jb in range(PEER_N_KEYS // sub):
                    rows = slice(r * PEER_N_KEYS + jb * sub, r * PEER_N_KEYS + (jb + 1) * sub)
                    a = a_scr[p][z, rows, ls]
                    act = 0.5 * a * (1.0 + lax.erf(a * inv_sqrt2))
                    g = None
                    for h in range(PEER_HEADS):
                        e1 = e1_scr[h, jb * sub:(jb + 1) * sub, ls]
                        term = jnp.where(e1 >= thr_b[h], e1, 0.0) * w0_b[h]
                        g = term if g is None else g + term
                    c_scr[p][z, rows, ls] = (g * act).astype(BF16)

    def values(p):
        acc_scr[z] += jnp.dot(vt_ref[p], c_scr[p][z], preferred_element_type=F32)

    for p in range(PEER_PARTS):
        scores(p)
    for p in range(PEER_PARTS):
        coefficients(p)
        values(p)

    @pl.when(ec == pl.num_programs(1) - 1)
    def _():
        out_ref[...] = _rms(h_ref[...] + acc_scr[0].T, gfin_ref[...])


def _peer(st, aux, hnt, u, vt, h, g_final):
    d, t = hnt.shape
    tb = PEER_TOKENS
    ec = PEER_CHUNK
    n_chunks = u.shape[0] // ec
    rows = ec // PEER_N_KEYS
    return pl.pallas_call(
        _peer_body,
        grid=(t // tb, n_chunks),
        in_specs=[pl.BlockSpec((PEER_HEADS, 2 * PEER_N_KEYS, tb), lambda i, j: (0, 0, i),
                               pipeline_mode=pl.Buffered(1)),
                  pl.BlockSpec((4, PEER_HEADS, tb), lambda i, j: (0, 0, i)),
                  pl.BlockSpec((d, tb), lambda i, j: (0, i), pipeline_mode=pl.Buffered(1)),
                  pl.BlockSpec((ec, d), lambda i, j: (j, 0)),
                  pl.BlockSpec((PEER_PARTS, d, ec // PEER_PARTS), lambda i, j: (j, 0, 0)),
                  pl.BlockSpec((tb, d), lambda i, j: (i, 0), pipeline_mode=pl.Buffered(1)),
                  _resident(g_final.shape)],
        out_specs=pl.BlockSpec((tb, d), lambda i, j: (i, 0)),
        out_shape=jax.ShapeDtypeStruct((t, d), F32),
        scratch_shapes=([pltpu.VMEM((PEER_HEADS, n_chunks, rows, tb), F32),
                         pltpu.VMEM((PEER_HEADS, n_chunks, rows, tb), F32),
                         pltpu.VMEM((PEER_HEADS, PEER_N_KEYS, tb), F32),
                         pltpu.VMEM((PEER_HEADS, rows, tb), F32),
                         pltpu.VMEM((PEER_HEADS, rows, tb), F32),
                         pltpu.VMEM((1, d, tb), F32)]
                        + [pltpu.VMEM((1, ec // PEER_PARTS, tb), F32)] * PEER_PARTS
                        + [pltpu.VMEM((1, ec // PEER_PARTS, tb), BF16)] * PEER_PARTS),
        compiler_params=_params(2),
        name="peer_dense",
    )(st, aux, hnt, u, vt, h, g_final)


def _rope_tables(pos):
    half = SWA_HEAD_DIM // 2
    inv = jnp.exp(-math.log(ROPE_THETA) * jnp.arange(half, dtype=F32) * (2.0 / SWA_HEAD_DIM))
    ang = pos.astype(F32)[:, None] * inv[None, :]
    cos, sin = jnp.cos(ang), jnp.sin(ang)
    reps = LANES // SWA_HEAD_DIM
    return (jnp.tile(jnp.concatenate([cos, cos], axis=1), (1, reps)),
            jnp.tile(jnp.concatenate([-sin, sin], axis=1), (1, reps)))


def _pad_heads_to_kv_half(w, axis):
    shape = w.shape
    w = w.reshape(shape[:axis] + (SWA_KV_HEADS, SWA_GROUP, SWA_HEAD_DIM) + shape[axis + 1:])
    parts = []
    for kv in range(SWA_KV_HEADS):
        heads = lax.index_in_dim(w, kv, axis, keepdims=False)
        pads = [(0, 0)] * heads.ndim
        pads[axis + 1] = (kv * SWA_HEAD_DIM, (SWA_KV_HEADS - 1 - kv) * SWA_HEAD_DIM)
        parts.append(jnp.pad(heads, pads))
    out = jnp.concatenate(parts, axis=axis)
    return out.reshape(shape[:axis] + (SWA_HEADS * LANES,) + shape[axis + 1:])


def kernel(x_prompt, x_sample, state_gla, cache_win_k, cache_win_v, g_attn, w_in, w_gate_up, b_gate,
           g_gla_out, attn_sinks, w_out, g_ffn, w_peer_q, peer_sub_keys, peer_u, peer_v, g_final):
    bsz, seq, d = x_prompt.shape
    dbsz, dseq, _ = x_sample.shape
    win = cache_win_k.shape[2]

    n_gla = 2 * GLA_HEADS * GLA_DK + 2 * GLA_WIDTH
    o_sq = n_gla + GLA_GATE_RANK
    w_gla = w_in[0, :, :n_gla].astype(BF16)
    ga = jnp.pad(w_in[0, :, n_gla:o_sq], ((0, 0), (0, LANES - GLA_GATE_RANK)))
    wsq = w_in[0, :, o_sq:o_sq + SWA_WIDTH]
    wkv = w_in[0, :, o_sq + SWA_WIDTH:]
    w_swa_prompt = jnp.concatenate([wsq, wkv, ga], axis=1).astype(BF16)
    w_swa_sample = jnp.concatenate([_pad_heads_to_kv_half(wsq, 1), wkv, ga], axis=1).astype(BF16)
    wgu = jnp.pad(w_gate_up[0], ((0, LANES - GLA_GATE_RANK), (0, 0)))
    bg = b_gate[0][None, :]
    g_a = g_attn[0][None, :]
    g_f = g_ffn[0][None, :]
    g_n = g_gla_out[0][None, :]
    wo_gla = w_out[0, :GLA_WIDTH].astype(BF16)
    wo_swa = w_out[0, GLA_WIDTH:]
    wo_swa_pad = _pad_heads_to_kv_half(wo_swa, 0).astype(BF16)
    wo_swa = wo_swa.astype(BF16)
    wq = w_peer_q[0].astype(BF16)
    sk0 = jnp.pad(peer_sub_keys[0, :, 0], ((0, 0), (0, 0), (0, PEER_HALF)))
    sk1 = jnp.pad(peer_sub_keys[0, :, 1], ((0, 0), (0, 0), (PEER_HALF, 0)))
    keys = jnp.concatenate([sk0, sk1], axis=1).astype(BF16)
    u = peer_u[0].astype(BF16)
    e_piece = PEER_CHUNK // PEER_PARTS
    vt = peer_v[0].reshape(PEER_EXPERTS // e_piece, e_piece, d).transpose(0, 2, 1).astype(BF16)
    sinks = attn_sinks[0]
    sink_row = jnp.tile(sinks, dseq)[None, :]

    cos_p, sin_p = _rope_tables(jnp.arange(seq))
    cos_s, sin_s = _rope_tables(PAST_LEN + jnp.arange(dseq))
    cos_s = jnp.tile(cos_s, (PROJ_ROWS // dseq, 1))
    sin_s = jnp.tile(sin_s, (PROJ_ROWS // dseq, 1))

    def ffn(h, hn, st):
        aux = _topk(st)
        return _peer(st, aux, hn, u, vt, h, g_final[None, :])

    tp = bsz * seq
    q, k, v, r, la, sq, sk, sv = _proj(x_prompt.reshape(tp, d), g_a, w_gla, w_swa_prompt, wgu, bg,
                                       cos_p, sin_p)
    b3 = lambda a: a.reshape(bsz, seq, a.shape[-1])
    o_gla, s_prompt = _gla_prompt(b3(q), b3(k), b3(v), b3(la))
    o_swa = _swa_prompt(b3(sq), b3(sk), b3(sv), sinks)
    h, hn, st = _merge(o_gla.reshape(tp, GLA_WIDTH), r, o_swa.reshape(tp, SWA_WIDTH),
                       x_prompt.reshape(tp, d), g_n, wo_gla, wo_swa, g_f, wq, keys)
    y_prompt = ffn(h, hn, st).reshape(bsz, seq, d)
    kv_shape = (1, bsz, WINDOW, SWA_KV_HEADS, SWA_HEAD_DIM)
    k_prompt = b3(sk)[:, seq - WINDOW:].reshape(kv_shape)
    v_prompt = b3(sv)[:, seq - WINDOW:].reshape(kv_shape)

    ts = dbsz * dseq
    q, k, v, r, la, sq, sk, sv = _proj(x_sample.reshape(ts, d), g_a, w_gla, w_swa_sample, wgu, bg,
                                       cos_s, sin_s)
    d3 = lambda a: a.reshape(dbsz, dseq, a.shape[-1])
    o_gla, s_sample = _gla_sample(d3(q), d3(k), d3(v), d3(la), state_gla[0])
    o_swa, k_sample, v_sample = _swa_sample(
        sq.reshape(dbsz, dseq * SWA_HEADS, LANES),
        cache_win_k[0].reshape(dbsz, win, LANES), cache_win_v[0].reshape(dbsz, win, LANES),
        d3(sk), d3(sv), sink_row)
    h, hn, st = _merge(o_gla.reshape(ts, GLA_WIDTH), r, o_swa.reshape(ts, SWA_HEADS * LANES),
                       x_sample.reshape(ts, d), g_n, wo_gla, wo_swa_pad, g_f, wq, keys)
    y_sample = ffn(h, hn, st).reshape(dbsz, dseq, d)
    ckv_shape = (1, dbsz, win, SWA_KV_HEADS, SWA_HEAD_DIM)

    return (y_prompt, y_sample, s_prompt[None], k_prompt, v_prompt, s_sample[None],
            k_sample.reshape(ckv_shape), v_sample.reshape(ckv_shape))
```

```python
import functools
import math

import jax
import jax.numpy as jnp
from jax import lax
from jax.experimental import pallas as pl
from jax.experimental.pallas import tpu as pltpu

F32 = jnp.float32
BF16 = jnp.bfloat16
HIGHEST = lax.Precision.HIGHEST

D_MODEL = 2048
GLA_HEADS = 4
GLA_DK = 128
GLA_DV = 256
GLA_WIDTH = GLA_HEADS * GLA_DV
GLA_GATE_RANK = 16
GLA_TAU = 16.0
GLA_CHUNK = 64
SWA_HEAD_DIM = 64
SWA_HEADS = 16
SWA_KV_HEADS = 2
SWA_GROUP = SWA_HEADS // SWA_KV_HEADS
SWA_WIDTH = SWA_HEADS * SWA_HEAD_DIM
WINDOW = 128
ROPE_THETA = 10000.0
PAST_LEN = 8192
PEER_N_KEYS = 128
PEER_EXPERTS = PEER_N_KEYS * PEER_N_KEYS
PEER_HEADS = 8
PEER_QDIM = 128
PEER_HALF = PEER_QDIM // 2
PEER_TOPK = 16
EPS = 1e-6
NEG_INF = -1e30

LANES = 128
BF16_SUBLANES = 16
PROJ_ROWS = 256
TOPK_TOKENS = 128
PEER_TOKENS = 512
PEER_CHUNK = 1024
PEER_PARTS = 2
SAMPLE_BATCH_BLOCK = 16
VMEM_LIMIT = 60 * 1024 * 1024

NT_DIMS = (((1,), (1,)), ((), ()))


def _resident(shape):
    nd = len(shape)
    return pl.BlockSpec(shape, lambda *_: (0,) * nd, pipeline_mode=pl.Buffered(1))


def _params(n_axes):
    return pltpu.CompilerParams(dimension_semantics=("arbitrary",) * n_axes,
                                vmem_limit_bytes=VMEM_LIMIT)


def _rms(x, g):
    return x * lax.rsqrt(jnp.mean(x * x, axis=-1, keepdims=True) + EPS) * g


def _proj_body(x_ref, g_ref, wg_ref, ws_ref, wgu_ref, bg_ref, cos_ref, sin_ref,
               q_ref, k_ref, v_ref, r_ref, la_ref, sq_ref, sk_ref, sv_ref):
    xn = _rms(x_ref[...], g_ref[...]).astype(BF16)
    wsq = sq_ref.shape[1]

    def proj(w_ref, start, width):
        return jnp.dot(xn, w_ref[:, start:start + width], preferred_element_type=F32)

    nk = GLA_HEADS * GLA_DK
    q_ref[...] = proj(wg_ref, 0, nk) * (GLA_DK ** -0.5)
    k_ref[...] = proj(wg_ref, nk, nk)
    v_ref[...] = proj(wg_ref, 2 * nk, GLA_WIDTH)
    r_ref[...] = proj(wg_ref, 2 * nk + GLA_WIDTH, GLA_WIDTH)

    sq_all = proj(ws_ref, 0, wsq)
    tail = proj(ws_ref, wsq, 3 * LANES)
    ga = tail[:, 2 * LANES:]
    z = jnp.dot(ga, wgu_ref[...], precision=HIGHEST, preferred_element_type=F32) + bg_ref[...]
    la_ref[...] = (jnp.minimum(z, 0.0) - jnp.log1p(jnp.exp(-jnp.abs(z)))) * (1.0 / GLA_TAU)

    cos = cos_ref[...]
    sin = sin_ref[...]
    lane = lax.broadcasted_iota(jnp.int32, cos.shape, 1)
    first_half = (lane & (SWA_HEAD_DIM // 2)) == 0

    def rope(t):
        rot = jnp.where(first_half, pltpu.roll(t, LANES - SWA_HEAD_DIM // 2, 1),
                        pltpu.roll(t, SWA_HEAD_DIM // 2, 1))
        return t * cos + rot * sin

    for c in range(wsq // LANES):
        sq_ref[:, c * LANES:(c + 1) * LANES] = rope(sq_all[:, c * LANES:(c + 1) * LANES])
    sk_ref[...] = rope(tail[:, :LANES])
    sv_ref[...] = tail[:, LANES:2 * LANES]


def _proj(x, g, w_gla, w_swa, wgu, bg, cos, sin):
    t, d = x.shape
    tm = PROJ_ROWS
    n_tab = cos.shape[0] // tm
    row = lambda i: (i, 0)
    tab = lambda i: (i % n_tab, 0)
    wsq = w_swa.shape[1] - 3 * LANES
    nk = GLA_HEADS * GLA_DK
    widths = (nk, nk, GLA_WIDTH, GLA_WIDTH, nk, wsq, LANES, LANES)
    return pl.pallas_call(
        _proj_body,
        grid=(t // tm,),
        in_specs=[pl.BlockSpec((tm, d), row), _resident(g.shape), _resident(w_gla.shape),
                  _resident(w_swa.shape), _resident(wgu.shape), _resident(bg.shape),
                  pl.BlockSpec((tm, LANES), tab), pl.BlockSpec((tm, LANES), tab)],
        out_specs=[pl.BlockSpec((tm, n), row) for n in widths],
        out_shape=[jax.ShapeDtypeStruct((t, n), F32) for n in widths],
        compiler_params=_params(1),
        name="proj",
    )(x, g, w_gla, w_swa, wgu, bg, cos, sin)


def _gla_prompt_body(q_ref, k_ref, v_ref, la_ref, o_ref, sfin_ref,
                     s_scr, qt_scr, ds_scr, dec_scr, b_scr, kt_scr, ket_scr, a_scr):
    n = pl.program_id(0)
    bsz = q_ref.shape[0]
    n_ch = q_ref.shape[1] // GLA_CHUNK

    @pl.when(n == 0)
    def _():
        s_scr[...] = jnp.zeros_like(s_scr)

    c = GLA_CHUNK
    row = lax.broadcasted_iota(jnp.int32, (c, c), 0)
    col = lax.broadcasted_iota(jnp.int32, (c, c), 1)
    causal = row >= col
    tri = causal.astype(F32)
    z = lax.shift_right_logical(n, 30)

    def units():
        for ch in range(n_ch):
            for bi in range(bsz):
                for h in range(GLA_HEADS):
                    yield ((ch * bsz + bi) * GLA_HEADS + h, ch * bsz + bi, bi,
                           slice(ch * c, (ch + 1) * c),
                           slice(h * GLA_DK, (h + 1) * GLA_DK), slice(h * GLA_DV, (h + 1) * GLA_DV))

    for ch in range(n_ch):
        for bi in range(bsz):
            b_scr[z, ch * bsz + bi] = jnp.dot(tri, la_ref[bi, ch * c:(ch + 1) * c, :], precision=HIGHEST,
                                              preferred_element_type=F32)
    for unit, cb, bi, rows, kl, vl in units():
        b = b_scr[z, cb, :, kl]
        b_last = b[c - 1:c, :]
        k = k_ref[bi, rows, kl]
        qt_scr[z, unit] = (q_ref[bi, rows, kl] * jnp.exp(b)).astype(BF16)
        kt_scr[z, unit] = (k * jnp.exp(-b)).astype(BF16)
        ket_scr[z, unit] = (k * jnp.exp(b_last - b)).T.astype(BF16)
        dec_scr[z, unit] = jnp.exp(jnp.broadcast_to(b_last, (GLA_DK, GLA_DK))).T
    for unit, cb, bi, rows, kl, vl in units():
        a = lax.dot_general(qt_scr[z, unit], kt_scr[z, unit], NT_DIMS, preferred_element_type=F32)
        a_scr[z, unit] = jnp.where(causal, a, 0.0).astype(BF16)
    for unit, cb, bi, rows, kl, vl in units():
        vb = v_ref[bi, rows, vl].astype(BF16)
        o_ref[bi, rows, vl] = jnp.dot(a_scr[z, unit], vb, preferred_element_type=F32)
        ds_scr[z, unit] = jnp.dot(ket_scr[z, unit], vb, preferred_element_type=F32)

    for ch in range(n_ch):
        rows = slice(ch * c, (ch + 1) * c)
        for bi in range(bsz):
            for h in range(GLA_HEADS):
                unit = (ch * bsz + bi) * GLA_HEADS + h
                chain = bi * GLA_HEADS + h
                vl = slice(h * GLA_DV, (h + 1) * GLA_DV)
                s = s_scr[chain]
                o_ref[bi, rows, vl] += jnp.dot(qt_scr[z, unit], s.astype(BF16), preferred_element_type=F32)
                decay = dec_scr[z, unit]
                s_scr[chain] = jnp.concatenate([decay, decay], axis=1) * s + ds_scr[z, unit]

    @pl.when(n == pl.num_programs(0) - 1)
    def _():
        sfin_ref[...] = s_scr[...].reshape(sfin_ref.shape)


GLA_CHUNKS_PER_STEP = 4


def _gla_prompt(q, k, v, la):
    bsz, s, _ = q.shape
    rows = GLA_CHUNK * GLA_CHUNKS_PER_STEP
    units = GLA_CHUNKS_PER_STEP * bsz * GLA_HEADS
    qk_spec = pl.BlockSpec((bsz, rows, GLA_HEADS * GLA_DK), lambda n: (0, n, 0))
    v_spec = pl.BlockSpec((bsz, rows, GLA_WIDTH), lambda n: (0, n, 0))
    return pl.pallas_call(
        _gla_prompt_body,
        grid=(s // rows,),
        in_specs=[qk_spec, qk_spec, v_spec, qk_spec],
        out_specs=[v_spec, pl.BlockSpec((bsz, GLA_HEADS, GLA_DK, GLA_DV), lambda n: (0, 0, 0, 0))],
        out_shape=[jax.ShapeDtypeStruct((bsz, s, GLA_WIDTH), F32),
                   jax.ShapeDtypeStruct((bsz, GLA_HEADS, GLA_DK, GLA_DV), F32)],
        scratch_shapes=[pltpu.VMEM((bsz * GLA_HEADS, GLA_DK, GLA_DV), F32),
                        pltpu.VMEM((1, units, GLA_CHUNK, GLA_DK), BF16),
                        pltpu.VMEM((1, units, GLA_DK, GLA_DV), F32),
                        pltpu.VMEM((1, units, GLA_DK, GLA_DK), F32),
                        pltpu.VMEM((1, units // GLA_HEADS, GLA_CHUNK, GLA_HEADS * GLA_DK), F32),
                        pltpu.VMEM((1, units, GLA_CHUNK, GLA_DK), BF16),
                        pltpu.VMEM((1, units, GLA_DK, GLA_CHUNK), BF16),
                        pltpu.VMEM((1, units, GLA_CHUNK, GLA_CHUNK), BF16)],
        compiler_params=_params(1),
        name="gla_prompt",
    )(q, k, v, la)


def _gla_sample_body(q_ref, k_ref, v_ref, la_ref, s0_ref, o_ref, sout_ref, oi_scr, dsum_scr, upd_scr):
    bb, t, _ = q_ref.shape
    rows = bb * t
    q = q_ref[...].reshape(rows, GLA_DK)
    k = k_ref[...].reshape(rows, GLA_DK)
    la = la_ref[...].reshape(rows, GLA_DK)
    v = v_ref[...].reshape(rows, GLA_DV)
    row = lax.broadcasted_iota(jnp.int32, (rows, rows), 0)
    col = lax.broadcasted_iota(jnp.int32, (rows, rows), 1)
    same_seq = (row // t) == (col // t)
    causal = same_seq & (row >= col)
    b = jnp.dot(causal.astype(F32), la, precision=HIGHEST, preferred_element_type=F32)
    b_tot = jnp.dot(same_seq.astype(F32), la, precision=HIGHEST, preferred_element_type=F32)
    q_t = (q * jnp.exp(b)).astype(BF16)
    k_t = (k * jnp.exp(-b)).astype(BF16)
    k_e_t = (k * jnp.exp(b_tot - b)).T.astype(BF16)
    la_t = la.T
    a = lax.dot_general(q_t, k_t, NT_DIMS, preferred_element_type=F32)
    a = jnp.where(causal, a, 0.0).astype(BF16)
    o = jnp.dot(a, v.astype(BF16), preferred_element_type=F32)
    seq_of_row_v = lax.broadcasted_iota(jnp.int32, (rows, GLA_DV), 0) // t
    seq_of_row_k = lax.broadcasted_iota(jnp.int32, (rows, GLA_DK), 0) // t
    z = lax.shift_right_logical(pl.program_id(0), 30)
    for i in range(bb):
        oi_scr[z, i] = jnp.dot(q_t, s0_ref[i].astype(BF16), preferred_element_type=F32)
    for i in range(bb):
        ones_i = jnp.where(seq_of_row_k == i, 1.0, 0.0)
        dsum_scr[z, i] = jnp.dot(la_t, ones_i, precision=HIGHEST, preferred_element_type=F32)
    for i in range(bb):
        v_i = jnp.where(seq_of_row_v == i, v, 0.0).astype(BF16)
        upd_scr[z, i] = jnp.dot(k_e_t, v_i, preferred_element_type=F32)
    for i in range(bb):
        o = o + jnp.where(seq_of_row_v == i, oi_scr[z, i], 0.0)
        decay = jnp.exp(dsum_scr[z, i])
        sout_ref[i] = jnp.concatenate([decay, decay], axis=1) * s0_ref[i] + upd_scr[z, i]
    o_ref[...] = o.reshape(bb, t, GLA_DV)


def _gla_sample(q, k, v, la, s0):
    bsz, t, _ = q.shape
    bb = SAMPLE_BATCH_BLOCK
    qk_spec = pl.BlockSpec((bb, t, GLA_DK), lambda i, h: (i, 0, h))
    v_spec = pl.BlockSpec((bb, t, GLA_DV), lambda i, h: (i, 0, h))
    s_spec = pl.BlockSpec((bb, None, GLA_DK, GLA_DV), lambda i, h: (i, h, 0, 0))
    return pl.pallas_call(
        _gla_sample_body,
        grid=(bsz // bb, GLA_HEADS),
        in_specs=[qk_spec, qk_spec, v_spec, qk_spec, s_spec],
        out_specs=[v_spec, s_spec],
        out_shape=[jax.ShapeDtypeStruct((bsz, t, GLA_WIDTH), F32),
                   jax.ShapeDtypeStruct(s0.shape, F32)],
        scratch_shapes=[pltpu.VMEM((1, bb, bb * t, GLA_DV), F32),
                        pltpu.VMEM((1, bb, GLA_DK, GLA_DK), F32),
                        pltpu.VMEM((1, bb, GLA_DK, GLA_DV), F32)],
        compiler_params=_params(2),
        name="gla_sample",
    )(q, k, v, la, s0)


def _sink_softmax(s, mask, sink):
    s = jnp.where(mask, s, NEG_INF)
    m = jnp.maximum(jnp.max(s, axis=-1, keepdims=True), sink)
    p = jnp.exp(s - m)
    return p / (jnp.sum(p, axis=-1, keepdims=True) + jnp.exp(sink - m))


def _split_kv_lanes(x):
    lane = lax.broadcasted_iota(jnp.int32, x.shape, 1)
    low = lane < SWA_HEAD_DIM
    lo0 = jnp.where(low, x, 0.0)
    hi1 = jnp.where(low, 0.0, x)
    hi0 = pltpu.roll(lo0, SWA_HEAD_DIM, 1)
    lo1 = pltpu.roll(hi1, SWA_HEAD_DIM, 1)
    return (lo0.astype(BF16), lo1.astype(BF16)), (hi0.astype(BF16), hi1.astype(BF16))


def _swa_prompt_body(sink_ref, q_ref, kc_ref, kp_ref, vc_ref, vp_ref, o_ref, st_scr, p_scr):
    n = pl.program_id(1)
    blk = WINDOW
    k_parts = _split_kv_lanes(jnp.concatenate([kp_ref[...], kc_ref[...]], axis=0))
    v_parts = _split_kv_lanes(jnp.concatenate([vp_ref[...], vc_ref[...]], axis=0))
    kj = lax.broadcasted_iota(jnp.int32, (2 * blk, blk), 0)
    qi = lax.broadcasted_iota(jnp.int32, (2 * blk, blk), 1)
    diff = qi + blk - kj
    mask = (diff >= 0) & (diff < WINDOW) & ((n > 0) | (kj >= blk))
    scale = SWA_HEAD_DIM ** -0.5
    contract_rows = (((0,), (0,)), ((), ()))
    z = lax.shift_right_logical(n, 30)
    for head in range(SWA_HEADS):
        pair, half = divmod(head, 2)
        qp = q_ref[:, pair * LANES:(pair + 1) * LANES].astype(BF16)
        st_scr[z, head] = lax.dot_general(k_parts[half][head // SWA_GROUP], qp, NT_DIMS,
                                          preferred_element_type=F32)
    for head in range(SWA_HEADS):
        sink = sink_ref[head]
        st = jnp.where(mask, st_scr[z, head] * scale, NEG_INF)
        m = jnp.maximum(jnp.max(st, axis=0, keepdims=True), sink)
        p = jnp.exp(st - m)
        denom = jnp.sum(p, axis=0, keepdims=True) + jnp.exp(sink - m)
        p_scr[z, head] = (p * (1.0 / denom)).astype(BF16)
    for pair in range(SWA_HEADS // 2):
        kv = (2 * pair) // SWA_GROUP
        o_ref[:, pair * LANES:(pair + 1) * LANES] = (
            lax.dot_general(p_scr[z, 2 * pair], v_parts[0][kv], contract_rows, preferred_element_type=F32)
            + lax.dot_general(p_scr[z, 2 * pair + 1], v_parts[1][kv], contract_rows,
                              preferred_element_type=F32))


def _swa_prompt(sq, sk, sv, sinks):
    bsz, s, _ = sq.shape
    blk = WINDOW
    cur = lambda b, n: (b, n, 0)
    prev = lambda b, n: (b, jnp.maximum(n - 1, 0), 0)
    kv_cur = pl.BlockSpec((None, blk, LANES), cur)
    kv_prev = pl.BlockSpec((None, blk, LANES), prev)
    q_spec = pl.BlockSpec((None, blk, SWA_WIDTH), cur)
    return pl.pallas_call(
        _swa_prompt_body,
        grid=(bsz, s // blk),
        in_specs=[pl.BlockSpec(memory_space=pltpu.SMEM), q_spec, kv_cur, kv_prev, kv_cur, kv_prev],
        out_specs=q_spec,
        out_shape=jax.ShapeDtypeStruct((bsz, s, SWA_WIDTH), F32),
        scratch_shapes=[pltpu.VMEM((1, SWA_HEADS, 2 * blk, blk), F32),
                        pltpu.VMEM((1, SWA_HEADS, 2 * blk, blk), BF16)],
        compiler_params=_params(2),
        name="swa_prompt",
    )(sinks, sq, sk, sk, sv, sv)


def _swa_sample_body(t_new, sink_ref, q_ref, ck_ref, cv_ref, kn_ref, vn_ref, o_ref, nk_ref, nv_ref,
                     st_scr, p_scr):
    bb, rows, _ = q_ref.shape
    w = ck_ref.shape[1]
    pad = jnp.zeros((BF16_SUBLANES - t_new, LANES), F32)
    n_keys = w + BF16_SUBLANES
    c = lax.broadcasted_iota(jnp.int32, (n_keys, rows), 0)
    tq = lax.broadcasted_iota(jnp.int32, (n_keys, rows), 1) // SWA_HEADS
    mask = ((c < w) & (c > tq)) | ((c >= w) & (c - w <= tq) & (c < w + t_new))
    sink = sink_ref[...]
    scale = SWA_HEAD_DIM ** -0.5
    contract_rows = (((0,), (0,)), ((), ()))
    z = lax.shift_right_logical(pl.program_id(0), 30)
    for i in range(bb):
        kcat = jnp.concatenate([ck_ref[i], kn_ref[i], pad], axis=0).astype(BF16)
        st_scr[z, i] = lax.dot_general(kcat, q_ref[i].astype(BF16), NT_DIMS, preferred_element_type=F32)
    for i in range(bb):
        st = jnp.where(mask, st_scr[z, i] * scale, NEG_INF)
        m = jnp.maximum(jnp.max(st, axis=0, keepdims=True), sink)
        p = jnp.exp(st - m)
        denom = jnp.sum(p, axis=0, keepdims=True) + jnp.exp(sink - m)
        p_scr[z, i] = (p * (1.0 / denom)).astype(BF16)
    for i in range(bb):
        vcat = jnp.concatenate([cv_ref[i], vn_ref[i], pad], axis=0).astype(BF16)
        o_ref[i] = lax.dot_general(p_scr[z, i], vcat, contract_rows, preferred_element_type=F32)
        nk_ref[i, 0:w - t_new, :] = ck_ref[i, t_new:, :]
        nk_ref[i, w - t_new:w, :] = kn_ref[i]
        nv_ref[i, 0:w - t_new, :] = cv_ref[i, t_new:, :]
        nv_ref[i, w - t_new:w, :] = vn_ref[i]


def _swa_sample(q_pad, ck, cv, kn, vn, sink_row):
    bsz, rows, _ = q_pad.shape
    t_new = kn.shape[1]
    w = ck.shape[1]
    bb = SAMPLE_BATCH_BLOCK
    n_keys = w + BF16_SUBLANES
    blk = lambda r: pl.BlockSpec((bb, r, LANES), lambda i: (i, 0, 0))
    return pl.pallas_call(
        functools.partial(_swa_sample_body, t_new),
        grid=(bsz // bb,),
        in_specs=[_resident(sink_row.shape), blk(rows), blk(w), blk(w), blk(t_new), blk(t_new)],
        out_specs=[blk(rows), blk(w), blk(w)],
        out_shape=[jax.ShapeDtypeStruct((bsz, rows, LANES), F32),
                   jax.ShapeDtypeStruct(ck.shape, F32), jax.ShapeDtypeStruct(cv.shape, F32)],
        scratch_shapes=[pltpu.VMEM((1, bb, n_keys, rows), F32),
                        pltpu.VMEM((1, bb, n_keys, rows), BF16)],
        compiler_params=_params(1),
        name="swa_sample",
    )(sink_row, q_pad, ck, cv, kn, vn)


def _merge_body(og_ref, r_ref, os_ref, x_ref, gn_ref, wog_ref, wos_ref, gf_ref, wq_ref, keys_ref,
                h_ref, hn_ref, st_ref):
    og = og_ref[...]
    gn = gn_ref[...]
    heads = [_rms(og[:, i * GLA_DV:(i + 1) * GLA_DV], gn) for i in range(GLA_HEADS)]
    r = r_ref[...]
    gated = jnp.concatenate(heads, axis=1) * (r / (1.0 + jnp.exp(-r)))
    h = (x_ref[...]
         + jnp.dot(gated.astype(BF16), wog_ref[...], preferred_element_type=F32)
         + jnp.dot(os_ref[...].astype(BF16), wos_ref[...], preferred_element_type=F32))
    h_ref[...] = h
    hn_f32 = _rms(h, gf_ref[...])
    hn = hn_f32.astype(BF16)
    hn_ref[...] = hn_f32.T.astype(BF16)
    pq = jnp.dot(hn, wq_ref[...], preferred_element_type=F32).astype(BF16)
    for i in range(PEER_HEADS):
        st_ref[i] = lax.dot_general(keys_ref[i], pq[:, i * PEER_QDIM:(i + 1) * PEER_QDIM], NT_DIMS,
                                    preferred_element_type=F32)


def _merge(o_gla, r, o_swa, x, gn, wog, wos, gf, wq, keys):
    t, d = x.shape
    tm = PROJ_ROWS
    row = lambda i: (i, 0)
    return pl.pallas_call(
        _merge_body,
        grid=(t // tm,),
        in_specs=[pl.BlockSpec((tm, GLA_WIDTH), row), pl.BlockSpec((tm, GLA_WIDTH), row),
                  pl.BlockSpec((tm, o_swa.shape[1]), row), pl.BlockSpec((tm, d), row),
                  _resident(gn.shape), _resident(wog.shape), _resident(wos.shape),
                  _resident(gf.shape), _resident(wq.shape), _resident(keys.shape)],
        out_specs=[pl.BlockSpec((tm, d), row), pl.BlockSpec((d, tm), lambda i: (0, i)),
                   pl.BlockSpec((PEER_HEADS, 2 * PEER_N_KEYS, tm), lambda i: (0, 0, i))],
        out_shape=[jax.ShapeDtypeStruct((t, d), F32), jax.ShapeDtypeStruct((d, t), BF16),
                   jax.ShapeDtypeStruct((PEER_HEADS, 2 * PEER_N_KEYS, t), F32)],
        compiler_params=_params(1),
        name="merge",
    )(o_gla, r, o_swa, x, gn, wog, wos, gf, wq, keys)


N_SORTED = PEER_TOPK + 1


def _sorting_network(n):
    pairs = []
    p = 1
    while p < n:
        k = p
        while k >= 1:
            for j in range(k % p, n - k, 2 * k):
                for i in range(min(k, n - j - k)):
                    if (i + j) // (2 * p) == (i + j + k) // (2 * p):
                        pairs.append((i + j, i + j + k))
            k //= 2
        p *= 2
    return pairs


def _largest(vals, n_out):
    n = 1
    while n < len(vals):
        n *= 2
    wires = list(vals) + [None] * (n - len(vals))
    needed = set(range(n_out))
    kept = []
    for i, j in reversed(_sorting_network(n)):
        if i in needed or j in needed:
            needed.update((i, j))
            kept.append((i, j))
    for i, j in reversed(kept):
        hi, lo = wires[i], wires[j]
        if lo is None:
            continue
        if hi is None:
            wires[i], wires[j] = lo, None
        else:
            wires[i], wires[j] = jnp.maximum(hi, lo), jnp.minimum(hi, lo)
    return wires[:n_out]


def _topk_body(st_ref, aux_ref, top0_scr, top1_scr):
    scr = (top0_scr, top1_scr)
    sublanes = 8
    n_tiles = PEER_N_KEYS // sublanes
    network = _sorting_network(n_tiles)
    for h in range(PEER_HEADS):
        for half in range(2):
            base = half * PEER_N_KEYS
            col = [st_ref[h, base + k * sublanes:base + (k + 1) * sublanes, :] for k in range(n_tiles)]
            for i, j in network:
                col[i], col[j] = jnp.maximum(col[i], col[j]), jnp.minimum(col[i], col[j])
            for a in range(N_SORTED):
                m = col[0]
                for shift in (4, 2, 1):
                    m = jnp.maximum(m, pltpu.roll(m, shift, 0))
                scr[half][a, h:h + 1, :] = m[0:1, :]
                if a + 1 < N_SORTED:
                    hit = col[0] == m
                    for k in range(min(n_tiles, N_SORTED - 1 - a)):
                        below = col[k + 1] if k + 1 < n_tiles else NEG_INF
                        col[k] = jnp.where(hit, below, col[k])
    top0 = [top0_scr[a] for a in range(N_SORTED)]
    top1 = [top1_scr[a] for a in range(N_SORTED)]
    cands = [top0[a - 1] + top1[b - 1]
             for a in range(1, N_SORTED + 1) for b in range(1, N_SORTED // a + 1)]
    best = _largest(cands, N_SORTED)
    z = jnp.ones_like(best[0])
    for rank in range(1, PEER_TOPK):
        z = z + jnp.exp(best[rank] - best[0])
    aux_ref[0] = 0.5 * (best[PEER_TOPK - 1] + best[PEER_TOPK])
    aux_ref[1] = top0[0]
    aux_ref[2] = top1[0]
    aux_ref[3] = 1.0 / z


def _topk(st):
    t = st.shape[2]
    tt = TOPK_TOKENS
    return pl.pallas_call(
        _topk_body,
        grid=(t // tt,),
        in_specs=[pl.BlockSpec((PEER_HEADS, 2 * PEER_N_KEYS, tt), lambda i: (0, 0, i))],
        out_specs=pl.BlockSpec((4, PEER_HEADS, tt), lambda i: (0, 0, i)),
        out_shape=jax.ShapeDtypeStruct((4, PEER_HEADS, t), F32),
        scratch_shapes=[pltpu.VMEM((N_SORTED, PEER_HEADS, tt), F32),
                        pltpu.VMEM((N_SORTED, PEER_HEADS, tt), F32)],
        compiler_params=_params(1),
        name="peer_topk",
    )(st)


def _peer_body(st_ref, aux_ref, hnt_ref, u_ref, vt_ref, h_ref, gfin_ref, out_ref,
               thr_scr, w0_scr, e1_scr, thr_cur, w0_cur, acc_scr, *piece_scr):
    ec = pl.program_id(1)
    n_rows = PEER_CHUNK // PEER_N_KEYS
    tb = hnt_ref.shape[1]
    a_scr = piece_scr[:PEER_PARTS]
    c_scr = piece_scr[PEER_PARTS:]

    @pl.when(ec == 0)
    def _():
        grid_rows = (PEER_N_KEYS // n_rows, n_rows, tb)
        for h in range(PEER_HEADS):
            s0 = st_ref[h, 0:PEER_N_KEYS, :]
            s1 = st_ref[h, PEER_N_KEYS:2 * PEER_N_KEYS, :]
            thr_scr[h] = jnp.exp(aux_ref[0, h:h + 1, :] - s0 - aux_ref[2, h:h + 1, :]).reshape(grid_rows)
            w0_scr[h] = (jnp.exp(s0 - aux_ref[1, h:h + 1, :])
                         * (0.5 * aux_ref[3, h:h + 1, :])).reshape(grid_rows)
            e1_scr[h] = jnp.exp(s1 - aux_ref[2, h:h + 1, :])
        acc_scr[...] = jnp.zeros_like(acc_scr)

    sub = 2 * BF16_SUBLANES
    inv_sqrt2 = 1.0 / math.sqrt(2.0)
    e_part = PEER_CHUNK // PEER_PARTS
    rows_per_part = n_rows // PEER_PARTS

    z = lax.shift_right_logical(ec, 30)

    for h in range(PEER_HEADS):
        thr_cur[h] = thr_scr[h, ec]
        w0_cur[h] = w0_scr[h, ec]

    def scores(p):
        a_scr[p][z] = jnp.dot(u_ref[p * e_part:(p + 1) * e_part, :], hnt_ref[...],
                              preferred_element_type=F32)

    def coefficients(p):
        for r in range(rows_per_part):
            ii = p * rows_per_part + r
            for lt in range(tb // LANES):
                ls = slice(lt * LANES, (lt + 1) * LANES)
                thr_b = [thr_cur[h, ii:ii + 1, ls] for h in range(PEER_HEADS)]
                w0_b = [w0_cur[h, ii:ii + 1, ls] for h in range(PEER_HEADS)]
                for jb in range(PEER_N_KEYS // sub):
                    rows = slice(r * PEER_N_KEYS + jb * sub, r * PEER_N_KEYS + (jb + 1) * sub)
                    a = a_scr[p][z, rows, ls]
                    act = a * (1.0 + lax.erf(a * inv_sqrt2))
                    g = None
                    for h in range(PEER_HEADS):
                        e1 = e1_scr[h, jb * sub:(jb + 1) * sub, ls]
                        term = jnp.where(e1 >= thr_b[h], e1, 0.0) * w0_b[h]
                        g = term if g is None else g + term
                    c_scr[p][z, rows, ls] = (g * act).astype(BF16)

    def values(p):
        acc_scr[z] += jnp.dot(vt_ref[p], c_scr[p][z], preferred_element_type=F32)

    for p in range(PEER_PARTS):
        scores(p)
    for p in range(PEER_PARTS):
        coefficients(p)
        values(p)

    @pl.when(ec == pl.num_programs(1) - 1)
    def _():
        out_ref[...] = _rms(h_ref[...] + acc_scr[0].T, gfin_ref[...])


def _peer(st, aux, hnt, u, vt, h, g_final):
    d, t = hnt.shape
    tb = PEER_TOKENS
    ec = PEER_CHUNK
    n_chunks = u.shape[0] // ec
    rows = ec // PEER_N_KEYS
    return pl.pallas_call(
        _peer_body,
        grid=(t // tb, n_chunks),
        in_specs=[pl.BlockSpec((PEER_HEADS, 2 * PEER_N_KEYS, tb), lambda i, j: (0, 0, i),
                               pipeline_mode=pl.Buffered(1)),
                  pl.BlockSpec((4, PEER_HEADS, tb), lambda i, j: (0, 0, i)),
                  pl.BlockSpec((d, tb), lambda i, j: (0, i), pipeline_mode=pl.Buffered(1)),
                  pl.BlockSpec((ec, d), lambda i, j: (j, 0)),
                  pl.BlockSpec((PEER_PARTS, d, ec // PEER_PARTS), lambda i, j: (j, 0, 0)),
                  pl.BlockSpec((tb, d), lambda i, j: (i, 0), pipeline_mode=pl.Buffered(1)),
                  _resident(g_final.shape)],
        out_specs=pl.BlockSpec((tb, d), lambda i, j: (i, 0)),
        out_shape=jax.ShapeDtypeStruct((t, d), F32),
        scratch_shapes=([pltpu.VMEM((PEER_HEADS, n_chunks, rows, tb), F32),
                         pltpu.VMEM((PEER_HEADS, n_chunks, rows, tb), F32),
                         pltpu.VMEM((PEER_HEADS, PEER_N_KEYS, tb), F32),
                         pltpu.VMEM((PEER_HEADS, rows, tb), F32),
                         pltpu.VMEM((PEER_HEADS, rows, tb), F32),
                         pltpu.VMEM((1, d, tb), F32)]
                        + [pltpu.VMEM((1, ec // PEER_PARTS, tb), F32)] * PEER_PARTS
                        + [pltpu.VMEM((1, ec // PEER_PARTS, tb), BF16)] * PEER_PARTS),
        compiler_params=_params(2),
        name="peer_dense",
    )(st, aux, hnt, u, vt, h, g_final)


def _rope_tables(pos):
    half = SWA_HEAD_DIM // 2
    inv = jnp.exp(-math.log(ROPE_THETA) * jnp.arange(half, dtype=F32) * (2.0 / SWA_HEAD_DIM))
    ang = pos.astype(F32)[:, None] * inv[None, :]
    cos, sin = jnp.cos(ang), jnp.sin(ang)
    reps = LANES // SWA_HEAD_DIM
    return (jnp.tile(jnp.concatenate([cos, cos], axis=1), (1, reps)),
            jnp.tile(jnp.concatenate([-sin, sin], axis=1), (1, reps)))


def _pad_heads_to_kv_half(w, axis):
    shape = w.shape
    w = w.reshape(shape[:axis] + (SWA_KV_HEADS, SWA_GROUP, SWA_HEAD_DIM) + shape[axis + 1:])
    parts = []
    for kv in range(SWA_KV_HEADS):
        heads = lax.index_in_dim(w, kv, axis, keepdims=False)
        pads = [(0, 0)] * heads.ndim
        pads[axis + 1] = (kv * SWA_HEAD_DIM, (SWA_KV_HEADS - 1 - kv) * SWA_HEAD_DIM)
        parts.append(jnp.pad(heads, pads))
    out = jnp.concatenate(parts, axis=axis)
    return out.reshape(shape[:axis] + (SWA_HEADS * LANES,) + shape[axis + 1:])


def kernel(x_prompt, x_sample, state_gla, cache_win_k, cache_win_v, g_attn, w_in, w_gate_up, b_gate,
           g_gla_out, attn_sinks, w_out, g_ffn, w_peer_q, peer_sub_keys, peer_u, peer_v, g_final):
    bsz, seq, d = x_prompt.shape
    dbsz, dseq, _ = x_sample.shape
    win = cache_win_k.shape[2]

    n_gla = 2 * GLA_HEADS * GLA_DK + 2 * GLA_WIDTH
    o_sq = n_gla + GLA_GATE_RANK
    w_gla = w_in[0, :, :n_gla].astype(BF16)
    ga = jnp.pad(w_in[0, :, n_gla:o_sq], ((0, 0), (0, LANES - GLA_GATE_RANK)))
    wsq = w_in[0, :, o_sq:o_sq + SWA_WIDTH]
    wkv = w_in[0, :, o_sq + SWA_WIDTH:]
    w_swa_prompt = jnp.concatenate([wsq, wkv, ga], axis=1).astype(BF16)
    w_swa_sample = jnp.concatenate([_pad_heads_to_kv_half(wsq, 1), wkv, ga], axis=1).astype(BF16)
    wgu = jnp.pad(w_gate_up[0], ((0, LANES - GLA_GATE_RANK), (0, 0)))
    bg = b_gate[0][None, :]
    g_a = g_attn[0][None, :]
    g_f = g_ffn[0][None, :]
    g_n = g_gla_out[0][None, :]
    wo_gla = w_out[0, :GLA_WIDTH].astype(BF16)
    wo_swa = w_out[0, GLA_WIDTH:]
    wo_swa_pad = _pad_heads_to_kv_half(wo_swa, 0).astype(BF16)
    wo_swa = wo_swa.astype(BF16)
    wq = w_peer_q[0].astype(BF16)
    sk0 = jnp.pad(peer_sub_keys[0, :, 0], ((0, 0), (0, 0), (0, PEER_HALF)))
    sk1 = jnp.pad(peer_sub_keys[0, :, 1], ((0, 0), (0, 0), (PEER_HALF, 0)))
    keys = jnp.concatenate([sk0, sk1], axis=1).astype(BF16)
    u = peer_u[0].astype(BF16)
    e_piece = PEER_CHUNK // PEER_PARTS
    vt = peer_v[0].reshape(PEER_EXPERTS // e_piece, e_piece, d).transpose(0, 2, 1).astype(BF16)
    sinks = attn_sinks[0]
    sink_row = jnp.tile(sinks, dseq)[None, :]

    cos_p, sin_p = _rope_tables(jnp.arange(seq))
    cos_s, sin_s = _rope_tables(PAST_LEN + jnp.arange(dseq))
    cos_s = jnp.tile(cos_s, (PROJ_ROWS // dseq, 1))
    sin_s = jnp.tile(sin_s, (PROJ_ROWS // dseq, 1))

    def ffn(h, hn, st):
        aux = _topk(st)
        return _peer(st, aux, hn, u, vt, h, g_final[None, :])

    tp = bsz * seq
    q, k, v, r, la, sq, sk, sv = _proj(x_prompt.reshape(tp, d), g_a, w_gla, w_swa_prompt, wgu, bg,
                                       cos_p, sin_p)
    b3 = lambda a: a.reshape(bsz, seq, a.shape[-1])
    o_gla, s_prompt = _gla_prompt(b3(q), b3(k), b3(v), b3(la))
    o_swa = _swa_prompt(b3(sq), b3(sk), b3(sv), sinks)
    h, hn, st = _merge(o_gla.reshape(tp, GLA_WIDTH), r, o_swa.reshape(tp, SWA_WIDTH),
                       x_prompt.reshape(tp, d), g_n, wo_gla, wo_swa, g_f, wq, keys)
    y_prompt = ffn(h, hn, st).reshape(bsz, seq, d)
    kv_shape = (1, bsz, WINDOW, SWA_KV_HEADS, SWA_HEAD_DIM)
    k_prompt = b3(sk)[:, seq - WINDOW:].reshape(kv_shape)
    v_prompt = b3(sv)[:, seq - WINDOW:].reshape(kv_shape)

    ts = dbsz * dseq
    q, k, v, r, la, sq, sk, sv = _proj(x_sample.reshape(ts, d), g_a, w_gla, w_swa_sample, wgu, bg,
                                       cos_s, sin_s)
    d3 = lambda a: a.reshape(dbsz, dseq, a.shape[-1])
    o_gla, s_sample = _gla_sample(d3(q), d3(k), d3(v), d3(la), state_gla[0])
    o_swa, k_sample, v_sample = _swa_sample(
        sq.reshape(dbsz, dseq * SWA_HEADS, LANES),
        cache_win_k[0].reshape(dbsz, win, LANES), cache_win_v[0].reshape(dbsz, win, LANES),
        d3(sk), d3(sv), sink_row)
    h, hn, st = _merge(o_gla.reshape(ts, GLA_WIDTH), r, o_swa.reshape(ts, SWA_HEADS * LANES),
                       x_sample.reshape(ts, d), g_n, wo_gla, wo_swa_pad, g_f, wq, keys)
    y_sample = ffn(h, hn, st).reshape(dbsz, dseq, d)
    ckv_shape = (1, dbsz, win, SWA_KV_HEADS, SWA_HEAD_DIM)

    return (y_prompt, y_sample, s_prompt[None], k_prompt, v_prompt, s_sample[None],
            k_sample.reshape(ckv_shape), v_sample.reshape(ckv_shape))
```

```python
import functools
import math

import jax
import jax.numpy as jnp
from jax import lax
from jax.experimental import pallas as pl
from jax.experimental.pallas import tpu as pltpu

F32 = jnp.float32
BF16 = jnp.bfloat16
HIGHEST = lax.Precision.HIGHEST

GLA_HEADS = 4
GLA_DK = 128
GLA_DV = 256
GLA_WIDTH = GLA_HEADS * GLA_DV
GLA_GATE_RANK = 16
GLA_TAU = 16.0
GLA_CHUNK = 64
SWA_HEAD_DIM = 64
SWA_HEADS = 16
SWA_KV_HEADS = 2
SWA_GROUP = SWA_HEADS // SWA_KV_HEADS
SWA_WIDTH = SWA_HEADS * SWA_HEAD_DIM
WINDOW = 128
ROPE_THETA = 10000.0
PAST_LEN = 8192
PEER_N_KEYS = 128
PEER_EXPERTS = PEER_N_KEYS * PEER_N_KEYS
PEER_HEADS = 8
PEER_QDIM = 128
PEER_HALF = PEER_QDIM // 2
PEER_TOPK = 16
EPS = 1e-6
NEG_INF = -1e30

LANES = 128
BF16_SUBLANES = 16
PROJ_ROWS = 512
MERGE_ROWS = 256
GLA_CHUNKS_PER_STEP = 4
TOPK_TOKENS = 128
PEER_TOKENS = 512
PEER_CHUNK = 1024
PEER_PARTS = 2
SAMPLE_BATCH_BLOCK = 16
VMEM_LIMIT = 60 * 1024 * 1024

NT_DIMS = (((1,), (1,)), ((), ()))


def _resident(shape):
    nd = len(shape)
    return pl.BlockSpec(shape, lambda *_: (0,) * nd, pipeline_mode=pl.Buffered(1))


def _params(n_axes):
    return pltpu.CompilerParams(dimension_semantics=("arbitrary",) * n_axes,
                                vmem_limit_bytes=VMEM_LIMIT)


def _rms(x, g):
    return x * lax.rsqrt(jnp.mean(x * x, axis=-1, keepdims=True) + EPS) * g


def _proj_body(x_ref, g_ref, wg_ref, ws_ref, wgu_ref, bg_ref, cos_ref, sin_ref,
               q_ref, k_ref, v_ref, r_ref, la_ref, sq_ref, sk_ref, sv_ref):
    xn = _rms(x_ref[...], g_ref[...]).astype(BF16)
    wsq = sq_ref.shape[1]

    def proj(w_ref, start, width):
        return jnp.dot(xn, w_ref[:, start:start + width], preferred_element_type=F32)

    nk = GLA_HEADS * GLA_DK
    q_ref[...] = proj(wg_ref, 0, nk) * (GLA_DK ** -0.5)
    k_ref[...] = proj(wg_ref, nk, nk)
    v_ref[...] = proj(wg_ref, 2 * nk, GLA_WIDTH)
    r_ref[...] = proj(wg_ref, 2 * nk + GLA_WIDTH, GLA_WIDTH)

    sq_all = proj(ws_ref, 0, wsq)
    tail = proj(ws_ref, wsq, 3 * LANES)
    ga = tail[:, 2 * LANES:]
    z = jnp.dot(ga, wgu_ref[...], precision=HIGHEST, preferred_element_type=F32) + bg_ref[...]
    la_ref[...] = (jnp.minimum(z, 0.0) - jnp.log1p(jnp.exp(-jnp.abs(z)))) * (1.0 / GLA_TAU)

    cos = cos_ref[...]
    sin = sin_ref[...]
    lane = lax.broadcasted_iota(jnp.int32, cos.shape, 1)
    first_half = (lane & (SWA_HEAD_DIM // 2)) == 0

    def rope(t):
        rot = jnp.where(first_half, pltpu.roll(t, LANES - SWA_HEAD_DIM // 2, 1),
                        pltpu.roll(t, SWA_HEAD_DIM // 2, 1))
        return t * cos + rot * sin

    for c in range(wsq // LANES):
        sq_ref[:, c * LANES:(c + 1) * LANES] = rope(sq_all[:, c * LANES:(c + 1) * LANES])
    sk_ref[...] = rope(tail[:, :LANES])
    sv_ref[...] = tail[:, LANES:2 * LANES]


def _proj(x, g, w_gla, w_swa, wgu, bg, cos, sin):
    t, d = x.shape
    tm = PROJ_ROWS
    n_tab = cos.shape[0] // tm
    row = lambda i: (i, 0)
    tab = lambda i: (i % n_tab, 0)
    wsq = w_swa.shape[1] - 3 * LANES
    nk = GLA_HEADS * GLA_DK
    widths = (nk, nk, GLA_WIDTH, GLA_WIDTH, nk, wsq, LANES, LANES)
    return pl.pallas_call(
        _proj_body,
        grid=(t // tm,),
        in_specs=[pl.BlockSpec((tm, d), row), _resident(g.shape), _resident(w_gla.shape),
                  _resident(w_swa.shape), _resident(wgu.shape), _resident(bg.shape),
                  pl.BlockSpec((tm, LANES), tab), pl.BlockSpec((tm, LANES), tab)],
        out_specs=[pl.BlockSpec((tm, n), row) for n in widths],
        out_shape=[jax.ShapeDtypeStruct((t, n), F32) for n in widths],
        compiler_params=_params(1),
        name="proj",
    )(x, g, w_gla, w_swa, wgu, bg, cos, sin)


def _gla_prompt_body(q_ref, k_ref, v_ref, la_ref, o_ref, sfin_ref,
                     s_scr, qt_scr, ds_scr, dec_scr, b_scr, kt_scr, ket_scr, a_scr):
    n = pl.program_id(0)
    bsz = q_ref.shape[0]
    n_ch = q_ref.shape[1] // GLA_CHUNK

    @pl.when(n == 0)
    def _():
        s_scr[...] = jnp.zeros_like(s_scr)

    c = GLA_CHUNK
    row = lax.broadcasted_iota(jnp.int32, (c, c), 0)
    col = lax.broadcasted_iota(jnp.int32, (c, c), 1)
    causal = row >= col
    tri = causal.astype(F32)
    z = lax.shift_right_logical(n, 30)

    def units():
        for ch in range(n_ch):
            for bi in range(bsz):
                for h in range(GLA_HEADS):
                    yield ((ch * bsz + bi) * GLA_HEADS + h, ch * bsz + bi, bi,
                           slice(ch * c, (ch + 1) * c),
                           slice(h * GLA_DK, (h + 1) * GLA_DK), slice(h * GLA_DV, (h + 1) * GLA_DV))

    for ch in range(n_ch):
        for bi in range(bsz):
            b_scr[z, ch * bsz + bi] = jnp.dot(tri, la_ref[bi, ch * c:(ch + 1) * c, :], precision=HIGHEST,
                                              preferred_element_type=F32)
    for unit, cb, bi, rows, kl, vl in units():
        b = b_scr[z, cb, :, kl]
        b_last = b[c - 1:c, :]
        k = k_ref[bi, rows, kl]
        qt_scr[z, unit] = (q_ref[bi, rows, kl] * jnp.exp(b)).astype(BF16)
        kt_scr[z, unit] = (k * jnp.exp(-b)).astype(BF16)
        ket_scr[z, unit] = (k * jnp.exp(b_last - b)).T.astype(BF16)
        dec_scr[z, unit] = jnp.exp(jnp.broadcast_to(b_last, (GLA_DK, GLA_DK))).T
    for unit, cb, bi, rows, kl, vl in units():
        a = lax.dot_general(qt_scr[z, unit], kt_scr[z, unit], NT_DIMS, preferred_element_type=F32)
        a_scr[z, unit] = jnp.where(causal, a, 0.0).astype(BF16)
    for unit, cb, bi, rows, kl, vl in units():
        vb = v_ref[bi, rows, vl].astype(BF16)
        o_ref[bi, rows, vl] = jnp.dot(a_scr[z, unit], vb, preferred_element_type=F32)
        ds_scr[z, unit] = jnp.dot(ket_scr[z, unit], vb, preferred_element_type=F32)

    for ch in range(n_ch):
        rows = slice(ch * c, (ch + 1) * c)
        for bi in range(bsz):
            for h in range(GLA_HEADS):
                unit = (ch * bsz + bi) * GLA_HEADS + h
                chain = bi * GLA_HEADS + h
                vl = slice(h * GLA_DV, (h + 1) * GLA_DV)
                s = s_scr[chain]
                o_ref[bi, rows, vl] += jnp.dot(qt_scr[z, unit], s.astype(BF16), preferred_element_type=F32)
                decay = dec_scr[z, unit]
                s_scr[chain] = jnp.concatenate([decay, decay], axis=1) * s + ds_scr[z, unit]

    @pl.when(n == pl.num_programs(0) - 1)
    def _():
        sfin_ref[...] = s_scr[...].reshape(sfin_ref.shape)


def _gla_prompt(q, k, v, la):
    bsz, s, _ = q.shape
    rows = GLA_CHUNK * GLA_CHUNKS_PER_STEP
    units = GLA_CHUNKS_PER_STEP * bsz * GLA_HEADS
    qk_spec = pl.BlockSpec((bsz, rows, GLA_HEADS * GLA_DK), lambda n: (0, n, 0))
    v_spec = pl.BlockSpec((bsz, rows, GLA_WIDTH), lambda n: (0, n, 0))
    return pl.pallas_call(
        _gla_prompt_body,
        grid=(s // rows,),
        in_specs=[qk_spec, qk_spec, v_spec, qk_spec],
        out_specs=[v_spec, pl.BlockSpec((bsz, GLA_HEADS, GLA_DK, GLA_DV), lambda n: (0, 0, 0, 0))],
        out_shape=[jax.ShapeDtypeStruct((bsz, s, GLA_WIDTH), F32),
                   jax.ShapeDtypeStruct((bsz, GLA_HEADS, GLA_DK, GLA_DV), F32)],
        scratch_shapes=[pltpu.VMEM((bsz * GLA_HEADS, GLA_DK, GLA_DV), F32),
                        pltpu.VMEM((1, units, GLA_CHUNK, GLA_DK), BF16),
                        pltpu.VMEM((1, units, GLA_DK, GLA_DV), F32),
                        pltpu.VMEM((1, units, GLA_DK, GLA_DK), F32),
                        pltpu.VMEM((1, units // GLA_HEADS, GLA_CHUNK, GLA_HEADS * GLA_DK), F32),
                        pltpu.VMEM((1, units, GLA_CHUNK, GLA_DK), BF16),
                        pltpu.VMEM((1, units, GLA_DK, GLA_CHUNK), BF16),
                        pltpu.VMEM((1, units, GLA_CHUNK, GLA_CHUNK), BF16)],
        compiler_params=_params(1),
        name="gla_prompt",
    )(q, k, v, la)


def _gla_sample_body(q_ref, k_ref, v_ref, la_ref, s0_ref, o_ref, sout_ref, oi_scr, dsum_scr, upd_scr):
    bb, t, _ = q_ref.shape
    rows = bb * t
    q = q_ref[...].reshape(rows, GLA_DK)
    k = k_ref[...].reshape(rows, GLA_DK)
    la = la_ref[...].reshape(rows, GLA_DK)
    v = v_ref[...].reshape(rows, GLA_DV)
    row = lax.broadcasted_iota(jnp.int32, (rows, rows), 0)
    col = lax.broadcasted_iota(jnp.int32, (rows, rows), 1)
    same_seq = (row // t) == (col // t)
    causal = same_seq & (row >= col)
    b = jnp.dot(causal.astype(F32), la, precision=HIGHEST, preferred_element_type=F32)
    b_tot = jnp.dot(same_seq.astype(F32), la, precision=HIGHEST, preferred_element_type=F32)
    q_t = (q * jnp.exp(b)).astype(BF16)
    k_t = (k * jnp.exp(-b)).astype(BF16)
    k_e_t = (k * jnp.exp(b_tot - b)).T.astype(BF16)
    la_t = la.T
    a = lax.dot_general(q_t, k_t, NT_DIMS, preferred_element_type=F32)
    a = jnp.where(causal, a, 0.0).astype(BF16)
    o = jnp.dot(a, v.astype(BF16), preferred_element_type=F32)
    seq_of_row_v = lax.broadcasted_iota(jnp.int32, (rows, GLA_DV), 0) // t
    seq_of_row_k = lax.broadcasted_iota(jnp.int32, (rows, GLA_DK), 0) // t
    z = lax.shift_right_logical(pl.program_id(0), 30)
    for i in range(bb):
        oi_scr[z, i] = jnp.dot(q_t, s0_ref[i].astype(BF16), preferred_element_type=F32)
    for i in range(bb):
        ones_i = jnp.where(seq_of_row_k == i, 1.0, 0.0)
        dsum_scr[z, i] = jnp.dot(la_t, ones_i, precision=HIGHEST, preferred_element_type=F32)
    for i in range(bb):
        v_i = jnp.where(seq_of_row_v == i, v, 0.0).astype(BF16)
        upd_scr[z, i] = jnp.dot(k_e_t, v_i, preferred_element_type=F32)
    for i in range(bb):
        o = o + jnp.where(seq_of_row_v == i, oi_scr[z, i], 0.0)
        decay = jnp.exp(dsum_scr[z, i])
        sout_ref[i] = jnp.concatenate([decay, decay], axis=1) * s0_ref[i] + upd_scr[z, i]
    o_ref[...] = o.reshape(bb, t, GLA_DV)


def _gla_sample(q, k, v, la, s0):
    bsz, t, _ = q.shape
    bb = SAMPLE_BATCH_BLOCK
    qk_spec = pl.BlockSpec((bb, t, GLA_DK), lambda i, h: (i, 0, h))
    v_spec = pl.BlockSpec((bb, t, GLA_DV), lambda i, h: (i, 0, h))
    s_spec = pl.BlockSpec((bb, None, GLA_DK, GLA_DV), lambda i, h: (i, h, 0, 0))
    return pl.pallas_call(
        _gla_sample_body,
        grid=(bsz // bb, GLA_HEADS),
        in_specs=[qk_spec, qk_spec, v_spec, qk_spec, s_spec],
        out_specs=[v_spec, s_spec],
        out_shape=[jax.ShapeDtypeStruct((bsz, t, GLA_WIDTH), F32),
                   jax.ShapeDtypeStruct(s0.shape, F32)],
        scratch_shapes=[pltpu.VMEM((1, bb, bb * t, GLA_DV), F32),
                        pltpu.VMEM((1, bb, GLA_DK, GLA_DK), F32),
                        pltpu.VMEM((1, bb, GLA_DK, GLA_DV), F32)],
        compiler_params=_params(2),
        name="gla_sample",
    )(q, k, v, la, s0)


def _split_kv_lanes(x):
    lane = lax.broadcasted_iota(jnp.int32, x.shape, 1)
    low = lane < SWA_HEAD_DIM
    lo0 = jnp.where(low, x, 0.0)
    hi1 = jnp.where(low, 0.0, x)
    hi0 = pltpu.roll(lo0, SWA_HEAD_DIM, 1)
    lo1 = pltpu.roll(hi1, SWA_HEAD_DIM, 1)
    return (lo0.astype(BF16), lo1.astype(BF16)), (hi0.astype(BF16), hi1.astype(BF16))


def _swa_prompt_body(sink_ref, q_ref, kc_ref, kp_ref, vc_ref, vp_ref, o_ref, st_scr, p_scr):
    n = pl.program_id(1)
    blk = WINDOW
    k_parts = _split_kv_lanes(jnp.concatenate([kp_ref[...], kc_ref[...]], axis=0))
    v_parts = _split_kv_lanes(jnp.concatenate([vp_ref[...], vc_ref[...]], axis=0))
    kj = lax.broadcasted_iota(jnp.int32, (2 * blk, blk), 0)
    qi = lax.broadcasted_iota(jnp.int32, (2 * blk, blk), 1)
    diff = qi + blk - kj
    mask = (diff >= 0) & (diff < WINDOW) & ((n > 0) | (kj >= blk))
    scale = SWA_HEAD_DIM ** -0.5
    contract_rows = (((0,), (0,)), ((), ()))
    z = lax.shift_right_logical(n, 30)
    for head in range(SWA_HEADS):
        pair, half = divmod(head, 2)
        qp = q_ref[:, pair * LANES:(pair + 1) * LANES].astype(BF16)
        st_scr[z, head] = lax.dot_general(k_parts[half][head // SWA_GROUP], qp, NT_DIMS,
                                          preferred_element_type=F32)
    for head in range(SWA_HEADS):
        sink = sink_ref[head]
        st = jnp.where(mask, st_scr[z, head] * scale, NEG_INF)
        m = jnp.maximum(jnp.max(st, axis=0, keepdims=True), sink)
        p = jnp.exp(st - m)
        denom = jnp.sum(p, axis=0, keepdims=True) + jnp.exp(sink - m)
        p_scr[z, head] = (p * (1.0 / denom)).astype(BF16)
    for pair in range(SWA_HEADS // 2):
        kv = (2 * pair) // SWA_GROUP
        o_ref[:, pair * LANES:(pair + 1) * LANES] = (
            lax.dot_general(p_scr[z, 2 * pair], v_parts[0][kv], contract_rows, preferred_element_type=F32)
            + lax.dot_general(p_scr[z, 2 * pair + 1], v_parts[1][kv], contract_rows,
                              preferred_element_type=F32))


def _swa_prompt(sq, sk, sv, sinks):
    bsz, s, _ = sq.shape
    blk = WINDOW
    cur = lambda b, n: (b, n, 0)
    prev = lambda b, n: (b, jnp.maximum(n - 1, 0), 0)
    kv_cur = pl.BlockSpec((None, blk, LANES), cur)
    kv_prev = pl.BlockSpec((None, blk, LANES), prev)
    q_spec = pl.BlockSpec((None, blk, SWA_WIDTH), cur)
    return pl.pallas_call(
        _swa_prompt_body,
        grid=(bsz, s // blk),
        in_specs=[pl.BlockSpec(memory_space=pltpu.SMEM), q_spec, kv_cur, kv_prev, kv_cur, kv_prev],
        out_specs=q_spec,
        out_shape=jax.ShapeDtypeStruct((bsz, s, SWA_WIDTH), F32),
        scratch_shapes=[pltpu.VMEM((1, SWA_HEADS, 2 * blk, blk), F32),
                        pltpu.VMEM((1, SWA_HEADS, 2 * blk, blk), BF16)],
        compiler_params=_params(2),
        name="swa_prompt",
    )(sinks, sq, sk, sk, sv, sv)


def _swa_sample_body(t_new, sink_ref, q_ref, ck_ref, cv_ref, kn_ref, vn_ref, o_ref, nk_ref, nv_ref,
                     st_scr, p_scr):
    bb, rows, _ = q_ref.shape
    w = ck_ref.shape[1]
    pad = jnp.zeros((BF16_SUBLANES - t_new, LANES), F32)
    n_keys = w + BF16_SUBLANES
    c = lax.broadcasted_iota(jnp.int32, (n_keys, rows), 0)
    tq = lax.broadcasted_iota(jnp.int32, (n_keys, rows), 1) // SWA_HEADS
    mask = ((c < w) & (c > tq)) | ((c >= w) & (c - w <= tq) & (c < w + t_new))
    sink = sink_ref[...]
    scale = SWA_HEAD_DIM ** -0.5
    contract_rows = (((0,), (0,)), ((), ()))
    z = lax.shift_right_logical(pl.program_id(0), 30)
    for i in range(bb):
        kcat = jnp.concatenate([ck_ref[i], kn_ref[i], pad], axis=0).astype(BF16)
        st_scr[z, i] = lax.dot_general(kcat, q_ref[i].astype(BF16), NT_DIMS, preferred_element_type=F32)
    for i in range(bb):
        st = jnp.where(mask, st_scr[z, i] * scale, NEG_INF)
        m = jnp.maximum(jnp.max(st, axis=0, keepdims=True), sink)
        p = jnp.exp(st - m)
        denom = jnp.sum(p, axis=0, keepdims=True) + jnp.exp(sink - m)
        p_scr[z, i] = (p * (1.0 / denom)).astype(BF16)
    for i in range(bb):
        vcat = jnp.concatenate([cv_ref[i], vn_ref[i], pad], axis=0).astype(BF16)
        o_ref[i] = lax.dot_general(p_scr[z, i], vcat, contract_rows, preferred_element_type=F32)
        nk_ref[i, 0:w - t_new, :] = ck_ref[i, t_new:, :]
        nk_ref[i, w - t_new:w, :] = kn_ref[i]
        nv_ref[i, 0:w - t_new, :] = cv_ref[i, t_new:, :]
        nv_ref[i, w - t_new:w, :] = vn_ref[i]


def _swa_sample(q_pad, ck, cv, kn, vn, sink_row):
    bsz, rows, _ = q_pad.shape
    t_new = kn.shape[1]
    w = ck.shape[1]
    bb = SAMPLE_BATCH_BLOCK
    n_keys = w + BF16_SUBLANES
    blk = lambda r: pl.BlockSpec((bb, r, LANES), lambda i: (i, 0, 0))
    return pl.pallas_call(
        functools.partial(_swa_sample_body, t_new),
        grid=(bsz // bb,),
        in_specs=[_resident(sink_row.shape), blk(rows), blk(w), blk(w), blk(t_new), blk(t_new)],
        out_specs=[blk(rows), blk(w), blk(w)],
        out_shape=[jax.ShapeDtypeStruct((bsz, rows, LANES), F32),
                   jax.ShapeDtypeStruct(ck.shape, F32), jax.ShapeDtypeStruct(cv.shape, F32)],
        scratch_shapes=[pltpu.VMEM((1, bb, n_keys, rows), F32),
                        pltpu.VMEM((1, bb, n_keys, rows), BF16)],
        compiler_params=_params(1),
        name="swa_sample",
    )(sink_row, q_pad, ck, cv, kn, vn)


def _merge_body(og_ref, r_ref, os_ref, x_ref, gn_ref, wog_ref, wos_ref, gf_ref, wq_ref, keys_ref,
                h_ref, hn_ref, st_ref):
    og = og_ref[...]
    gn = gn_ref[...]
    heads = [_rms(og[:, i * GLA_DV:(i + 1) * GLA_DV], gn) for i in range(GLA_HEADS)]
    r = r_ref[...]
    gated = jnp.concatenate(heads, axis=1) * (r / (1.0 + jnp.exp(-r)))
    h = (x_ref[...]
         + jnp.dot(gated.astype(BF16), wog_ref[...], preferred_element_type=F32)
         + jnp.dot(os_ref[...].astype(BF16), wos_ref[...], preferred_element_type=F32))
    h_ref[...] = h
    hn_f32 = _rms(h, gf_ref[...])
    hn = hn_f32.astype(BF16)
    hn_ref[...] = hn_f32.T.astype(BF16)
    pq = jnp.dot(hn, wq_ref[...], preferred_element_type=F32).astype(BF16)
    for i in range(PEER_HEADS):
        st_ref[i] = lax.dot_general(keys_ref[i], pq[:, i * PEER_QDIM:(i + 1) * PEER_QDIM], NT_DIMS,
                                    preferred_element_type=F32)


def _merge(o_gla, r, o_swa, x, gn, wog, wos, gf, wq, keys):
    t, d = x.shape
    tm = MERGE_ROWS
    row = lambda i: (i, 0)
    return pl.pallas_call(
        _merge_body,
        grid=(t // tm,),
        in_specs=[pl.BlockSpec((tm, GLA_WIDTH), row), pl.BlockSpec((tm, GLA_WIDTH), row),
                  pl.BlockSpec((tm, o_swa.shape[1]), row), pl.BlockSpec((tm, d), row),
                  _resident(gn.shape), _resident(wog.shape), _resident(wos.shape),
                  _resident(gf.shape), _resident(wq.shape), _resident(keys.shape)],
        out_specs=[pl.BlockSpec((tm, d), row), pl.BlockSpec((d, tm), lambda i: (0, i)),
                   pl.BlockSpec((PEER_HEADS, 2 * PEER_N_KEYS, tm), lambda i: (0, 0, i))],
        out_shape=[jax.ShapeDtypeStruct((t, d), F32), jax.ShapeDtypeStruct((d, t), BF16),
                   jax.ShapeDtypeStruct((PEER_HEADS, 2 * PEER_N_KEYS, t), F32)],
        compiler_params=_params(1),
        name="merge",
    )(o_gla, r, o_swa, x, gn, wog, wos, gf, wq, keys)


N_SORTED = PEER_TOPK + 1


def _sorting_network(n):
    pairs = []
    p = 1
    while p < n:
        k = p
        while k >= 1:
            for j in range(k % p, n - k, 2 * k):
                for i in range(min(k, n - j - k)):
                    if (i + j) // (2 * p) == (i + j + k) // (2 * p):
                        pairs.append((i + j, i + j + k))
            k //= 2
        p *= 2
    return pairs


def _largest(vals, n_out):
    n = 1
    while n < len(vals):
        n *= 2
    wires = list(vals) + [None] * (n - len(vals))
    needed = set(range(n_out))
    kept = []
    for i, j in reversed(_sorting_network(n)):
        if i in needed or j in needed:
            needed.update((i, j))
            kept.append((i, j))
    for i, j in reversed(kept):
        hi, lo = wires[i], wires[j]
        if lo is None:
            continue
        if hi is None:
            wires[i], wires[j] = lo, None
        else:
            wires[i], wires[j] = jnp.maximum(hi, lo), jnp.minimum(hi, lo)
    return wires[:n_out]


def _topk_body(st_ref, aux_ref, top0_scr, top1_scr):
    scr = (top0_scr, top1_scr)
    sublanes = 8
    n_tiles = PEER_N_KEYS // sublanes
    network = _sorting_network(n_tiles)
    for h in range(PEER_HEADS):
        for half in range(2):
            base = half * PEER_N_KEYS
            col = [st_ref[h, base + k * sublanes:base + (k + 1) * sublanes, :] for k in range(n_tiles)]
            for i, j in network:
                col[i], col[j] = jnp.maximum(col[i], col[j]), jnp.minimum(col[i], col[j])
            for a in range(N_SORTED):
                m = col[0]
                for shift in (4, 2, 1):
                    m = jnp.maximum(m, pltpu.roll(m, shift, 0))
                scr[half][a, h:h + 1, :] = m[0:1, :]
                if a + 1 < N_SORTED:
                    hit = col[0] == m
                    for k in range(min(n_tiles, N_SORTED - 1 - a)):
                        below = col[k + 1] if k + 1 < n_tiles else NEG_INF
                        col[k] = jnp.where(hit, below, col[k])
    top0 = [top0_scr[a] for a in range(N_SORTED)]
    top1 = [top1_scr[a] for a in range(N_SORTED)]
    cands = [top0[a - 1] + top1[b - 1]
             for a in range(1, N_SORTED + 1) for b in range(1, N_SORTED // a + 1)]
    best = _largest(cands, N_SORTED)
    z = jnp.ones_like(best[0])
    for rank in range(1, PEER_TOPK):
        z = z + jnp.exp(best[rank] - best[0])
    aux_ref[0] = 0.5 * (best[PEER_TOPK - 1] + best[PEER_TOPK])
    aux_ref[1] = top0[0]
    aux_ref[2] = top1[0]
    aux_ref[3] = 1.0 / z


def _topk(st):
    t = st.shape[2]
    tt = TOPK_TOKENS
    return pl.pallas_call(
        _topk_body,
        grid=(t // tt,),
        in_specs=[pl.BlockSpec((PEER_HEADS, 2 * PEER_N_KEYS, tt), lambda i: (0, 0, i))],
        out_specs=pl.BlockSpec((4, PEER_HEADS, tt), lambda i: (0, 0, i)),
        out_shape=jax.ShapeDtypeStruct((4, PEER_HEADS, t), F32),
        scratch_shapes=[pltpu.VMEM((N_SORTED, PEER_HEADS, tt), F32),
                        pltpu.VMEM((N_SORTED, PEER_HEADS, tt), F32)],
        compiler_params=_params(1),
        name="peer_topk",
    )(st)


def _peer_body(st_ref, aux_ref, hnt_ref, u_ref, vt_ref, h_ref, gfin_ref, out_ref,
               thr_scr, w0_scr, e1_scr, thr_cur, w0_cur, acc_scr, *piece_scr):
    ec = pl.program_id(1)
    n_rows = PEER_CHUNK // PEER_N_KEYS
    tb = hnt_ref.shape[1]
    a_scr = piece_scr[:PEER_PARTS]
    c_scr = piece_scr[PEER_PARTS:]

    @pl.when(ec == 0)
    def _():
        grid_rows = (PEER_N_KEYS // n_rows, n_rows, tb)
        for h in range(PEER_HEADS):
            s0 = st_ref[h, 0:PEER_N_KEYS, :]
            s1 = st_ref[h, PEER_N_KEYS:2 * PEER_N_KEYS, :]
            thr_scr[h] = jnp.exp(aux_ref[0, h:h + 1, :] - s0 - aux_ref[2, h:h + 1, :]).reshape(grid_rows)
            w0_scr[h] = (jnp.exp(s0 - aux_ref[1, h:h + 1, :])
                         * (0.5 * aux_ref[3, h:h + 1, :])).reshape(grid_rows)
            e1_scr[h] = jnp.exp(s1 - aux_ref[2, h:h + 1, :])
        acc_scr[...] = jnp.zeros_like(acc_scr)

    sub = 2 * BF16_SUBLANES
    inv_sqrt2 = 1.0 / math.sqrt(2.0)
    e_part = PEER_CHUNK // PEER_PARTS
    rows_per_part = n_rows // PEER_PARTS

    z = lax.shift_right_logical(ec, 30)

    for h in range(PEER_HEADS):
        thr_cur[h] = thr_scr[h, ec]
        w0_cur[h] = w0_scr[h, ec]

    def scores(p):
        a_scr[p][z] = jnp.dot(u_ref[p * e_part:(p + 1) * e_part, :], hnt_ref[...],
                              preferred_element_type=F32)

    def coefficients(p):
        for r in range(rows_per_part):
            ii = p * rows_per_part + r
            for lt in range(tb // LANES):
                ls = slice(lt * LANES, (lt + 1) * LANES)
                thr_b = [thr_cur[h, ii:ii + 1, ls] for h in range(PEER_HEADS)]
                w0_b = [w0_cur[h, ii:ii + 1, ls] for h in range(PEER_HEADS)]
                for jb in range(PEER_N_KEYS // sub):
                    rows = slice(r * PEER_N_KEYS + jb * sub, r * PEER_N_KEYS + (jb + 1) * sub)
                    a = a_scr[p][z, rows, ls]
                    act = a * (1.0 + lax.erf(a * inv_sqrt2))
                    g = None
                    for h in range(PEER_HEADS):
                        e1 = e1_scr[h, jb * sub:(jb + 1) * sub, ls]
                        term = jnp.where(e1 >= thr_b[h], e1, 0.0) * w0_b[h]
                        g = term if g is None else g + term
                    c_scr[p][z, rows, ls] = (g * act).astype(BF16)

    def values(p):
        acc_scr[z] += jnp.dot(vt_ref[p], c_scr[p][z], preferred_element_type=F32)

    for p in range(PEER_PARTS):
        scores(p)
    for p in range(PEER_PARTS):
        coefficients(p)
        values(p)

    @pl.when(ec == pl.num_programs(1) - 1)
    def _():
        out_ref[...] = _rms(h_ref[...] + acc_scr[0].T, gfin_ref[...])


def _peer(st, aux, hnt, u, vt, h, g_final):
    d, t = hnt.shape
    tb = PEER_TOKENS
    ec = PEER_CHUNK
    n_chunks = u.shape[0] // ec
    rows = ec // PEER_N_KEYS
    return pl.pallas_call(
        _peer_body,
        grid=(t // tb, n_chunks),
        in_specs=[pl.BlockSpec((PEER_HEADS, 2 * PEER_N_KEYS, tb), lambda i, j: (0, 0, i),
                               pipeline_mode=pl.Buffered(1)),
                  pl.BlockSpec((4, PEER_HEADS, tb), lambda i, j: (0, 0, i)),
                  pl.BlockSpec((d, tb), lambda i, j: (0, i), pipeline_mode=pl.Buffered(1)),
                  pl.BlockSpec((ec, d), lambda i, j: (j, 0)),
                  pl.BlockSpec((PEER_PARTS, d, ec // PEER_PARTS), lambda i, j: (j, 0, 0)),
                  pl.BlockSpec((tb, d), lambda i, j: (i, 0), pipeline_mode=pl.Buffered(1)),
                  _resident(g_final.shape)],
        out_specs=pl.BlockSpec((tb, d), lambda i, j: (i, 0)),
        out_shape=jax.ShapeDtypeStruct((t, d), F32),
        scratch_shapes=([pltpu.VMEM((PEER_HEADS, n_chunks, rows, tb), F32),
                         pltpu.VMEM((PEER_HEADS, n_chunks, rows, tb), F32),
                         pltpu.VMEM((PEER_HEADS, PEER_N_KEYS, tb), F32),
                         pltpu.VMEM((PEER_HEADS, rows, tb), F32),
                         pltpu.VMEM((PEER_HEADS, rows, tb), F32),
                         pltpu.VMEM((1, d, tb), F32)]
                        + [pltpu.VMEM((1, ec // PEER_PARTS, tb), F32)] * PEER_PARTS
                        + [pltpu.VMEM((1, ec // PEER_PARTS, tb), BF16)] * PEER_PARTS),
        compiler_params=_params(2),
        name="peer_dense",
    )(st, aux, hnt, u, vt, h, g_final)


def _rope_tables(pos):
    half = SWA_HEAD_DIM // 2
    inv = jnp.exp(-math.log(ROPE_THETA) * jnp.arange(half, dtype=F32) * (2.0 / SWA_HEAD_DIM))
    ang = pos.astype(F32)[:, None] * inv[None, :]
    cos, sin = jnp.cos(ang), jnp.sin(ang)
    reps = LANES // SWA_HEAD_DIM
    return (jnp.tile(jnp.concatenate([cos, cos], axis=1), (1, reps)),
            jnp.tile(jnp.concatenate([-sin, sin], axis=1), (1, reps)))


def _pad_heads_to_kv_half(w, axis):
    shape = w.shape
    w = w.reshape(shape[:axis] + (SWA_KV_HEADS, SWA_GROUP, SWA_HEAD_DIM) + shape[axis + 1:])
    parts = []
    for kv in range(SWA_KV_HEADS):
        heads = lax.index_in_dim(w, kv, axis, keepdims=False)
        pads = [(0, 0)] * heads.ndim
        pads[axis + 1] = (kv * SWA_HEAD_DIM, (SWA_KV_HEADS - 1 - kv) * SWA_HEAD_DIM)
        parts.append(jnp.pad(heads, pads))
    out = jnp.concatenate(parts, axis=axis)
    return out.reshape(shape[:axis] + (SWA_HEADS * LANES,) + shape[axis + 1:])


def kernel(x_prompt, x_sample, state_gla, cache_win_k, cache_win_v, g_attn, w_in, w_gate_up, b_gate,
           g_gla_out, attn_sinks, w_out, g_ffn, w_peer_q, peer_sub_keys, peer_u, peer_v, g_final):
    bsz, seq, d = x_prompt.shape
    dbsz, dseq, _ = x_sample.shape
    win = cache_win_k.shape[2]

    n_gla = 2 * GLA_HEADS * GLA_DK + 2 * GLA_WIDTH
    o_sq = n_gla + GLA_GATE_RANK
    w_gla = w_in[0, :, :n_gla].astype(BF16)
    ga = jnp.pad(w_in[0, :, n_gla:o_sq], ((0, 0), (0, LANES - GLA_GATE_RANK)))
    wsq = w_in[0, :, o_sq:o_sq + SWA_WIDTH]
    wkv = w_in[0, :, o_sq + SWA_WIDTH:]
    w_swa_prompt = jnp.concatenate([wsq, wkv, ga], axis=1).astype(BF16)
    w_swa_sample = jnp.concatenate([_pad_heads_to_kv_half(wsq, 1), wkv, ga], axis=1).astype(BF16)
    wgu = jnp.pad(w_gate_up[0], ((0, LANES - GLA_GATE_RANK), (0, 0)))
    bg = b_gate[0][None, :]
    g_a = g_attn[0][None, :]
    g_f = g_ffn[0][None, :]
    g_n = g_gla_out[0][None, :]
    wo_gla = w_out[0, :GLA_WIDTH].astype(BF16)
    wo_swa = w_out[0, GLA_WIDTH:]
    wo_swa_pad = _pad_heads_to_kv_half(wo_swa, 0).astype(BF16)
    wo_swa = wo_swa.astype(BF16)
    wq = w_peer_q[0].astype(BF16)
    sk0 = jnp.pad(peer_sub_keys[0, :, 0], ((0, 0), (0, 0), (0, PEER_HALF)))
    sk1 = jnp.pad(peer_sub_keys[0, :, 1], ((0, 0), (0, 0), (PEER_HALF, 0)))
    keys = jnp.concatenate([sk0, sk1], axis=1).astype(BF16)
    u = peer_u[0].astype(BF16)
    e_piece = PEER_CHUNK // PEER_PARTS
    vt = peer_v[0].reshape(PEER_EXPERTS // e_piece, e_piece, d).transpose(0, 2, 1).astype(BF16)
    sinks = attn_sinks[0]
    sink_row = jnp.tile(sinks, dseq)[None, :]

    cos_p, sin_p = _rope_tables(jnp.arange(seq))
    cos_s, sin_s = _rope_tables(PAST_LEN + jnp.arange(dseq))
    cos_s = jnp.tile(cos_s, (PROJ_ROWS // dseq, 1))
    sin_s = jnp.tile(sin_s, (PROJ_ROWS // dseq, 1))

    def ffn(h, hn, st):
        aux = _topk(st)
        return _peer(st, aux, hn, u, vt, h, g_final[None, :])

    tp = bsz * seq
    q, k, v, r, la, sq, sk, sv = _proj(x_prompt.reshape(tp, d), g_a, w_gla, w_swa_prompt, wgu, bg,
                                       cos_p, sin_p)
    b3 = lambda a: a.reshape(bsz, seq, a.shape[-1])
    o_gla, s_prompt = _gla_prompt(b3(q), b3(k), b3(v), b3(la))
    o_swa = _swa_prompt(b3(sq), b3(sk), b3(sv), sinks)
    h, hn, st = _merge(o_gla.reshape(tp, GLA_WIDTH), r, o_swa.reshape(tp, SWA_WIDTH),
                       x_prompt.reshape(tp, d), g_n, wo_gla, wo_swa, g_f, wq, keys)
    y_prompt = ffn(h, hn, st).reshape(bsz, seq, d)
    kv_shape = (1, bsz, WINDOW, SWA_KV_HEADS, SWA_HEAD_DIM)
    k_prompt = b3(sk)[:, seq - WINDOW:].reshape(kv_shape)
    v_prompt = b3(sv)[:, seq - WINDOW:].reshape(kv_shape)

    ts = dbsz * dseq
    q, k, v, r, la, sq, sk, sv = _proj(x_sample.reshape(ts, d), g_a, w_gla, w_swa_sample, wgu, bg,
                                       cos_s, sin_s)
    d3 = lambda a: a.reshape(dbsz, dseq, a.shape[-1])
    o_gla, s_sample = _gla_sample(d3(q), d3(k), d3(v), d3(la), state_gla[0])
    o_swa, k_sample, v_sample = _swa_sample(
        sq.reshape(dbsz, dseq * SWA_HEADS, LANES),
        cache_win_k[0].reshape(dbsz, win, LANES), cache_win_v[0].reshape(dbsz, win, LANES),
        d3(sk), d3(sv), sink_row)
    h, hn, st = _merge(o_gla.reshape(ts, GLA_WIDTH), r, o_swa.reshape(ts, SWA_HEADS * LANES),
                       x_sample.reshape(ts, d), g_n, wo_gla, wo_swa_pad, g_f, wq, keys)
    y_sample = ffn(h, hn, st).reshape(dbsz, dseq, d)
    ckv_shape = (1, dbsz, win, SWA_KV_HEADS, SWA_HEAD_DIM)

    return (y_prompt, y_sample, s_prompt[None], k_prompt, v_prompt, s_sample[None],
            k_sample.reshape(ckv_shape), v_sample.reshape(ckv_shape))
```

```python
import functools
import math

import jax
import jax.numpy as jnp
from jax import lax
from jax.experimental import pallas as pl
from jax.experimental.pallas import tpu as pltpu

F32 = jnp.float32
BF16 = jnp.bfloat16
HIGHEST = lax.Precision.HIGHEST

GLA_HEADS = 4
GLA_DK = 128
GLA_DV = 256
GLA_WIDTH = GLA_HEADS * GLA_DV
GLA_GATE_RANK = 16
GLA_TAU = 16.0
GLA_CHUNK = 64
SWA_HEAD_DIM = 64
SWA_HEADS = 16
SWA_KV_HEADS = 2
SWA_GROUP = SWA_HEADS // SWA_KV_HEADS
SWA_WIDTH = SWA_HEADS * SWA_HEAD_DIM
WINDOW = 128
ROPE_THETA = 10000.0
PAST_LEN = 8192
PEER_N_KEYS = 128
PEER_EXPERTS = PEER_N_KEYS * PEER_N_KEYS
PEER_HEADS = 8
PEER_QDIM = 128
PEER_HALF = PEER_QDIM // 2
PEER_TOPK = 16
EPS = 1e-6
NEG_INF = -1e30

LANES = 128
BF16_SUBLANES = 16
PROJ_ROWS = 512
MERGE_ROWS = 256
GLA_CHUNKS_PER_STEP = 8
TOPK_TOKENS = 256
PEER_TOKENS = 512
PEER_CHUNK = 1024
PEER_PARTS = 2
SAMPLE_BATCH_BLOCK = 16
VMEM_LIMIT = 60 * 1024 * 1024

NT_DIMS = (((1,), (1,)), ((), ()))


def _resident(shape):
    nd = len(shape)
    return pl.BlockSpec(shape, lambda *_: (0,) * nd, pipeline_mode=pl.Buffered(1))


def _params(n_axes):
    return pltpu.CompilerParams(dimension_semantics=("arbitrary",) * n_axes,
                                vmem_limit_bytes=VMEM_LIMIT)


def _rms(x, g):
    return x * lax.rsqrt(jnp.mean(x * x, axis=-1, keepdims=True) + EPS) * g


def _proj_body(x_ref, g_ref, wg_ref, ws_ref, wgu_ref, bg_ref, cos_ref, sin_ref,
               q_ref, k_ref, v_ref, r_ref, la_ref, sq_ref, sk_ref, sv_ref):
    xn = _rms(x_ref[...], g_ref[...]).astype(BF16)
    wsq = sq_ref.shape[1]

    def proj(w_ref, start, width):
        return jnp.dot(xn, w_ref[:, start:start + width], preferred_element_type=F32)

    nk = GLA_HEADS * GLA_DK
    q_ref[...] = proj(wg_ref, 0, nk) * (GLA_DK ** -0.5)
    k_ref[...] = proj(wg_ref, nk, nk)
    v_ref[...] = proj(wg_ref, 2 * nk, GLA_WIDTH)
    r_ref[...] = proj(wg_ref, 2 * nk + GLA_WIDTH, GLA_WIDTH)

    sq_all = proj(ws_ref, 0, wsq)
    tail = proj(ws_ref, wsq, 3 * LANES)
    ga = tail[:, 2 * LANES:]
    z = jnp.dot(ga, wgu_ref[...], precision=HIGHEST, preferred_element_type=F32) + bg_ref[...]
    la_ref[...] = (jnp.minimum(z, 0.0) - jnp.log1p(jnp.exp(-jnp.abs(z)))) * (1.0 / GLA_TAU)

    cos = cos_ref[...]
    sin = sin_ref[...]
    lane = lax.broadcasted_iota(jnp.int32, cos.shape, 1)
    first_half = (lane & (SWA_HEAD_DIM // 2)) == 0

    def rope(t):
        rot = jnp.where(first_half, pltpu.roll(t, LANES - SWA_HEAD_DIM // 2, 1),
                        pltpu.roll(t, SWA_HEAD_DIM // 2, 1))
        return t * cos + rot * sin

    for c in range(wsq // LANES):
        sq_ref[:, c * LANES:(c + 1) * LANES] = rope(sq_all[:, c * LANES:(c + 1) * LANES])
    sk_ref[...] = rope(tail[:, :LANES])
    sv_ref[...] = tail[:, LANES:2 * LANES]


def _proj(x, g, w_gla, w_swa, wgu, bg, cos, sin):
    t, d = x.shape
    tm = PROJ_ROWS
    n_tab = cos.shape[0] // tm
    row = lambda i: (i, 0)
    tab = lambda i: (i % n_tab, 0)
    wsq = w_swa.shape[1] - 3 * LANES
    nk = GLA_HEADS * GLA_DK
    widths = (nk, nk, GLA_WIDTH, GLA_WIDTH, nk, wsq, LANES, LANES)
    return pl.pallas_call(
        _proj_body,
        grid=(t // tm,),
        in_specs=[pl.BlockSpec((tm, d), row), _resident(g.shape), _resident(w_gla.shape),
                  _resident(w_swa.shape), _resident(wgu.shape), _resident(bg.shape),
                  pl.BlockSpec((tm, LANES), tab), pl.BlockSpec((tm, LANES), tab)],
        out_specs=[pl.BlockSpec((tm, n), row) for n in widths],
        out_shape=[jax.ShapeDtypeStruct((t, n), F32) for n in widths],
        compiler_params=_params(1),
        name="proj",
    )(x, g, w_gla, w_swa, wgu, bg, cos, sin)


def _gla_prompt_body(q_ref, k_ref, v_ref, la_ref, o_ref, sfin_ref,
                     s_scr, qt_scr, ds_scr, dec_scr, b_scr, kt_scr, ket_scr, a_scr):
    n = pl.program_id(0)
    bsz = q_ref.shape[0]
    n_ch = q_ref.shape[1] // GLA_CHUNK

    @pl.when(n == 0)
    def _():
        s_scr[...] = jnp.zeros_like(s_scr)

    c = GLA_CHUNK
    row = lax.broadcasted_iota(jnp.int32, (c, c), 0)
    col = lax.broadcasted_iota(jnp.int32, (c, c), 1)
    causal = row >= col
    tri = causal.astype(F32)
    z = lax.shift_right_logical(n, 30)

    def units():
        for ch in range(n_ch):
            for bi in range(bsz):
                for h in range(GLA_HEADS):
                    yield ((ch * bsz + bi) * GLA_HEADS + h, ch * bsz + bi, bi,
                           slice(ch * c, (ch + 1) * c),
                           slice(h * GLA_DK, (h + 1) * GLA_DK), slice(h * GLA_DV, (h + 1) * GLA_DV))

    for ch in range(n_ch):
        for bi in range(bsz):
            b_scr[z, ch * bsz + bi] = jnp.dot(tri, la_ref[bi, ch * c:(ch + 1) * c, :], precision=HIGHEST,
                                              preferred_element_type=F32)
    for unit, cb, bi, rows, kl, vl in units():
        b = b_scr[z, cb, :, kl]
        b_last = b[c - 1:c, :]
        k = k_ref[bi, rows, kl]
        qt_scr[z, unit] = (q_ref[bi, rows, kl] * jnp.exp(b)).astype(BF16)
        kt_scr[z, unit] = (k * jnp.exp(-b)).astype(BF16)
        ket_scr[z, unit] = (k * jnp.exp(b_last - b)).T.astype(BF16)
        dec_scr[z, unit] = jnp.exp(jnp.broadcast_to(b_last, (GLA_DK, GLA_DK))).T
    for unit, cb, bi, rows, kl, vl in units():
        a = lax.dot_general(qt_scr[z, unit], kt_scr[z, unit], NT_DIMS, preferred_element_type=F32)
        a_scr[z, unit] = jnp.where(causal, a, 0.0).astype(BF16)
    for unit, cb, bi, rows, kl, vl in units():
        vb = v_ref[bi, rows, vl].astype(BF16)
        o_ref[bi, rows, vl] = jnp.dot(a_scr[z, unit], vb, preferred_element_type=F32)
        ds_scr[z, unit] = jnp.dot(ket_scr[z, unit], vb, preferred_element_type=F32)

    for ch in range(n_ch):
        rows = slice(ch * c, (ch + 1) * c)
        for bi in range(bsz):
            for h in range(GLA_HEADS):
                unit = (ch * bsz + bi) * GLA_HEADS + h
                chain = bi * GLA_HEADS + h
                vl = slice(h * GLA_DV, (h + 1) * GLA_DV)
                s = s_scr[chain]
                o_ref[bi, rows, vl] += jnp.dot(qt_scr[z, unit], s.astype(BF16), preferred_element_type=F32)
                decay = dec_scr[z, unit]
                s_scr[chain] = jnp.concatenate([decay, decay], axis=1) * s + ds_scr[z, unit]

    @pl.when(n == pl.num_programs(0) - 1)
    def _():
        sfin_ref[...] = s_scr[...].reshape(sfin_ref.shape)


def _gla_prompt(q, k, v, la):
    bsz, s, _ = q.shape
    rows = GLA_CHUNK * GLA_CHUNKS_PER_STEP
    units = GLA_CHUNKS_PER_STEP * bsz * GLA_HEADS
    qk_spec = pl.BlockSpec((bsz, rows, GLA_HEADS * GLA_DK), lambda n: (0, n, 0))
    v_spec = pl.BlockSpec((bsz, rows, GLA_WIDTH), lambda n: (0, n, 0))
    return pl.pallas_call(
        _gla_prompt_body,
        grid=(s // rows,),
        in_specs=[qk_spec, qk_spec, v_spec, qk_spec],
        out_specs=[v_spec, pl.BlockSpec((bsz, GLA_HEADS, GLA_DK, GLA_DV), lambda n: (0, 0, 0, 0))],
        out_shape=[jax.ShapeDtypeStruct((bsz, s, GLA_WIDTH), F32),
                   jax.ShapeDtypeStruct((bsz, GLA_HEADS, GLA_DK, GLA_DV), F32)],
        scratch_shapes=[pltpu.VMEM((bsz * GLA_HEADS, GLA_DK, GLA_DV), F32),
                        pltpu.VMEM((1, units, GLA_CHUNK, GLA_DK), BF16),
                        pltpu.VMEM((1, units, GLA_DK, GLA_DV), F32),
                        pltpu.VMEM((1, units, GLA_DK, GLA_DK), F32),
                        pltpu.VMEM((1, units // GLA_HEADS, GLA_CHUNK, GLA_HEADS * GLA_DK), F32),
                        pltpu.VMEM((1, units, GLA_CHUNK, GLA_DK), BF16),
                        pltpu.VMEM((1, units, GLA_DK, GLA_CHUNK), BF16),
                        pltpu.VMEM((1, units, GLA_CHUNK, GLA_CHUNK), BF16)],
        compiler_params=_params(1),
        name="gla_prompt",
    )(q, k, v, la)


def _gla_sample_body(q_ref, k_ref, v_ref, la_ref, s0_ref, o_ref, sout_ref, oi_scr, dsum_scr, upd_scr):
    bb, t, _ = q_ref.shape
    rows = bb * t
    q = q_ref[...].reshape(rows, GLA_DK)
    k = k_ref[...].reshape(rows, GLA_DK)
    la = la_ref[...].reshape(rows, GLA_DK)
    v = v_ref[...].reshape(rows, GLA_DV)
    row = lax.broadcasted_iota(jnp.int32, (rows, rows), 0)
    col = lax.broadcasted_iota(jnp.int32, (rows, rows), 1)
    same_seq = (row // t) == (col // t)
    causal = same_seq & (row >= col)
    b = jnp.dot(causal.astype(F32), la, precision=HIGHEST, preferred_element_type=F32)
    b_tot = jnp.dot(same_seq.astype(F32), la, precision=HIGHEST, preferred_element_type=F32)
    q_t = (q * jnp.exp(b)).astype(BF16)
    k_t = (k * jnp.exp(-b)).astype(BF16)
    k_e_t = (k * jnp.exp(b_tot - b)).T.astype(BF16)
    la_t = la.T
    a = lax.dot_general(q_t, k_t, NT_DIMS, preferred_element_type=F32)
    a = jnp.where(causal, a, 0.0).astype(BF16)
    o = jnp.dot(a, v.astype(BF16), preferred_element_type=F32)
    seq_of_row_v = lax.broadcasted_iota(jnp.int32, (rows, GLA_DV), 0) // t
    seq_of_row_k = lax.broadcasted_iota(jnp.int32, (rows, GLA_DK), 0) // t
    z = lax.shift_right_logical(pl.program_id(0), 30)
    for i in range(bb):
        oi_scr[z, i] = jnp.dot(q_t, s0_ref[i].astype(BF16), preferred_element_type=F32)
    for i in range(bb):
        ones_i = jnp.where(seq_of_row_k == i, 1.0, 0.0)
        dsum_scr[z, i] = jnp.dot(la_t, ones_i, precision=HIGHEST, preferred_element_type=F32)
    for i in range(bb):
        v_i = jnp.where(seq_of_row_v == i, v, 0.0).astype(BF16)
        upd_scr[z, i] = jnp.dot(k_e_t, v_i, preferred_element_type=F32)
    for i in range(bb):
        o = o + jnp.where(seq_of_row_v == i, oi_scr[z, i], 0.0)
        decay = jnp.exp(dsum_scr[z, i])
        sout_ref[i] = jnp.concatenate([decay, decay], axis=1) * s0_ref[i] + upd_scr[z, i]
    o_ref[...] = o.reshape(bb, t, GLA_DV)


def _gla_sample(q, k, v, la, s0):
    bsz, t, _ = q.shape
    bb = SAMPLE_BATCH_BLOCK
    qk_spec = pl.BlockSpec((bb, t, GLA_DK), lambda i, h: (i, 0, h))
    v_spec = pl.BlockSpec((bb, t, GLA_DV), lambda i, h: (i, 0, h))
    s_spec = pl.BlockSpec((bb, None, GLA_DK, GLA_DV), lambda i, h: (i, h, 0, 0))
    return pl.pallas_call(
        _gla_sample_body,
        grid=(bsz // bb, GLA_HEADS),
        in_specs=[qk_spec, qk_spec, v_spec, qk_spec, s_spec],
        out_specs=[v_spec, s_spec],
        out_shape=[jax.ShapeDtypeStruct((bsz, t, GLA_WIDTH), F32),
                   jax.ShapeDtypeStruct(s0.shape, F32)],
        scratch_shapes=[pltpu.VMEM((1, bb, bb * t, GLA_DV), F32),
                        pltpu.VMEM((1, bb, GLA_DK, GLA_DK), F32),
                        pltpu.VMEM((1, bb, GLA_DK, GLA_DV), F32)],
        compiler_params=_params(2),
        name="gla_sample",
    )(q, k, v, la, s0)


def _split_kv_lanes(x):
    lane = lax.broadcasted_iota(jnp.int32, x.shape, 1)
    low = lane < SWA_HEAD_DIM
    lo0 = jnp.where(low, x, 0.0)
    hi1 = jnp.where(low, 0.0, x)
    hi0 = pltpu.roll(lo0, SWA_HEAD_DIM, 1)
    lo1 = pltpu.roll(hi1, SWA_HEAD_DIM, 1)
    return (lo0.astype(BF16), lo1.astype(BF16)), (hi0.astype(BF16), hi1.astype(BF16))


def _swa_prompt_body(sink_ref, q_ref, kc_ref, kp_ref, vc_ref, vp_ref, o_ref, st_scr, p_scr):
    n = pl.program_id(1)
    blk = WINDOW
    k_parts = _split_kv_lanes(jnp.concatenate([kp_ref[...], kc_ref[...]], axis=0))
    v_parts = _split_kv_lanes(jnp.concatenate([vp_ref[...], vc_ref[...]], axis=0))
    kj = lax.broadcasted_iota(jnp.int32, (2 * blk, blk), 0)
    qi = lax.broadcasted_iota(jnp.int32, (2 * blk, blk), 1)
    diff = qi + blk - kj
    mask = (diff >= 0) & (diff < WINDOW) & ((n > 0) | (kj >= blk))
    scale = SWA_HEAD_DIM ** -0.5
    contract_rows = (((0,), (0,)), ((), ()))
    z = lax.shift_right_logical(n, 30)
    for head in range(SWA_HEADS):
        pair, half = divmod(head, 2)
        qp = q_ref[:, pair * LANES:(pair + 1) * LANES].astype(BF16)
        st_scr[z, head] = lax.dot_general(k_parts[half][head // SWA_GROUP], qp, NT_DIMS,
                                          preferred_element_type=F32)
    for head in range(SWA_HEADS):
        sink = sink_ref[head]
        st = jnp.where(mask, st_scr[z, head] * scale, NEG_INF)
        m = jnp.maximum(jnp.max(st, axis=0, keepdims=True), sink)
        p = jnp.exp(st - m)
        denom = jnp.sum(p, axis=0, keepdims=True) + jnp.exp(sink - m)
        p_scr[z, head] = (p * (1.0 / denom)).astype(BF16)
    for pair in range(SWA_HEADS // 2):
        kv = (2 * pair) // SWA_GROUP
        o_ref[:, pair * LANES:(pair + 1) * LANES] = (
            lax.dot_general(p_scr[z, 2 * pair], v_parts[0][kv], contract_rows, preferred_element_type=F32)
            + lax.dot_general(p_scr[z, 2 * pair + 1], v_parts[1][kv], contract_rows,
                              preferred_element_type=F32))


def _swa_prompt(sq, sk, sv, sinks):
    bsz, s, _ = sq.shape
    blk = WINDOW
    cur = lambda b, n: (b, n, 0)
    prev = lambda b, n: (b, jnp.maximum(n - 1, 0), 0)
    kv_cur = pl.BlockSpec((None, blk, LANES), cur)
    kv_prev = pl.BlockSpec((None, blk, LANES), prev)
    q_spec = pl.BlockSpec((None, blk, SWA_WIDTH), cur)
    return pl.pallas_call(
        _swa_prompt_body,
        grid=(bsz, s // blk),
        in_specs=[pl.BlockSpec(memory_space=pltpu.SMEM), q_spec, kv_cur, kv_prev, kv_cur, kv_prev],
        out_specs=q_spec,
        out_shape=jax.ShapeDtypeStruct((bsz, s, SWA_WIDTH), F32),
        scratch_shapes=[pltpu.VMEM((1, SWA_HEADS, 2 * blk, blk), F32),
                        pltpu.VMEM((1, SWA_HEADS, 2 * blk, blk), BF16)],
        compiler_params=_params(2),
        name="swa_prompt",
    )(sinks, sq, sk, sk, sv, sv)


def _swa_sample_body(t_new, sink_ref, q_ref, ck_ref, cv_ref, kn_ref, vn_ref, o_ref, nk_ref, nv_ref,
                     st_scr, p_scr):
    bb, rows, _ = q_ref.shape
    w = ck_ref.shape[1]
    pad = jnp.zeros((BF16_SUBLANES - t_new, LANES), F32)
    n_keys = w + BF16_SUBLANES
    c = lax.broadcasted_iota(jnp.int32, (n_keys, rows), 0)
    tq = lax.broadcasted_iota(jnp.int32, (n_keys, rows), 1) // SWA_HEADS
    mask = ((c < w) & (c > tq)) | ((c >= w) & (c - w <= tq) & (c < w + t_new))
    sink = sink_ref[...]
    scale = SWA_HEAD_DIM ** -0.5
    contract_rows = (((0,), (0,)), ((), ()))
    z = lax.shift_right_logical(pl.program_id(0), 30)
    for i in range(bb):
        kcat = jnp.concatenate([ck_ref[i], kn_ref[i], pad], axis=0).astype(BF16)
        st_scr[z, i] = lax.dot_general(kcat, q_ref[i].astype(BF16), NT_DIMS, preferred_element_type=F32)
    for i in range(bb):
        st = jnp.where(mask, st_scr[z, i] * scale, NEG_INF)
        m = jnp.maximum(jnp.max(st, axis=0, keepdims=True), sink)
        p = jnp.exp(st - m)
        denom = jnp.sum(p, axis=0, keepdims=True) + jnp.exp(sink - m)
        p_scr[z, i] = (p * (1.0 / denom)).astype(BF16)
    for i in range(bb):
        vcat = jnp.concatenate([cv_ref[i], vn_ref[i], pad], axis=0).astype(BF16)
        o_ref[i] = lax.dot_general(p_scr[z, i], vcat, contract_rows, preferred_element_type=F32)
        nk_ref[i, 0:w - t_new, :] = ck_ref[i, t_new:, :]
        nk_ref[i, w - t_new:w, :] = kn_ref[i]
        nv_ref[i, 0:w - t_new, :] = cv_ref[i, t_new:, :]
        nv_ref[i, w - t_new:w, :] = vn_ref[i]


def _swa_sample(q_pad, ck, cv, kn, vn, sink_row):
    bsz, rows, _ = q_pad.shape
    t_new = kn.shape[1]
    w = ck.shape[1]
    bb = SAMPLE_BATCH_BLOCK
    n_keys = w + BF16_SUBLANES
    blk = lambda r: pl.BlockSpec((bb, r, LANES), lambda i: (i, 0, 0))
    return pl.pallas_call(
        functools.partial(_swa_sample_body, t_new),
        grid=(bsz // bb,),
        in_specs=[_resident(sink_row.shape), blk(rows), blk(w), blk(w), blk(t_new), blk(t_new)],
        out_specs=[blk(rows), blk(w), blk(w)],
        out_shape=[jax.ShapeDtypeStruct((bsz, rows, LANES), F32),
                   jax.ShapeDtypeStruct(ck.shape, F32), jax.ShapeDtypeStruct(cv.shape, F32)],
        scratch_shapes=[pltpu.VMEM((1, bb, n_keys, rows), F32),
                        pltpu.VMEM((1, bb, n_keys, rows), BF16)],
        compiler_params=_params(1),
        name="swa_sample",
    )(sink_row, q_pad, ck, cv, kn, vn)


def _merge_body(og_ref, r_ref, os_ref, x_ref, gn_ref, wog_ref, wos_ref, gf_ref, wq_ref, keys_ref,
                h_ref, hn_ref, st_ref):
    og = og_ref[...]
    gn = gn_ref[...]
    heads = [_rms(og[:, i * GLA_DV:(i + 1) * GLA_DV], gn) for i in range(GLA_HEADS)]
    r = r_ref[...]
    gated = jnp.concatenate(heads, axis=1) * (r / (1.0 + jnp.exp(-r)))
    h = (x_ref[...]
         + jnp.dot(gated.astype(BF16), wog_ref[...], preferred_element_type=F32)
         + jnp.dot(os_ref[...].astype(BF16), wos_ref[...], preferred_element_type=F32))
    h_ref[...] = h
    hn_f32 = _rms(h, gf_ref[...])
    hn = hn_f32.astype(BF16)
    hn_ref[...] = hn_f32.T.astype(BF16)
    pq = jnp.dot(hn, wq_ref[...], preferred_element_type=F32).astype(BF16)
    for i in range(PEER_HEADS):
        st_ref[i] = lax.dot_general(keys_ref[i], pq[:, i * PEER_QDIM:(i + 1) * PEER_QDIM], NT_DIMS,
                                    preferred_element_type=F32)


def _merge(o_gla, r, o_swa, x, gn, wog, wos, gf, wq, keys):
    t, d = x.shape
    tm = MERGE_ROWS
    row = lambda i: (i, 0)
    return pl.pallas_call(
        _merge_body,
        grid=(t // tm,),
        in_specs=[pl.BlockSpec((tm, GLA_WIDTH), row), pl.BlockSpec((tm, GLA_WIDTH), row),
                  pl.BlockSpec((tm, o_swa.shape[1]), row), pl.BlockSpec((tm, d), row),
                  _resident(gn.shape), _resident(wog.shape), _resident(wos.shape),
                  _resident(gf.shape), _resident(wq.shape), _resident(keys.shape)],
        out_specs=[pl.BlockSpec((tm, d), row), pl.BlockSpec((d, tm), lambda i: (0, i)),
                   pl.BlockSpec((PEER_HEADS, 2 * PEER_N_KEYS, tm), lambda i: (0, 0, i))],
        out_shape=[jax.ShapeDtypeStruct((t, d), F32), jax.ShapeDtypeStruct((d, t), BF16),
                   jax.ShapeDtypeStruct((PEER_HEADS, 2 * PEER_N_KEYS, t), F32)],
        compiler_params=_params(1),
        name="merge",
    )(o_gla, r, o_swa, x, gn, wog, wos, gf, wq, keys)


N_SORTED = PEER_TOPK + 1


def _sorting_network(n):
    pairs = []
    p = 1
    while p < n:
        k = p
        while k >= 1:
            for j in range(k % p, n - k, 2 * k):
                for i in range(min(k, n - j - k)):
                    if (i + j) // (2 * p) == (i + j + k) // (2 * p):
                        pairs.append((i + j, i + j + k))
            k //= 2
        p *= 2
    return pairs


def _largest(vals, n_out):
    n = 1
    while n < len(vals):
        n *= 2
    wires = list(vals) + [None] * (n - len(vals))
    needed = set(range(n_out))
    kept = []
    for i, j in reversed(_sorting_network(n)):
        if i in needed or j in needed:
            needed.update((i, j))
            kept.append((i, j))
    for i, j in reversed(kept):
        hi, lo = wires[i], wires[j]
        if lo is None:
            continue
        if hi is None:
            wires[i], wires[j] = lo, None
        else:
            wires[i], wires[j] = jnp.maximum(hi, lo), jnp.minimum(hi, lo)
    return wires[:n_out]


def _topk_body(st_ref, aux_ref, top0_scr, top1_scr):
    scr = (top0_scr, top1_scr)
    sublanes = 8
    n_tiles = PEER_N_KEYS // sublanes
    network = _sorting_network(n_tiles)
    for h in range(PEER_HEADS):
        for half in range(2):
            base = half * PEER_N_KEYS
            col = [st_ref[h, base + k * sublanes:base + (k + 1) * sublanes, :] for k in range(n_tiles)]
            for i, j in network:
                col[i], col[j] = jnp.maximum(col[i], col[j]), jnp.minimum(col[i], col[j])
            for a in range(N_SORTED):
                m = col[0]
                for shift in (4, 2, 1):
                    m = jnp.maximum(m, pltpu.roll(m, shift, 0))
                scr[half][a, h:h + 1, :] = m[0:1, :]
                if a + 1 < N_SORTED:
                    hit = col[0] == m
                    for k in range(min(n_tiles, N_SORTED - 1 - a)):
                        below = col[k + 1] if k + 1 < n_tiles else NEG_INF
                        col[k] = jnp.where(hit, below, col[k])
    top0 = [top0_scr[a] for a in range(N_SORTED)]
    top1 = [top1_scr[a] for a in range(N_SORTED)]
    cands = [top0[a - 1] + top1[b - 1]
             for a in range(1, N_SORTED + 1) for b in range(1, N_SORTED // a + 1)]
    best = _largest(cands, N_SORTED)
    z = jnp.ones_like(best[0])
    for rank in range(1, PEER_TOPK):
        z = z + jnp.exp(best[rank] - best[0])
    aux_ref[0] = 0.5 * (best[PEER_TOPK - 1] + best[PEER_TOPK])
    aux_ref[1] = top0[0]
    aux_ref[2] = top1[0]
    aux_ref[3] = 1.0 / z


def _topk(st):
    t = st.shape[2]
    tt = TOPK_TOKENS
    return pl.pallas_call(
        _topk_body,
        grid=(t // tt,),
        in_specs=[pl.BlockSpec((PEER_HEADS, 2 * PEER_N_KEYS, tt), lambda i: (0, 0, i))],
        out_specs=pl.BlockSpec((4, PEER_HEADS, tt), lambda i: (0, 0, i)),
        out_shape=jax.ShapeDtypeStruct((4, PEER_HEADS, t), F32),
        scratch_shapes=[pltpu.VMEM((N_SORTED, PEER_HEADS, tt), F32),
                        pltpu.VMEM((N_SORTED, PEER_HEADS, tt), F32)],
        compiler_params=_params(1),
        name="peer_topk",
    )(st)


def _peer_body(st_ref, aux_ref, hnt_ref, u_ref, vt_ref, h_ref, gfin_ref, out_ref,
               thr_scr, w0_scr, e1_scr, thr_cur, w0_cur, acc_scr, *piece_scr):
    ec = pl.program_id(1)
    n_rows = PEER_CHUNK // PEER_N_KEYS
    tb = hnt_ref.shape[1]
    a_scr = piece_scr[:PEER_PARTS]
    c_scr = piece_scr[PEER_PARTS:]

    @pl.when(ec == 0)
    def _():
        grid_rows = (PEER_N_KEYS // n_rows, n_rows, tb)
        for h in range(PEER_HEADS):
            s0 = st_ref[h, 0:PEER_N_KEYS, :]
            s1 = st_ref[h, PEER_N_KEYS:2 * PEER_N_KEYS, :]
            thr_scr[h] = jnp.exp(aux_ref[0, h:h + 1, :] - s0 - aux_ref[2, h:h + 1, :]).reshape(grid_rows)
            w0_scr[h] = (jnp.exp(s0 - aux_ref[1, h:h + 1, :])
                         * (0.5 * aux_ref[3, h:h + 1, :])).reshape(grid_rows)
            e1_scr[h] = jnp.exp(s1 - aux_ref[2, h:h + 1, :])
        acc_scr[...] = jnp.zeros_like(acc_scr)

    sub = 2 * BF16_SUBLANES
    inv_sqrt2 = 1.0 / math.sqrt(2.0)
    e_part = PEER_CHUNK // PEER_PARTS
    rows_per_part = n_rows // PEER_PARTS

    z = lax.shift_right_logical(ec, 30)

    for h in range(PEER_HEADS):
        thr_cur[h] = thr_scr[h, ec]
        w0_cur[h] = w0_scr[h, ec]

    def scores(p):
        a_scr[p][z] = jnp.dot(u_ref[p * e_part:(p + 1) * e_part, :], hnt_ref[...],
                              preferred_element_type=F32)

    def coefficients(p):
        for r in range(rows_per_part):
            ii = p * rows_per_part + r
            for lt in range(tb // LANES):
                ls = slice(lt * LANES, (lt + 1) * LANES)
                thr_b = [thr_cur[h, ii:ii + 1, ls] for h in range(PEER_HEADS)]
                w0_b = [w0_cur[h, ii:ii + 1, ls] for h in range(PEER_HEADS)]
                for jb in range(PEER_N_KEYS // sub):
                    rows = slice(r * PEER_N_KEYS + jb * sub, r * PEER_N_KEYS + (jb + 1) * sub)
                    a = a_scr[p][z, rows, ls]
                    act = a * (1.0 + lax.erf(a * inv_sqrt2))
                    g = None
                    for h in range(PEER_HEADS):
                        e1 = e1_scr[h, jb * sub:(jb + 1) * sub, ls]
                        term = jnp.where(e1 >= thr_b[h], e1, 0.0) * w0_b[h]
                        g = term if g is None else g + term
                    c_scr[p][z, rows, ls] = (g * act).astype(BF16)

    def values(p):
        acc_scr[z] += jnp.dot(vt_ref[p], c_scr[p][z], preferred_element_type=F32)

    for p in range(PEER_PARTS):
        scores(p)
    for p in range(PEER_PARTS):
        coefficients(p)
        values(p)

    @pl.when(ec == pl.num_programs(1) - 1)
    def _():
        out_ref[...] = _rms(h_ref[...] + acc_scr[0].T, gfin_ref[...])


def _peer(st, aux, hnt, u, vt, h, g_final):
    d, t = hnt.shape
    tb = PEER_TOKENS
    ec = PEER_CHUNK
    n_chunks = u.shape[0] // ec
    rows = ec // PEER_N_KEYS
    return pl.pallas_call(
        _peer_body,
        grid=(t // tb, n_chunks),
        in_specs=[pl.BlockSpec((PEER_HEADS, 2 * PEER_N_KEYS, tb), lambda i, j: (0, 0, i),
                               pipeline_mode=pl.Buffered(1)),
                  pl.BlockSpec((4, PEER_HEADS, tb), lambda i, j: (0, 0, i)),
                  pl.BlockSpec((d, tb), lambda i, j: (0, i), pipeline_mode=pl.Buffered(1)),
                  pl.BlockSpec((ec, d), lambda i, j: (j, 0)),
                  pl.BlockSpec((PEER_PARTS, d, ec // PEER_PARTS), lambda i, j: (j, 0, 0)),
                  pl.BlockSpec((tb, d), lambda i, j: (i, 0), pipeline_mode=pl.Buffered(1)),
                  _resident(g_final.shape)],
        out_specs=pl.BlockSpec((tb, d), lambda i, j: (i, 0)),
        out_shape=jax.ShapeDtypeStruct((t, d), F32),
        scratch_shapes=([pltpu.VMEM((PEER_HEADS, n_chunks, rows, tb), F32),
                         pltpu.VMEM((PEER_HEADS, n_chunks, rows, tb), F32),
                         pltpu.VMEM((PEER_HEADS, PEER_N_KEYS, tb), F32),
                         pltpu.VMEM((PEER_HEADS, rows, tb), F32),
                         pltpu.VMEM((PEER_HEADS, rows, tb), F32),
                         pltpu.VMEM((1, d, tb), F32)]
                        + [pltpu.VMEM((1, ec // PEER_PARTS, tb), F32)] * PEER_PARTS
                        + [pltpu.VMEM((1, ec // PEER_PARTS, tb), BF16)] * PEER_PARTS),
        compiler_params=_params(2),
        name="peer_dense",
    )(st, aux, hnt, u, vt, h, g_final)


def _rope_tables(pos):
    half = SWA_HEAD_DIM // 2
    inv = jnp.exp(-math.log(ROPE_THETA) * jnp.arange(half, dtype=F32) * (2.0 / SWA_HEAD_DIM))
    ang = pos.astype(F32)[:, None] * inv[None, :]
    cos, sin = jnp.cos(ang), jnp.sin(ang)
    reps = LANES // SWA_HEAD_DIM
    return (jnp.tile(jnp.concatenate([cos, cos], axis=1), (1, reps)),
            jnp.tile(jnp.concatenate([-sin, sin], axis=1), (1, reps)))


def _pad_heads_to_kv_half(w, axis):
    shape = w.shape
    w = w.reshape(shape[:axis] + (SWA_KV_HEADS, SWA_GROUP, SWA_HEAD_DIM) + shape[axis + 1:])
    parts = []
    for kv in range(SWA_KV_HEADS):
        heads = lax.index_in_dim(w, kv, axis, keepdims=False)
        pads = [(0, 0)] * heads.ndim
        pads[axis + 1] = (kv * SWA_HEAD_DIM, (SWA_KV_HEADS - 1 - kv) * SWA_HEAD_DIM)
        parts.append(jnp.pad(heads, pads))
    out = jnp.concatenate(parts, axis=axis)
    return out.reshape(shape[:axis] + (SWA_HEADS * LANES,) + shape[axis + 1:])


def kernel(x_prompt, x_sample, state_gla, cache_win_k, cache_win_v, g_attn, w_in, w_gate_up, b_gate,
           g_gla_out, attn_sinks, w_out, g_ffn, w_peer_q, peer_sub_keys, peer_u, peer_v, g_final):
    bsz, seq, d = x_prompt.shape
    dbsz, dseq, _ = x_sample.shape
    win = cache_win_k.shape[2]

    n_gla = 2 * GLA_HEADS * GLA_DK + 2 * GLA_WIDTH
    o_sq = n_gla + GLA_GATE_RANK
    w_gla = w_in[0, :, :n_gla].astype(BF16)
    ga = jnp.pad(w_in[0, :, n_gla:o_sq], ((0, 0), (0, LANES - GLA_GATE_RANK)))
    wsq = w_in[0, :, o_sq:o_sq + SWA_WIDTH]
    wkv = w_in[0, :, o_sq + SWA_WIDTH:]
    w_swa_prompt = jnp.concatenate([wsq, wkv, ga], axis=1).astype(BF16)
    w_swa_sample = jnp.concatenate([_pad_heads_to_kv_half(wsq, 1), wkv, ga], axis=1).astype(BF16)
    wgu = jnp.pad(w_gate_up[0], ((0, LANES - GLA_GATE_RANK), (0, 0)))
    bg = b_gate[0][None, :]
    g_a = g_attn[0][None, :]
    g_f = g_ffn[0][None, :]
    g_n = g_gla_out[0][None, :]
    wo_gla = w_out[0, :GLA_WIDTH].astype(BF16)
    wo_swa = w_out[0, GLA_WIDTH:]
    wo_swa_pad = _pad_heads_to_kv_half(wo_swa, 0).astype(BF16)
    wo_swa = wo_swa.astype(BF16)
    wq = w_peer_q[0].astype(BF16)
    sk0 = jnp.pad(peer_sub_keys[0, :, 0], ((0, 0), (0, 0), (0, PEER_HALF)))
    sk1 = jnp.pad(peer_sub_keys[0, :, 1], ((0, 0), (0, 0), (PEER_HALF, 0)))
    keys = jnp.concatenate([sk0, sk1], axis=1).astype(BF16)
    u = peer_u[0].astype(BF16)
    e_piece = PEER_CHUNK // PEER_PARTS
    vt = peer_v[0].reshape(PEER_EXPERTS // e_piece, e_piece, d).transpose(0, 2, 1).astype(BF16)
    sinks = attn_sinks[0]
    sink_row = jnp.tile(sinks, dseq)[None, :]

    cos_p, sin_p = _rope_tables(jnp.arange(seq))
    cos_s, sin_s = _rope_tables(PAST_LEN + jnp.arange(dseq))
    cos_s = jnp.tile(cos_s, (PROJ_ROWS // dseq, 1))
    sin_s = jnp.tile(sin_s, (PROJ_ROWS // dseq, 1))

    def ffn(h, hn, st):
        aux = _topk(st)
        return _peer(st, aux, hn, u, vt, h, g_final[None, :])

    tp = bsz * seq
    q, k, v, r, la, sq, sk, sv = _proj(x_prompt.reshape(tp, d), g_a, w_gla, w_swa_prompt, wgu, bg,
                                       cos_p, sin_p)
    b3 = lambda a: a.reshape(bsz, seq, a.shape[-1])
    o_gla, s_prompt = _gla_prompt(b3(q), b3(k), b3(v), b3(la))
    o_swa = _swa_prompt(b3(sq), b3(sk), b3(sv), sinks)
    h, hn, st = _merge(o_gla.reshape(tp, GLA_WIDTH), r, o_swa.reshape(tp, SWA_WIDTH),
                       x_prompt.reshape(tp, d), g_n, wo_gla, wo_swa, g_f, wq, keys)
    y_prompt = ffn(h, hn, st).reshape(bsz, seq, d)
    kv_shape = (1, bsz, WINDOW, SWA_KV_HEADS, SWA_HEAD_DIM)
    k_prompt = b3(sk)[:, seq - WINDOW:].reshape(kv_shape)
    v_prompt = b3(sv)[:, seq - WINDOW:].reshape(kv_shape)

    ts = dbsz * dseq
    q, k, v, r, la, sq, sk, sv = _proj(x_sample.reshape(ts, d), g_a, w_gla, w_swa_sample, wgu, bg,
                                       cos_s, sin_s)
    d3 = lambda a: a.reshape(dbsz, dseq, a.shape[-1])
    o_gla, s_sample = _gla_sample(d3(q), d3(k), d3(v), d3(la), state_gla[0])
    o_swa, k_sample, v_sample = _swa_sample(
        sq.reshape(dbsz, dseq * SWA_HEADS, LANES),
        cache_win_k[0].reshape(dbsz, win, LANES), cache_win_v[0].reshape(dbsz, win, LANES),
        d3(sk), d3(sv), sink_row)
    h, hn, st = _merge(o_gla.reshape(ts, GLA_WIDTH), r, o_swa.reshape(ts, SWA_HEADS * LANES),
                       x_sample.reshape(ts, d), g_n, wo_gla, wo_swa_pad, g_f, wq, keys)
    y_sample = ffn(h, hn, st).reshape(dbsz, dseq, d)
    ckv_shape = (1, dbsz, win, SWA_KV_HEADS, SWA_HEAD_DIM)

    return (y_prompt, y_sample, s_prompt[None], k_prompt, v_prompt, s_sample[None],
            k_sample.reshape(ckv_shape), v_sample.reshape(ckv_shape))
```

```python
import functools
import math

import jax
import jax.numpy as jnp
from jax import lax
from jax.experimental import pallas as pl
from jax.experimental.pallas import tpu as pltpu

F32 = jnp.float32
BF16 = jnp.bfloat16
HIGHEST = lax.Precision.HIGHEST

GLA_HEADS = 4
GLA_DK = 128
GLA_DV = 256
GLA_WIDTH = GLA_HEADS * GLA_DV
GLA_GATE_RANK = 16
GLA_TAU = 16.0
GLA_CHUNK = 64
SWA_HEAD_DIM = 64
SWA_HEADS = 16
SWA_KV_HEADS = 2
SWA_GROUP = SWA_HEADS // SWA_KV_HEADS
SWA_WIDTH = SWA_HEADS * SWA_HEAD_DIM
WINDOW = 128
ROPE_THETA = 10000.0
PAST_LEN = 8192
PEER_N_KEYS = 128
PEER_EXPERTS = PEER_N_KEYS * PEER_N_KEYS
PEER_HEADS = 8
PEER_QDIM = 128
PEER_HALF = PEER_QDIM // 2
PEER_TOPK = 16
EPS = 1e-6
NEG_INF = -1e30

LANES = 128
BF16_SUBLANES = 16
PROJ_ROWS = 512
MERGE_ROWS = 256
GLA_CHUNKS_PER_STEP = 8
PEER_TOKENS = 512
PEER_CHUNK = 1024
PEER_PARTS = 2
SAMPLE_BATCH_BLOCK = 16
VMEM_LIMIT = 60 * 1024 * 1024

NT_DIMS = (((1,), (1,)), ((), ()))


def _resident(shape):
    nd = len(shape)
    return pl.BlockSpec(shape, lambda *_: (0,) * nd, pipeline_mode=pl.Buffered(1))


def _params(n_axes):
    return pltpu.CompilerParams(dimension_semantics=("arbitrary",) * n_axes,
                                vmem_limit_bytes=VMEM_LIMIT)


def _rms(x, g):
    return x * lax.rsqrt(jnp.mean(x * x, axis=-1, keepdims=True) + EPS) * g


def _proj_body(x_ref, g_ref, wg_ref, ws_ref, wgu_ref, bg_ref, cos_ref, sin_ref,
               q_ref, k_ref, v_ref, r_ref, la_ref, sq_ref, sk_ref, sv_ref):
    xn = _rms(x_ref[...], g_ref[...]).astype(BF16)
    wsq = sq_ref.shape[1]

    def proj(w_ref, start, width):
        return jnp.dot(xn, w_ref[:, start:start + width], preferred_element_type=F32)

    nk = GLA_HEADS * GLA_DK
    q_ref[...] = proj(wg_ref, 0, nk) * (GLA_DK ** -0.5)
    k_ref[...] = proj(wg_ref, nk, nk)
    v_ref[...] = proj(wg_ref, 2 * nk, GLA_WIDTH)
    r_ref[...] = proj(wg_ref, 2 * nk + GLA_WIDTH, GLA_WIDTH)

    sq_all = proj(ws_ref, 0, wsq)
    tail = proj(ws_ref, wsq, 3 * LANES)
    ga = tail[:, 2 * LANES:]
    z = jnp.dot(ga, wgu_ref[...], precision=HIGHEST, preferred_element_type=F32) + bg_ref[...]
    la_ref[...] = (jnp.minimum(z, 0.0) - jnp.log1p(jnp.exp(-jnp.abs(z)))) * (1.0 / GLA_TAU)

    cos = cos_ref[...]
    sin = sin_ref[...]
    lane = lax.broadcasted_iota(jnp.int32, cos.shape, 1)
    first_half = (lane & (SWA_HEAD_DIM // 2)) == 0

    def rope(t):
        rot = jnp.where(first_half, pltpu.roll(t, LANES - SWA_HEAD_DIM // 2, 1),
                        pltpu.roll(t, SWA_HEAD_DIM // 2, 1))
        return t * cos + rot * sin

    for c in range(wsq // LANES):
        sq_ref[:, c * LANES:(c + 1) * LANES] = rope(sq_all[:, c * LANES:(c + 1) * LANES])
    sk_ref[...] = rope(tail[:, :LANES])
    sv_ref[...] = tail[:, LANES:2 * LANES]


def _proj(x, g, w_gla, w_swa, wgu, bg, cos, sin):
    t, d = x.shape
    tm = PROJ_ROWS
    n_tab = cos.shape[0] // tm
    row = lambda i: (i, 0)
    tab = lambda i: (i % n_tab, 0)
    wsq = w_swa.shape[1] - 3 * LANES
    nk = GLA_HEADS * GLA_DK
    widths = (nk, nk, GLA_WIDTH, GLA_WIDTH, nk, wsq, LANES, LANES)
    return pl.pallas_call(
        _proj_body,
        grid=(t // tm,),
        in_specs=[pl.BlockSpec((tm, d), row), _resident(g.shape), _resident(w_gla.shape),
                  _resident(w_swa.shape), _resident(wgu.shape), _resident(bg.shape),
                  pl.BlockSpec((tm, LANES), tab), pl.BlockSpec((tm, LANES), tab)],
        out_specs=[pl.BlockSpec((tm, n), row) for n in widths],
        out_shape=[jax.ShapeDtypeStruct((t, n), F32) for n in widths],
        compiler_params=_params(1),
        name="proj",
    )(x, g, w_gla, w_swa, wgu, bg, cos, sin)


def _gla_prompt_body(q_ref, k_ref, v_ref, la_ref, o_ref, sfin_ref,
                     s_scr, qt_scr, ds_scr, dec_scr, b_scr, kt_scr, ket_scr, a_scr):
    n = pl.program_id(0)
    bsz = q_ref.shape[0]
    n_ch = q_ref.shape[1] // GLA_CHUNK

    @pl.when(n == 0)
    def _():
        s_scr[...] = jnp.zeros_like(s_scr)

    c = GLA_CHUNK
    row = lax.broadcasted_iota(jnp.int32, (c, c), 0)
    col = lax.broadcasted_iota(jnp.int32, (c, c), 1)
    causal = row >= col
    tri = causal.astype(F32)
    z = lax.shift_right_logical(n, 30)

    def units():
        for ch in range(n_ch):
            for bi in range(bsz):
                for h in range(GLA_HEADS):
                    yield ((ch * bsz + bi) * GLA_HEADS + h, ch * bsz + bi, bi,
                           slice(ch * c, (ch + 1) * c),
                           slice(h * GLA_DK, (h + 1) * GLA_DK), slice(h * GLA_DV, (h + 1) * GLA_DV))

    for ch in range(n_ch):
        for bi in range(bsz):
            b_scr[z, ch * bsz + bi] = jnp.dot(tri, la_ref[bi, ch * c:(ch + 1) * c, :], precision=HIGHEST,
                                              preferred_element_type=F32)
    for unit, cb, bi, rows, kl, vl in units():
        b = b_scr[z, cb, :, kl]
        b_last = b[c - 1:c, :]
        k = k_ref[bi, rows, kl]
        qt_scr[z, unit] = (q_ref[bi, rows, kl] * jnp.exp(b)).astype(BF16)
        kt_scr[z, unit] = (k * jnp.exp(-b)).astype(BF16)
        ket_scr[z, unit] = (k * jnp.exp(b_last - b)).T.astype(BF16)
        dec_scr[z, unit] = jnp.exp(jnp.broadcast_to(b_last, (GLA_DK, GLA_DK))).T
    for unit, cb, bi, rows, kl, vl in units():
        a = lax.dot_general(qt_scr[z, unit], kt_scr[z, unit], NT_DIMS, preferred_element_type=F32)
        a_scr[z, unit] = jnp.where(causal, a, 0.0).astype(BF16)
    for unit, cb, bi, rows, kl, vl in units():
        vb = v_ref[bi, rows, vl].astype(BF16)
        o_ref[bi, rows, vl] = jnp.dot(a_scr[z, unit], vb, preferred_element_type=F32)
        ds_scr[z, unit] = jnp.dot(ket_scr[z, unit], vb, preferred_element_type=F32)

    for ch in range(n_ch):
        rows = slice(ch * c, (ch + 1) * c)
        for bi in range(bsz):
            for h in range(GLA_HEADS):
                unit = (ch * bsz + bi) * GLA_HEADS + h
                chain = bi * GLA_HEADS + h
                vl = slice(h * GLA_DV, (h + 1) * GLA_DV)
                s = s_scr[chain]
                o_ref[bi, rows, vl] += jnp.dot(qt_scr[z, unit], s.astype(BF16), preferred_element_type=F32)
                decay = dec_scr[z, unit]
                s_scr[chain] = jnp.concatenate([decay, decay], axis=1) * s + ds_scr[z, unit]

    @pl.when(n == pl.num_programs(0) - 1)
    def _():
        sfin_ref[...] = s_scr[...].reshape(sfin_ref.shape)


def _gla_prompt(q, k, v, la):
    bsz, s, _ = q.shape
    rows = GLA_CHUNK * GLA_CHUNKS_PER_STEP
    units = GLA_CHUNKS_PER_STEP * bsz * GLA_HEADS
    qk_spec = pl.BlockSpec((bsz, rows, GLA_HEADS * GLA_DK), lambda n: (0, n, 0))
    v_spec = pl.BlockSpec((bsz, rows, GLA_WIDTH), lambda n: (0, n, 0))
    return pl.pallas_call(
        _gla_prompt_body,
        grid=(s // rows,),
        in_specs=[qk_spec, qk_spec, v_spec, qk_spec],
        out_specs=[v_spec, pl.BlockSpec((bsz, GLA_HEADS, GLA_DK, GLA_DV), lambda n: (0, 0, 0, 0))],
        out_shape=[jax.ShapeDtypeStruct((bsz, s, GLA_WIDTH), F32),
                   jax.ShapeDtypeStruct((bsz, GLA_HEADS, GLA_DK, GLA_DV), F32)],
        scratch_shapes=[pltpu.VMEM((bsz * GLA_HEADS, GLA_DK, GLA_DV), F32),
                        pltpu.VMEM((1, units, GLA_CHUNK, GLA_DK), BF16),
                        pltpu.VMEM((1, units, GLA_DK, GLA_DV), F32),
                        pltpu.VMEM((1, units, GLA_DK, GLA_DK), F32),
                        pltpu.VMEM((1, units // GLA_HEADS, GLA_CHUNK, GLA_HEADS * GLA_DK), F32),
                        pltpu.VMEM((1, units, GLA_CHUNK, GLA_DK), BF16),
                        pltpu.VMEM((1, units, GLA_DK, GLA_CHUNK), BF16),
                        pltpu.VMEM((1, units, GLA_CHUNK, GLA_CHUNK), BF16)],
        compiler_params=_params(1),
        name="gla_prompt",
    )(q, k, v, la)


def _gla_sample_body(q_ref, k_ref, v_ref, la_ref, s0_ref, o_ref, sout_ref, oi_scr, dsum_scr, upd_scr):
    bb, t, _ = q_ref.shape
    rows = bb * t
    q = q_ref[...].reshape(rows, GLA_DK)
    k = k_ref[...].reshape(rows, GLA_DK)
    la = la_ref[...].reshape(rows, GLA_DK)
    v = v_ref[...].reshape(rows, GLA_DV)
    row = lax.broadcasted_iota(jnp.int32, (rows, rows), 0)
    col = lax.broadcasted_iota(jnp.int32, (rows, rows), 1)
    same_seq = (row // t) == (col // t)
    causal = same_seq & (row >= col)
    b = jnp.dot(causal.astype(F32), la, precision=HIGHEST, preferred_element_type=F32)
    b_tot = jnp.dot(same_seq.astype(F32), la, precision=HIGHEST, preferred_element_type=F32)
    q_t = (q * jnp.exp(b)).astype(BF16)
    k_t = (k * jnp.exp(-b)).astype(BF16)
    k_e_t = (k * jnp.exp(b_tot - b)).T.astype(BF16)
    la_t = la.T
    a = lax.dot_general(q_t, k_t, NT_DIMS, preferred_element_type=F32)
    a = jnp.where(causal, a, 0.0).astype(BF16)
    o = jnp.dot(a, v.astype(BF16), preferred_element_type=F32)
    seq_of_row_v = lax.broadcasted_iota(jnp.int32, (rows, GLA_DV), 0) // t
    seq_of_row_k = lax.broadcasted_iota(jnp.int32, (rows, GLA_DK), 0) // t
    z = lax.shift_right_logical(pl.program_id(0), 30)
    for i in range(bb):
        oi_scr[z, i] = jnp.dot(q_t, s0_ref[i].astype(BF16), preferred_element_type=F32)
    for i in range(bb):
        ones_i = jnp.where(seq_of_row_k == i, 1.0, 0.0)
        dsum_scr[z, i] = jnp.dot(la_t, ones_i, precision=HIGHEST, preferred_element_type=F32)
    for i in range(bb):
        v_i = jnp.where(seq_of_row_v == i, v, 0.0).astype(BF16)
        upd_scr[z, i] = jnp.dot(k_e_t, v_i, preferred_element_type=F32)
    for i in range(bb):
        o = o + jnp.where(seq_of_row_v == i, oi_scr[z, i], 0.0)
        decay = jnp.exp(dsum_scr[z, i])
        sout_ref[i] = jnp.concatenate([decay, decay], axis=1) * s0_ref[i] + upd_scr[z, i]
    o_ref[...] = o.reshape(bb, t, GLA_DV)


def _gla_sample(q, k, v, la, s0):
    bsz, t, _ = q.shape
    bb = SAMPLE_BATCH_BLOCK
    qk_spec = pl.BlockSpec((bb, t, GLA_DK), lambda i, h: (i, 0, h))
    v_spec = pl.BlockSpec((bb, t, GLA_DV), lambda i, h: (i, 0, h))
    s_spec = pl.BlockSpec((bb, None, GLA_DK, GLA_DV), lambda i, h: (i, h, 0, 0))
    return pl.pallas_call(
        _gla_sample_body,
        grid=(bsz // bb, GLA_HEADS),
        in_specs=[qk_spec, qk_spec, v_spec, qk_spec, s_spec],
        out_specs=[v_spec, s_spec],
        out_shape=[jax.ShapeDtypeStruct((bsz, t, GLA_WIDTH), F32),
                   jax.ShapeDtypeStruct(s0.shape, F32)],
        scratch_shapes=[pltpu.VMEM((1, bb, bb * t, GLA_DV), F32),
                        pltpu.VMEM((1, bb, GLA_DK, GLA_DK), F32),
                        pltpu.VMEM((1, bb, GLA_DK, GLA_DV), F32)],
        compiler_params=_params(2),
        name="gla_sample",
    )(q, k, v, la, s0)


def _split_kv_lanes(x):
    lane = lax.broadcasted_iota(jnp.int32, x.shape, 1)
    low = lane < SWA_HEAD_DIM
    lo0 = jnp.where(low, x, 0.0)
    hi1 = jnp.where(low, 0.0, x)
    hi0 = pltpu.roll(lo0, SWA_HEAD_DIM, 1)
    lo1 = pltpu.roll(hi1, SWA_HEAD_DIM, 1)
    return (lo0.astype(BF16), lo1.astype(BF16)), (hi0.astype(BF16), hi1.astype(BF16))


def _swa_prompt_body(sink_ref, q_ref, kc_ref, kp_ref, vc_ref, vp_ref, o_ref, st_scr, p_scr):
    n = pl.program_id(1)
    blk = WINDOW
    k_parts = _split_kv_lanes(jnp.concatenate([kp_ref[...], kc_ref[...]], axis=0))
    v_parts = _split_kv_lanes(jnp.concatenate([vp_ref[...], vc_ref[...]], axis=0))
    kj = lax.broadcasted_iota(jnp.int32, (2 * blk, blk), 0)
    qi = lax.broadcasted_iota(jnp.int32, (2 * blk, blk), 1)
    diff = qi + blk - kj
    mask = (diff >= 0) & (diff < WINDOW) & ((n > 0) | (kj >= blk))
    scale = SWA_HEAD_DIM ** -0.5
    contract_rows = (((0,), (0,)), ((), ()))
    z = lax.shift_right_logical(n, 30)
    for head in range(SWA_HEADS):
        pair, half = divmod(head, 2)
        qp = q_ref[:, pair * LANES:(pair + 1) * LANES].astype(BF16)
        st_scr[z, head] = lax.dot_general(k_parts[half][head // SWA_GROUP], qp, NT_DIMS,
                                          preferred_element_type=F32)
    for head in range(SWA_HEADS):
        sink = sink_ref[head]
        st = jnp.where(mask, st_scr[z, head] * scale, NEG_INF)
        m = jnp.maximum(jnp.max(st, axis=0, keepdims=True), sink)
        p = jnp.exp(st - m)
        denom = jnp.sum(p, axis=0, keepdims=True) + jnp.exp(sink - m)
        p_scr[z, head] = (p * (1.0 / denom)).astype(BF16)
    for pair in range(SWA_HEADS // 2):
        kv = (2 * pair) // SWA_GROUP
        o_ref[:, pair * LANES:(pair + 1) * LANES] = (
            lax.dot_general(p_scr[z, 2 * pair], v_parts[0][kv], contract_rows, preferred_element_type=F32)
            + lax.dot_general(p_scr[z, 2 * pair + 1], v_parts[1][kv], contract_rows,
                              preferred_element_type=F32))


def _swa_prompt(sq, sk, sv, sinks):
    bsz, s, _ = sq.shape
    blk = WINDOW
    cur = lambda b, n: (b, n, 0)
    prev = lambda b, n: (b, jnp.maximum(n - 1, 0), 0)
    kv_cur = pl.BlockSpec((None, blk, LANES), cur)
    kv_prev = pl.BlockSpec((None, blk, LANES), prev)
    q_spec = pl.BlockSpec((None, blk, SWA_WIDTH), cur)
    return pl.pallas_call(
        _swa_prompt_body,
        grid=(bsz, s // blk),
        in_specs=[pl.BlockSpec(memory_space=pltpu.SMEM), q_spec, kv_cur, kv_prev, kv_cur, kv_prev],
        out_specs=q_spec,
        out_shape=jax.ShapeDtypeStruct((bsz, s, SWA_WIDTH), F32),
        scratch_shapes=[pltpu.VMEM((1, SWA_HEADS, 2 * blk, blk), F32),
                        pltpu.VMEM((1, SWA_HEADS, 2 * blk, blk), BF16)],
        compiler_params=_params(2),
        name="swa_prompt",
    )(sinks, sq, sk, sk, sv, sv)


def _swa_sample_body(t_new, sink_ref, q_ref, ck_ref, cv_ref, kn_ref, vn_ref, o_ref, nk_ref, nv_ref,
                     st_scr, p_scr):
    bb, rows, _ = q_ref.shape
    w = ck_ref.shape[1]
    pad = jnp.zeros((BF16_SUBLANES - t_new, LANES), F32)
    n_keys = w + BF16_SUBLANES
    c = lax.broadcasted_iota(jnp.int32, (n_keys, rows), 0)
    tq = lax.broadcasted_iota(jnp.int32, (n_keys, rows), 1) // SWA_HEADS
    mask = ((c < w) & (c > tq)) | ((c >= w) & (c - w <= tq) & (c < w + t_new))
    sink = sink_ref[...]
    scale = SWA_HEAD_DIM ** -0.5
    contract_rows = (((0,), (0,)), ((), ()))
    z = lax.shift_right_logical(pl.program_id(0), 30)
    for i in range(bb):
        kcat = jnp.concatenate([ck_ref[i], kn_ref[i], pad], axis=0).astype(BF16)
        st_scr[z, i] = lax.dot_general(kcat, q_ref[i].astype(BF16), NT_DIMS, preferred_element_type=F32)
    for i in range(bb):
        st = jnp.where(mask, st_scr[z, i] * scale, NEG_INF)
        m = jnp.maximum(jnp.max(st, axis=0, keepdims=True), sink)
        p = jnp.exp(st - m)
        denom = jnp.sum(p, axis=0, keepdims=True) + jnp.exp(sink - m)
        p_scr[z, i] = (p * (1.0 / denom)).astype(BF16)
    for i in range(bb):
        vcat = jnp.concatenate([cv_ref[i], vn_ref[i], pad], axis=0).astype(BF16)
        o_ref[i] = lax.dot_general(p_scr[z, i], vcat, contract_rows, preferred_element_type=F32)
        nk_ref[i, 0:w - t_new, :] = ck_ref[i, t_new:, :]
        nk_ref[i, w - t_new:w, :] = kn_ref[i]
        nv_ref[i, 0:w - t_new, :] = cv_ref[i, t_new:, :]
        nv_ref[i, w - t_new:w, :] = vn_ref[i]


def _swa_sample(q_pad, ck, cv, kn, vn, sink_row):
    bsz, rows, _ = q_pad.shape
    t_new = kn.shape[1]
    w = ck.shape[1]
    bb = SAMPLE_BATCH_BLOCK
    n_keys = w + BF16_SUBLANES
    blk = lambda r: pl.BlockSpec((bb, r, LANES), lambda i: (i, 0, 0))
    return pl.pallas_call(
        functools.partial(_swa_sample_body, t_new),
        grid=(bsz // bb,),
        in_specs=[_resident(sink_row.shape), blk(rows), blk(w), blk(w), blk(t_new), blk(t_new)],
        out_specs=[blk(rows), blk(w), blk(w)],
        out_shape=[jax.ShapeDtypeStruct((bsz, rows, LANES), F32),
                   jax.ShapeDtypeStruct(ck.shape, F32), jax.ShapeDtypeStruct(cv.shape, F32)],
        scratch_shapes=[pltpu.VMEM((1, bb, n_keys, rows), F32),
                        pltpu.VMEM((1, bb, n_keys, rows), BF16)],
        compiler_params=_params(1),
        name="swa_sample",
    )(sink_row, q_pad, ck, cv, kn, vn)


def _merge_body(og_ref, r_ref, os_ref, x_ref, gn_ref, wog_ref, wos_ref, gf_ref, wq_ref, keys_ref,
                h_ref, hn_ref, st_ref, aux_ref, top0_scr, top1_scr):
    og = og_ref[...]
    gn = gn_ref[...]
    heads = [_rms(og[:, i * GLA_DV:(i + 1) * GLA_DV], gn) for i in range(GLA_HEADS)]
    r = r_ref[...]
    gated = jnp.concatenate(heads, axis=1) * (r / (1.0 + jnp.exp(-r)))
    h = (x_ref[...]
         + jnp.dot(gated.astype(BF16), wog_ref[...], preferred_element_type=F32)
         + jnp.dot(os_ref[...].astype(BF16), wos_ref[...], preferred_element_type=F32))
    h_ref[...] = h
    hn_f32 = _rms(h, gf_ref[...])
    hn = hn_f32.astype(BF16)
    hn_ref[...] = hn_f32.T.astype(BF16)
    pq = jnp.dot(hn, wq_ref[...], preferred_element_type=F32).astype(BF16)
    for i in range(PEER_HEADS):
        st_ref[i] = lax.dot_general(keys_ref[i], pq[:, i * PEER_QDIM:(i + 1) * PEER_QDIM], NT_DIMS,
                                    preferred_element_type=F32)
    _topk_body(st_ref, aux_ref, top0_scr, top1_scr)


def _merge(o_gla, r, o_swa, x, gn, wog, wos, gf, wq, keys):
    t, d = x.shape
    tm = MERGE_ROWS
    row = lambda i: (i, 0)
    return pl.pallas_call(
        _merge_body,
        grid=(t // tm,),
        in_specs=[pl.BlockSpec((tm, GLA_WIDTH), row), pl.BlockSpec((tm, GLA_WIDTH), row),
                  pl.BlockSpec((tm, o_swa.shape[1]), row), pl.BlockSpec((tm, d), row),
                  _resident(gn.shape), _resident(wog.shape), _resident(wos.shape),
                  _resident(gf.shape), _resident(wq.shape), _resident(keys.shape)],
        out_specs=[pl.BlockSpec((tm, d), row), pl.BlockSpec((d, tm), lambda i: (0, i)),
                   pl.BlockSpec((PEER_HEADS, 2 * PEER_N_KEYS, tm), lambda i: (0, 0, i)),
                   pl.BlockSpec((4, PEER_HEADS, tm), lambda i: (0, 0, i))],
        out_shape=[jax.ShapeDtypeStruct((t, d), F32), jax.ShapeDtypeStruct((d, t), BF16),
                   jax.ShapeDtypeStruct((PEER_HEADS, 2 * PEER_N_KEYS, t), F32),
                   jax.ShapeDtypeStruct((4, PEER_HEADS, t), F32)],
        scratch_shapes=[pltpu.VMEM((N_SORTED, PEER_HEADS, tm), F32),
                        pltpu.VMEM((N_SORTED, PEER_HEADS, tm), F32)],
        compiler_params=_params(1),
        name="merge",
    )(o_gla, r, o_swa, x, gn, wog, wos, gf, wq, keys)


N_SORTED = PEER_TOPK + 1


def _sorting_network(n):
    pairs = []
    p = 1
    while p < n:
        k = p
        while k >= 1:
            for j in range(k % p, n - k, 2 * k):
                for i in range(min(k, n - j - k)):
                    if (i + j) // (2 * p) == (i + j + k) // (2 * p):
                        pairs.append((i + j, i + j + k))
            k //= 2
        p *= 2
    return pairs


def _largest(vals, n_out):
    n = 1
    while n < len(vals):
        n *= 2
    wires = list(vals) + [None] * (n - len(vals))
    needed = set(range(n_out))
    kept = []
    for i, j in reversed(_sorting_network(n)):
        if i in needed or j in needed:
            needed.update((i, j))
            kept.append((i, j))
    for i, j in reversed(kept):
        hi, lo = wires[i], wires[j]
        if lo is None:
            continue
        if hi is None:
            wires[i], wires[j] = lo, None
        else:
            wires[i], wires[j] = jnp.maximum(hi, lo), jnp.minimum(hi, lo)
    return wires[:n_out]


def _topk_body(st_ref, aux_ref, top0_scr, top1_scr):
    scr = (top0_scr, top1_scr)
    sublanes = 8
    n_tiles = PEER_N_KEYS // sublanes
    network = _sorting_network(n_tiles)
    for h in range(PEER_HEADS):
        for half in range(2):
            base = half * PEER_N_KEYS
            col = [st_ref[h, base + k * sublanes:base + (k + 1) * sublanes, :] for k in range(n_tiles)]
            for i, j in network:
                col[i], col[j] = jnp.maximum(col[i], col[j]), jnp.minimum(col[i], col[j])
            for a in range(N_SORTED):
                m = col[0]
                for shift in (4, 2, 1):
                    m = jnp.maximum(m, pltpu.roll(m, shift, 0))
                scr[half][a, h:h + 1, :] = m[0:1, :]
                if a + 1 < N_SORTED:
                    hit = col[0] == m
                    for k in range(min(n_tiles, N_SORTED - 1 - a)):
                        below = col[k + 1] if k + 1 < n_tiles else NEG_INF
                        col[k] = jnp.where(hit, below, col[k])
    top0 = [top0_scr[a] for a in range(N_SORTED)]
    top1 = [top1_scr[a] for a in range(N_SORTED)]
    cands = [top0[a - 1] + top1[b - 1]
             for a in range(1, N_SORTED + 1) for b in range(1, N_SORTED // a + 1)]
    best = _largest(cands, N_SORTED)
    z = jnp.ones_like(best[0])
    for rank in range(1, PEER_TOPK):
        z = z + jnp.exp(best[rank] - best[0])
    aux_ref[0] = 0.5 * (best[PEER_TOPK - 1] + best[PEER_TOPK])
    aux_ref[1] = top0[0]
    aux_ref[2] = top1[0]
    aux_ref[3] = 1.0 / z


def _peer_body(st_ref, aux_ref, hnt_ref, u_ref, vt_ref, h_ref, gfin_ref, out_ref,
               thr_scr, w0_scr, e1_scr, thr_cur, w0_cur, acc_scr, *piece_scr):
    ec = pl.program_id(1)
    n_rows = PEER_CHUNK // PEER_N_KEYS
    tb = hnt_ref.shape[1]
    a_scr = piece_scr[:PEER_PARTS]
    c_scr = piece_scr[PEER_PARTS:]

    @pl.when(ec == 0)
    def _():
        grid_rows = (PEER_N_KEYS // n_rows, n_rows, tb)
        for h in range(PEER_HEADS):
            s0 = st_ref[h, 0:PEER_N_KEYS, :]
            s1 = st_ref[h, PEER_N_KEYS:2 * PEER_N_KEYS, :]
            thr_scr[h] = jnp.exp(aux_ref[0, h:h + 1, :] - s0 - aux_ref[2, h:h + 1, :]).reshape(grid_rows)
            w0_scr[h] = (jnp.exp(s0 - aux_ref[1, h:h + 1, :])
                         * (0.5 * aux_ref[3, h:h + 1, :])).reshape(grid_rows)
            e1_scr[h] = jnp.exp(s1 - aux_ref[2, h:h + 1, :])
        acc_scr[...] = jnp.zeros_like(acc_scr)

    sub = 2 * BF16_SUBLANES
    inv_sqrt2 = 1.0 / math.sqrt(2.0)
    e_part = PEER_CHUNK // PEER_PARTS
    rows_per_part = n_rows // PEER_PARTS

    z = lax.shift_right_logical(ec, 30)

    for h in range(PEER_HEADS):
        thr_cur[h] = thr_scr[h, ec]
        w0_cur[h] = w0_scr[h, ec]

    def scores(p):
        a_scr[p][z] = jnp.dot(u_ref[p * e_part:(p + 1) * e_part, :], hnt_ref[...],
                              preferred_element_type=F32)

    def coefficients(p):
        for r in range(rows_per_part):
            ii = p * rows_per_part + r
            for lt in range(tb // LANES):
                ls = slice(lt * LANES, (lt + 1) * LANES)
                thr_b = [thr_cur[h, ii:ii + 1, ls] for h in range(PEER_HEADS)]
                w0_b = [w0_cur[h, ii:ii + 1, ls] for h in range(PEER_HEADS)]
                for jb in range(PEER_N_KEYS // sub):
                    rows = slice(r * PEER_N_KEYS + jb * sub, r * PEER_N_KEYS + (jb + 1) * sub)
                    a = a_scr[p][z, rows, ls]
                    act = a * (1.0 + lax.erf(a * inv_sqrt2))
                    g = None
                    for h in range(PEER_HEADS):
                        e1 = e1_scr[h, jb * sub:(jb + 1) * sub, ls]
                        term = jnp.where(e1 >= thr_b[h], e1, 0.0) * w0_b[h]
                        g = term if g is None else g + term
                    c_scr[p][z, rows, ls] = (g * act).astype(BF16)

    def values(p):
        acc_scr[z] += jnp.dot(vt_ref[p], c_scr[p][z], preferred_element_type=F32)

    for p in range(PEER_PARTS):
        scores(p)
    for p in range(PEER_PARTS):
        coefficients(p)
        values(p)

    @pl.when(ec == pl.num_programs(1) - 1)
    def _():
        out_ref[...] = _rms(h_ref[...] + acc_scr[0].T, gfin_ref[...])


def _peer(st, aux, hnt, u, vt, h, g_final):
    d, t = hnt.shape
    tb = PEER_TOKENS
    ec = PEER_CHUNK
    n_chunks = u.shape[0] // ec
    rows = ec // PEER_N_KEYS
    return pl.pallas_call(
        _peer_body,
        grid=(t // tb, n_chunks),
        in_specs=[pl.BlockSpec((PEER_HEADS, 2 * PEER_N_KEYS, tb), lambda i, j: (0, 0, i),
                               pipeline_mode=pl.Buffered(1)),
                  pl.BlockSpec((4, PEER_HEADS, tb), lambda i, j: (0, 0, i)),
                  pl.BlockSpec((d, tb), lambda i, j: (0, i), pipeline_mode=pl.Buffered(1)),
                  pl.BlockSpec((ec, d), lambda i, j: (j, 0)),
                  pl.BlockSpec((PEER_PARTS, d, ec // PEER_PARTS), lambda i, j: (j, 0, 0)),
                  pl.BlockSpec((tb, d), lambda i, j: (i, 0), pipeline_mode=pl.Buffered(1)),
                  _resident(g_final.shape)],
        out_specs=pl.BlockSpec((tb, d), lambda i, j: (i, 0)),
        out_shape=jax.ShapeDtypeStruct((t, d), F32),
        scratch_shapes=([pltpu.VMEM((PEER_HEADS, n_chunks, rows, tb), F32),
                         pltpu.VMEM((PEER_HEADS, n_chunks, rows, tb), F32),
                         pltpu.VMEM((PEER_HEADS, PEER_N_KEYS, tb), F32),
                         pltpu.VMEM((PEER_HEADS, rows, tb), F32),
                         pltpu.VMEM((PEER_HEADS, rows, tb), F32),
                         pltpu.VMEM((1, d, tb), F32)]
                        + [pltpu.VMEM((1, ec // PEER_PARTS, tb), F32)] * PEER_PARTS
                        + [pltpu.VMEM((1, ec // PEER_PARTS, tb), BF16)] * PEER_PARTS),
        compiler_params=_params(2),
        name="peer_dense",
    )(st, aux, hnt, u, vt, h, g_final)


def _rope_tables(pos):
    half = SWA_HEAD_DIM // 2
    inv = jnp.exp(-math.log(ROPE_THETA) * jnp.arange(half, dtype=F32) * (2.0 / SWA_HEAD_DIM))
    ang = pos.astype(F32)[:, None] * inv[None, :]
    cos, sin = jnp.cos(ang), jnp.sin(ang)
    reps = LANES // SWA_HEAD_DIM
    return (jnp.tile(jnp.concatenate([cos, cos], axis=1), (1, reps)),
            jnp.tile(jnp.concatenate([-sin, sin], axis=1), (1, reps)))


def _pad_heads_to_kv_half(w, axis):
    shape = w.shape
    w = w.reshape(shape[:axis] + (SWA_KV_HEADS, SWA_GROUP, SWA_HEAD_DIM) + shape[axis + 1:])
    parts = []
    for kv in range(SWA_KV_HEADS):
        heads = lax.index_in_dim(w, kv, axis, keepdims=False)
        pads = [(0, 0)] * heads.ndim
        pads[axis + 1] = (kv * SWA_HEAD_DIM, (SWA_KV_HEADS - 1 - kv) * SWA_HEAD_DIM)
        parts.append(jnp.pad(heads, pads))
    out = jnp.concatenate(parts, axis=axis)
    return out.reshape(shape[:axis] + (SWA_HEADS * LANES,) + shape[axis + 1:])


def kernel(x_prompt, x_sample, state_gla, cache_win_k, cache_win_v, g_attn, w_in, w_gate_up, b_gate,
           g_gla_out, attn_sinks, w_out, g_ffn, w_peer_q, peer_sub_keys, peer_u, peer_v, g_final):
    bsz, seq, d = x_prompt.shape
    dbsz, dseq, _ = x_sample.shape
    win = cache_win_k.shape[2]

    n_gla = 2 * GLA_HEADS * GLA_DK + 2 * GLA_WIDTH
    o_sq = n_gla + GLA_GATE_RANK
    w_gla = w_in[0, :, :n_gla].astype(BF16)
    ga = jnp.pad(w_in[0, :, n_gla:o_sq], ((0, 0), (0, LANES - GLA_GATE_RANK)))
    wsq = w_in[0, :, o_sq:o_sq + SWA_WIDTH]
    wkv = w_in[0, :, o_sq + SWA_WIDTH:]
    w_swa_prompt = jnp.concatenate([wsq, wkv, ga], axis=1).astype(BF16)
    w_swa_sample = jnp.concatenate([_pad_heads_to_kv_half(wsq, 1), wkv, ga], axis=1).astype(BF16)
    wgu = jnp.pad(w_gate_up[0], ((0, LANES - GLA_GATE_RANK), (0, 0)))
    bg = b_gate[0][None, :]
    g_a = g_attn[0][None, :]
    g_f = g_ffn[0][None, :]
    g_n = g_gla_out[0][None, :]
    wo_gla = w_out[0, :GLA_WIDTH].astype(BF16)
    wo_swa = w_out[0, GLA_WIDTH:]
    wo_swa_pad = _pad_heads_to_kv_half(wo_swa, 0).astype(BF16)
    wo_swa = wo_swa.astype(BF16)
    wq = w_peer_q[0].astype(BF16)
    sk0 = jnp.pad(peer_sub_keys[0, :, 0], ((0, 0), (0, 0), (0, PEER_HALF)))
    sk1 = jnp.pad(peer_sub_keys[0, :, 1], ((0, 0), (0, 0), (PEER_HALF, 0)))
    keys = jnp.concatenate([sk0, sk1], axis=1).astype(BF16)
    u = peer_u[0].astype(BF16)
    e_piece = PEER_CHUNK // PEER_PARTS
    vt = peer_v[0].reshape(PEER_EXPERTS // e_piece, e_piece, d).transpose(0, 2, 1).astype(BF16)
    sinks = attn_sinks[0]
    sink_row = jnp.tile(sinks, dseq)[None, :]

    cos_p, sin_p = _rope_tables(jnp.arange(seq))
    cos_s, sin_s = _rope_tables(PAST_LEN + jnp.arange(dseq))
    cos_s = jnp.tile(cos_s, (PROJ_ROWS // dseq, 1))
    sin_s = jnp.tile(sin_s, (PROJ_ROWS // dseq, 1))

    def ffn(h, hn, st, aux):
        return _peer(st, aux, hn, u, vt, h, g_final[None, :])

    tp = bsz * seq
    q, k, v, r, la, sq, sk, sv = _proj(x_prompt.reshape(tp, d), g_a, w_gla, w_swa_prompt, wgu, bg,
                                       cos_p, sin_p)
    b3 = lambda a: a.reshape(bsz, seq, a.shape[-1])
    o_gla, s_prompt = _gla_prompt(b3(q), b3(k), b3(v), b3(la))
    o_swa = _swa_prompt(b3(sq), b3(sk), b3(sv), sinks)
    h, hn, st, aux = _merge(o_gla.reshape(tp, GLA_WIDTH), r, o_swa.reshape(tp, SWA_WIDTH),
                            x_prompt.reshape(tp, d), g_n, wo_gla, wo_swa, g_f, wq, keys)
    y_prompt = ffn(h, hn, st, aux).reshape(bsz, seq, d)
    kv_shape = (1, bsz, WINDOW, SWA_KV_HEADS, SWA_HEAD_DIM)
    k_prompt = b3(sk)[:, seq - WINDOW:].reshape(kv_shape)
    v_prompt = b3(sv)[:, seq - WINDOW:].reshape(kv_shape)

    ts = dbsz * dseq
    q, k, v, r, la, sq, sk, sv = _proj(x_sample.reshape(ts, d), g_a, w_gla, w_swa_sample, wgu, bg,
                                       cos_s, sin_s)
    d3 = lambda a: a.reshape(dbsz, dseq, a.shape[-1])
    o_gla, s_sample = _gla_sample(d3(q), d3(k), d3(v), d3(la), state_gla[0])
    o_swa, k_sample, v_sample = _swa_sample(
        sq.reshape(dbsz, dseq * SWA_HEADS, LANES),
        cache_win_k[0].reshape(dbsz, win, LANES), cache_win_v[0].reshape(dbsz, win, LANES),
        d3(sk), d3(sv), sink_row)
    h, hn, st, aux = _merge(o_gla.reshape(ts, GLA_WIDTH), r, o_swa.reshape(ts, SWA_HEADS * LANES),
                            x_sample.reshape(ts, d), g_n, wo_gla, wo_swa_pad, g_f, wq, keys)
    y_sample = ffn(h, hn, st, aux).reshape(dbsz, dseq, d)
    ckv_shape = (1, dbsz, win, SWA_KV_HEADS, SWA_HEAD_DIM)

    return (y_prompt, y_sample, s_prompt[None], k_prompt, v_prompt, s_sample[None],
            k_sample.reshape(ckv_shape), v_sample.reshape(ckv_shape))
```
